```python
import jax, jax.numpy as jnp
from jax import lax
import numpy as np

D_MODEL = 1024
BATCH = 32
SEQ = 256
DEPTH = 2
DEC_BATCH = 8
DEC_SEQ = 1024
PAST_LEN = 512

GRID_W = 64
N_MIXERS = 2
N_CONV_LAYERS = (DEPTH + 1) // 2
N_MLA_LAYERS = DEPTH // 2
N_MOD = 6
CONV_WIDTH = 31
CONV_PAD = CONV_WIDTH // 2
N_HEADS = 16
QK_NOPE_DIM = 128
ROPE_DIM = 64
QK_HEAD_DIM = QK_NOPE_DIM + ROPE_DIM
V_HEAD_DIM = 128
Q_LORA_RANK = 512
KV_LORA_RANK = 256
AXIS_ROPE_DIM = ROPE_DIM // 2
ROPE_BASE = 10000.0
Q_BLOCK = 128
D_FF = 2816
N_EXPERTS = 8
TOP_K = 2
D_FF_EXPERT = 1536
EPS = 1e-6
F32 = jnp.float32

kernel_name = 'hybrid_conv_mla_diffusion_step'


def rms_norm(x, g):
    x32 = x.astype(F32)
    y = x32 * lax.rsqrt(jnp.mean(x32 * x32, axis=-1, keepdims=True) + EPS)
    return (y * g.astype(F32)).astype(x.dtype)


def layer_norm(x, g, b):
    x32 = x.astype(F32)
    mu = jnp.mean(x32, axis=-1, keepdims=True)
    xc = x32 - mu
    y = xc * lax.rsqrt(jnp.mean(xc * xc, axis=-1, keepdims=True) + EPS)
    return (y * g.astype(F32) + b.astype(F32)).astype(x.dtype)


def adaln(cond, w, b):
    m = jax.nn.silu(cond) @ w + b
    return jnp.split(m[:, None, :], N_MOD, axis=-1)


def modulate(h, shift, scale):
    return h * (1 + scale) + shift


def axial_rope_tables(n_tokens):
    rows = n_tokens // GRID_W
    row = jnp.repeat(jnp.arange(rows), GRID_W).astype(F32)
    col = jnp.tile(jnp.arange(GRID_W), rows).astype(F32)
    inv = ROPE_BASE ** (-jnp.arange(0, AXIS_ROPE_DIM, 2, dtype=F32) / AXIS_ROPE_DIM)
    ang = jnp.stack([row[:, None] * inv, col[:, None] * inv], axis=1)
    return jnp.cos(ang), jnp.sin(ang)


def apply_axial_rope(x, cos, sin):
    xr = x.reshape(x.shape[:-1] + (2, 2, AXIS_ROPE_DIM // 2))
    x1, x2 = xr[..., 0, :], xr[..., 1, :]
    c = cos[None, :, None].astype(x.dtype)
    s = sin[None, :, None].astype(x.dtype)
    out = jnp.stack([x1 * c - x2 * s, x1 * s + x2 * c], axis=-2)
    return out.reshape(x.shape)


def rope_tail(t, cos, sin):
    return jnp.concatenate([t[..., :QK_NOPE_DIM], apply_axial_rope(t[..., QK_NOPE_DIM:], cos, sin)], axis=-1)


def conv_module(h, pw1, dw, dw_b, ln_g, ln_b, pw2):
    a, g = jnp.split(h @ pw1, 2, axis=-1)
    u = a * jax.nn.sigmoid(g)
    u = lax.conv_general_dilated(u, dw[:, None, :], window_strides=(1,), padding=[(CONV_PAD, CONV_PAD)],
                                 dimension_numbers=('NWC', 'WIO', 'NWC'), feature_group_count=D_MODEL) + dw_b
    u = jax.nn.silu(layer_norm(u, ln_g, ln_b))
    return u @ pw2


def mla_query(h, wdq, q_norm_g, wuq, q_qk_g):
    b, s, _ = h.shape
    cq = rms_norm(h @ wdq, q_norm_g)
    q = (cq @ wuq).reshape(b, s, N_HEADS, QK_HEAD_DIM)
    return rms_norm(q, q_qk_g)


def mla_compress(h, wdkv, kv_norm_g):
    kv_a = h @ wdkv
    ckv = rms_norm(kv_a[..., :KV_LORA_RANK], kv_norm_g)
    krope = kv_a[..., KV_LORA_RANK:]
    return ckv, krope


def mla_expand(ckv, krope, wukv, k_qk_g):
    b, s, _ = ckv.shape
    kv = (ckv @ wukv).reshape(b, s, N_HEADS, QK_NOPE_DIM + V_HEAD_DIM)
    k_nope, v = kv[..., :QK_NOPE_DIM], kv[..., QK_NOPE_DIM:]
    k = jnp.concatenate([k_nope, jnp.broadcast_to(krope[:, :, None, :], (b, s, N_HEADS, ROPE_DIM))], axis=-1)
    return rms_norm(k, k_qk_g), v


def attention(q, k, v):
    b, sq, h, dk = q.shape
    nb = sq // Q_BLOCK
    qb = q.reshape(b, nb, Q_BLOCK, h, dk).transpose(1, 0, 2, 3, 4)
    scale = QK_HEAD_DIM ** -0.5

    def block(qi):
        s = jnp.einsum('bqhd,bkhd->bhqk', qi, k).astype(F32) * scale
        p = jax.nn.softmax(s, axis=-1).astype(v.dtype)
        return jnp.einsum('bhqk,bkhd->bqhd', p, v)

    o = lax.map(block, qb)
    return o.transpose(1, 0, 2, 3, 4).reshape(b, sq, h * V_HEAD_DIM)


def swiglu(t, w1, w3, w2):
    return (jax.nn.silu(t @ w1) * (t @ w3)) @ w2


def moe_swiglu(h, w_router, w1, w3, w2):
    b, s, d = h.shape
    t = h.reshape(b * s, d)
    logits = (t @ w_router).astype(F32)
    top_v, top_i = lax.top_k(logits, TOP_K)
    top_w = jax.nn.softmax(top_v, axis=-1)
    gates = jnp.sum(jax.nn.one_hot(top_i, N_EXPERTS, dtype=F32) * top_w[..., None], axis=-2).astype(t.dtype)
    out = jnp.zeros_like(t)
    for e in range(N_EXPERTS):
        out = out + gates[:, e:e + 1] * swiglu(t, w1[e], w3[e], w2[e])
    return out.reshape(b, s, d)


def setup_inputs(seed: int = 0) -> dict:
    key = jax.random.key(seed)
    ks = iter(jax.random.split(key, 48))
    D = D_MODEL
    NC = N_CONV_LAYERS
    NM = N_MLA_LAYERS

    def w(shape, fan_in, mult=1.0):
        return jax.random.normal(next(ks), shape, F32) * (mult * fan_in ** -0.5)

    def gain(shape):
        return 1.0 + 0.1 * jax.random.normal(next(ks), shape, F32)

    def bias(shape):
        return 0.02 * jax.random.normal(next(ks), shape, F32)

    def nrm(shape):
        return jax.random.normal(next(ks), shape, F32)

    return {
        'x_prompt': nrm((BATCH, SEQ, D)),
        'x_sample': nrm((DEC_BATCH, DEC_SEQ, D)),
        'c': nrm((DEC_BATCH, D)),
        'cache_ckv': nrm((DEC_BATCH, NM, PAST_LEN, KV_LORA_RANK)),
        'cache_krope': nrm((DEC_BATCH, NM, PAST_LEN, ROPE_DIM)),
        'c_ctx': nrm((D,)),
        'ada_w': w((DEPTH, D, N_MOD * D), D, 0.5),
        'ada_b': bias((DEPTH, N_MOD * D)),
        'norm1_g': gain((DEPTH, D)),
        'norm2_g': gain((DEPTH, D)),
        'conv_pw1': w((NC, D, 2 * D), D),
        'conv_dw': w((NC, CONV_WIDTH, D), CONV_WIDTH),
        'conv_dw_b': bias((NC, D)),
        'conv_ln_g': gain((NC, D)),
        'conv_ln_b': bias((NC, D)),
        'conv_pw2': w((NC, D, D), D),
        'ffn_w1': w((NC, D, D_FF), D),
        'ffn_w3': w((NC, D, D_FF), D),
        'ffn_w2': w((NC, D_FF, D), D_FF),
        'mla_wdq': w((NM, D, Q_LORA_RANK), D),
        'mla_q_norm_g': gain((NM, Q_LORA_RANK)),
        'mla_wuq': w((NM, Q_LORA_RANK, N_HEADS * QK_HEAD_DIM), Q_LORA_RANK),
        'mla_wdkv': w((NM, D, KV_LORA_RANK + ROPE_DIM), D),
        'mla_kv_norm_g': gain((NM, KV_LORA_RANK)),
        'mla_wukv': w((NM, KV_LORA_RANK, N_HEADS * (QK_NOPE_DIM + V_HEAD_DIM)), KV_LORA_RANK),
        'mla_q_qk_g': gain((NM, QK_HEAD_DIM)),
        'mla_k_qk_g': gain((NM, QK_HEAD_DIM)),
        'mla_wo': w((NM, N_HEADS * V_HEAD_DIM, D), N_HEADS * V_HEAD_DIM),
        'moe_router': w((NM, D, N_EXPERTS), D),
        'moe_w1': w((NM, N_EXPERTS, D, D_FF_EXPERT), D),
        'moe_w3': w((NM, N_EXPERTS, D, D_FF_EXPERT), D),
        'moe_w2': w((NM, N_EXPERTS, D_FF_EXPERT, D), D_FF_EXPERT),
    }


def reference(x_prompt, x_sample, c, cache_ckv, cache_krope, c_ctx, ada_w, ada_b, norm1_g, norm2_g,
              conv_pw1, conv_dw, conv_dw_b, conv_ln_g, conv_ln_b, conv_pw2, ffn_w1, ffn_w3, ffn_w2,
              mla_wdq, mla_q_norm_g, mla_wuq, mla_wdkv, mla_kv_norm_g, mla_wukv, mla_q_qk_g, mla_k_qk_g,
              mla_wo, moe_router, moe_w1, moe_w3, moe_w2):
    cos, sin = axial_rope_tables(x_sample.shape[1])
    xp, xs = x_prompt, x_sample
    new_ckv, new_krope = [], []
    for i in range(DEPTH):
        j = i // N_MIXERS
        mp = adaln(c_ctx[None, :], ada_w[i], ada_b[i])
        ms = adaln(c, ada_w[i], ada_b[i])
        hp = modulate(rms_norm(xp, norm1_g[i]), mp[0], mp[1])
        hs = modulate(rms_norm(xs, norm1_g[i]), ms[0], ms[1])
        if i % N_MIXERS == 0:
            op = conv_module(hp, conv_pw1[j], conv_dw[j], conv_dw_b[j], conv_ln_g[j], conv_ln_b[j], conv_pw2[j])
            os_ = conv_module(hs, conv_pw1[j], conv_dw[j], conv_dw_b[j], conv_ln_g[j], conv_ln_b[j], conv_pw2[j])
        else:
            ckv_p, krope_p = mla_compress(hp, mla_wdkv[j], mla_kv_norm_g[j])
            new_ckv.append(ckv_p)
            new_krope.append(krope_p)
            kp, vp = mla_expand(ckv_p, krope_p, mla_wukv[j], mla_k_qk_g[j])
            qp = mla_query(hp, mla_wdq[j], mla_q_norm_g[j], mla_wuq[j], mla_q_qk_g[j])
            op = attention(qp, kp, vp) @ mla_wo[j]
            kc, vc = mla_expand(cache_ckv[:, j], cache_krope[:, j], mla_wukv[j], mla_k_qk_g[j])
            ckv_s, krope_s = mla_compress(hs, mla_wdkv[j], mla_kv_norm_g[j])
            ks, vs = mla_expand(ckv_s, krope_s, mla_wukv[j], mla_k_qk_g[j])
            ks = rope_tail(ks, cos, sin)
            qs = rope_tail(mla_query(hs, mla_wdq[j], mla_q_norm_g[j], mla_wuq[j], mla_q_qk_g[j]), cos, sin)
            os_ = attention(qs, jnp.concatenate([ks, kc], axis=1), jnp.concatenate([vs, vc], axis=1)) @ mla_wo[j]
        xp = xp + mp[2] * op
        xs = xs + ms[2] * os_
        hp = modulate(rms_norm(xp, norm2_g[i]), mp[3], mp[4])
        hs = modulate(rms_norm(xs, norm2_g[i]), ms[3], ms[4])
        if i % 2 == 0:
            fp = swiglu(hp, ffn_w1[j], ffn_w3[j], ffn_w2[j])
            fs = swiglu(hs, ffn_w1[j], ffn_w3[j], ffn_w2[j])
        else:
            fp = moe_swiglu(hp, moe_router[j], moe_w1[j], moe_w3[j], moe_w2[j])
            fs = moe_swiglu(hs, moe_router[j], moe_w1[j], moe_w3[j], moe_w2[j])
        xp = xp + mp[5] * fp
        xs = xs + ms[5] * fs
    return (xp, xs, jnp.stack(new_ckv, axis=1), jnp.stack(new_krope, axis=1))
```

```python
import functools

import jax
import jax.numpy as jnp
from jax import lax
from jax.experimental import pallas as pl
from jax.experimental.pallas import tpu as pltpu

D_MODEL = 1024
BATCH = 32
SEQ = 256
DEC_BATCH = 8
DEC_SEQ = 1024
PAST_LEN = 512
GRID_W = 64
N_MOD = 6
CONV_WIDTH = 31
CONV_PAD = CONV_WIDTH // 2
N_HEADS = 16
QK_NOPE_DIM = 128
ROPE_DIM = 64
QK_HEAD_DIM = QK_NOPE_DIM + ROPE_DIM
V_HEAD_DIM = 128
Q_LORA_RANK = 512
KV_LORA_RANK = 256
AXIS_ROPE_DIM = ROPE_DIM // 2
ROPE_BASE = 10000.0
D_FF = 2816
N_EXPERTS = 8
D_FF_EXPERT = 1536
EPS = 1e-6
F32 = jnp.float32
BF16 = jnp.bfloat16

NP_TOK = BATCH * SEQ
NS_TOK = DEC_BATCH * DEC_SEQ
N_TOK = NP_TOK + NS_TOK
MOD_ROWS = 16
LANE = 128
HEAD_PAD = 2 * LANE
VMEM_LIMIT = 56 * 1024 * 1024


def _cparams(*sem):
    return pltpu.CompilerParams(dimension_semantics=sem, vmem_limit_bytes=VMEM_LIMIT)


def _mod_row(tile, tm):
    start = tile * tm
    return jnp.where(start < NP_TOK, 0, 1 + (start - NP_TOK) // DEC_SEQ)


def _mod_spec(layer, tm):
    return pl.BlockSpec((None, None, 1, N_MOD * D_MODEL),
                        lambda i, *_: (layer, _mod_row(i, tm), 0, 0))


def _mod_part(mod, k):
    return mod[:, k * D_MODEL:(k + 1) * D_MODEL]


def _rms(x, g):
    return x * lax.rsqrt(jnp.mean(x * x, axis=-1, keepdims=True) + EPS) * g


def _silu(x):
    return x * jax.nn.sigmoid(x)


def _split_bf16(x):
    hi = x.astype(BF16)
    lo = (x - hi.astype(F32)).astype(BF16)
    return hi, lo


def _dot(a, b):
    return jnp.dot(a, b, preferred_element_type=F32)


def _dot3(a, b):
    ah, al = _split_bf16(a)
    bh, bl = _split_bf16(b)
    return _dot(ah, bh) + (_dot(al, bh) + _dot(ah, bl))


def _adaln_kernel(cond_ref, w_ref, b_ref, o_ref):
    o_ref[...] = _dot3(_silu(cond_ref[...]), w_ref[...]) + b_ref[...]


def _adaln(cond16, ada_w, ada_b):
    depth = ada_w.shape[0]
    tn = 1536
    return pl.pallas_call(
        _adaln_kernel,
        grid=(depth, N_MOD * D_MODEL // tn),
        in_specs=[
            pl.BlockSpec((MOD_ROWS, D_MODEL), lambda l, j: (0, 0)),
            pl.BlockSpec((None, D_MODEL, tn), lambda l, j: (l, 0, j)),
            pl.BlockSpec((None, 1, tn), lambda l, j: (l, 0, j)),
        ],
        out_specs=pl.BlockSpec((None, MOD_ROWS, tn), lambda l, j: (l, 0, j)),
        out_shape=jax.ShapeDtypeStruct((depth, MOD_ROWS, N_MOD * D_MODEL), F32),
        compiler_params=_cparams("arbitrary", "arbitrary"),
        name="adaln",
    )(cond16, ada_w, ada_b.reshape(depth, 1, N_MOD * D_MODEL))


def _glu_kernel(x_ref, mod_ref, g_ref, w_ref, u_ref):
    mod = mod_ref[...]
    h = _rms(x_ref[...], g_ref[...]) * (1.0 + _mod_part(mod, 1)) + _mod_part(mod, 0)
    ag = _dot(h.astype(BF16), w_ref[...])
    u_ref[...] = ag[:, :D_MODEL] * jax.nn.sigmoid(ag[:, D_MODEL:])


def _glu(x, mods, g, pw1, layer):
    tm = 512
    return pl.pallas_call(
        _glu_kernel,
        grid=(N_TOK // tm,),
        in_specs=[
            pl.BlockSpec((tm, D_MODEL), lambda i: (i, 0)),
            _mod_spec(layer, tm),
            pl.BlockSpec((1, D_MODEL), lambda i: (0, 0)),
            pl.BlockSpec((D_MODEL, 2 * D_MODEL), lambda i: (0, 0)),
        ],
        out_specs=pl.BlockSpec((tm, D_MODEL), lambda i: (i, 0)),
        out_shape=jax.ShapeDtypeStruct((N_TOK, D_MODEL), F32),
        compiler_params=_cparams("arbitrary"),
        name="glu",
    )(x, mods, g, pw1)


CONV_CHUNK = 256
CONV_HALO = 16
CONV_ROWS = 64


def _conv_kernel(uc_ref, up_ref, un_ref, dw_ref, dwb_ref, lng_ref, lnb_ref, w_ref, x_ref, mod_ref,
                 o_ref, pad_ref, conv_ref):
    i = pl.program_id(0)
    start = i * CONV_CHUNK
    seq_len = jnp.where(start < NP_TOK, SEQ, DEC_SEQ)
    off = jnp.where(start < NP_TOK, start, start - NP_TOK) % seq_len
    prev_ok = off > 0
    next_ok = off + CONV_CHUNK < seq_len
    pad_ref[0:CONV_HALO, :] = jnp.where(prev_ok, up_ref[...], 0.0)
    pad_ref[CONV_HALO:CONV_HALO + CONV_CHUNK, :] = uc_ref[...]
    pad_ref[CONV_HALO + CONV_CHUNK:, :] = jnp.where(next_ok, un_ref[...], 0.0)

    base = CONV_HALO - CONV_PAD
    for c in range(D_MODEL // LANE):
        cs = slice(c * LANE, (c + 1) * LANE)
        wcol = dw_ref[:, cs]
        bias = dwb_ref[:, cs]
        for r in range(CONV_CHUNK // CONV_ROWS):
            acc = jnp.broadcast_to(bias, (CONV_ROWS, LANE))
            for k in range(CONV_WIDTH):
                lo = r * CONV_ROWS + base + k
                acc = acc + wcol[k:k + 1, :] * pad_ref[lo:lo + CONV_ROWS, cs]
            conv_ref[r * CONV_ROWS:(r + 1) * CONV_ROWS, cs] = acc

    t = conv_ref[...]
    mu = jnp.mean(t, axis=-1, keepdims=True)
    tc = t - mu
    y = tc * lax.rsqrt(jnp.mean(tc * tc, axis=-1, keepdims=True) + EPS) * lng_ref[...] + lnb_ref[...]
    res = _dot(_silu(y).astype(BF16), w_ref[...])
    o_ref[...] = x_ref[...] + _mod_part(mod_ref[...], 2) * res


def _conv(u, x, mods, dw, dwb, lng, lnb, pw2, layer):
    n_chunks = N_TOK // CONV_CHUNK
    halo_per_chunk = CONV_CHUNK // CONV_HALO
    n_halo = N_TOK // CONV_HALO
    row = lambda i: (i, 0)
    const = lambda i: (0, 0)
    return pl.pallas_call(
        _conv_kernel,
        grid=(n_chunks,),
        in_specs=[
            pl.BlockSpec((CONV_CHUNK, D_MODEL), row),
            pl.BlockSpec((CONV_HALO, D_MODEL), lambda i: (jnp.maximum(i * halo_per_chunk - 1, 0), 0)),
            pl.BlockSpec((CONV_HALO, D_MODEL),
                         lambda i: (jnp.minimum((i + 1) * halo_per_chunk, n_halo - 1), 0)),
            pl.BlockSpec((CONV_WIDTH + 1, D_MODEL), const),
            pl.BlockSpec((1, D_MODEL), const),
            pl.BlockSpec((1, D_MODEL), const),
            pl.BlockSpec((1, D_MODEL), const),
            pl.BlockSpec((D_MODEL, D_MODEL), const),
            pl.BlockSpec((CONV_CHUNK, D_MODEL), row),
            _mod_spec(layer, CONV_CHUNK),
        ],
        out_specs=pl.BlockSpec((CONV_CHUNK, D_MODEL), row),
        out_shape=jax.ShapeDtypeStruct((N_TOK, D_MODEL), F32),
        scratch_shapes=[
            pltpu.VMEM((CONV_CHUNK + 2 * CONV_HALO, D_MODEL), F32),
            pltpu.VMEM((CONV_CHUNK, D_MODEL), F32),
        ],
        compiler_params=_cparams("arbitrary"),
        name="conv",
    )(u, u, u, dw, dwb, lng, lnb, pw2, x, mods)


def _ffn_kernel(x_ref, mod_ref, g_ref, w1_ref, w3_ref, w2_ref, o_ref, h_ref, acc_ref):
    j = pl.program_id(1)

    @pl.when(j == 0)
    def _():
        mod = mod_ref[...]
        h = _rms(x_ref[...], g_ref[...]) * (1.0 + _mod_part(mod, 4)) + _mod_part(mod, 3)
        h_ref[...] = h.astype(BF16)

    h = h_ref[...]
    t = _silu(_dot(h, w1_ref[...])) * _dot(h, w3_ref[...])
    y = _dot(t.astype(BF16), w2_ref[...])

    @pl.when(j == 0)
    def _():
        acc_ref[...] = y

    @pl.when(j > 0)
    def _():
        acc_ref[...] += y

    @pl.when(j == pl.num_programs(1) - 1)
    def _():
        o_ref[...] = x_ref[...] + _mod_part(mod_ref[...], 5) * acc_ref[...]


def _ffn(x, mods, g, w1, w3, w2, layer):
    tm = 512
    tf = D_FF // 2
    return pl.pallas_call(
        _ffn_kernel,
        grid=(N_TOK // tm, D_FF // tf),
        in_specs=[
            pl.BlockSpec((tm, D_MODEL), lambda i, j: (i, 0)),
            _mod_spec(layer, tm),
            pl.BlockSpec((1, D_MODEL), lambda i, j: (0, 0)),
            pl.BlockSpec((D_MODEL, tf), lambda i, j: (0, j)),
            pl.BlockSpec((D_MODEL, tf), lambda i, j: (0, j)),
            pl.BlockSpec((tf, D_MODEL), lambda i, j: (j, 0)),
        ],
        out_specs=pl.BlockSpec((tm, D_MODEL), lambda i, j: (i, 0)),
        out_shape=jax.ShapeDtypeStruct((N_TOK, D_MODEL), F32),
        scratch_shapes=[pltpu.VMEM((tm, D_MODEL), BF16), pltpu.VMEM((tm, D_MODEL), F32)],
        compiler_params=_cparams("arbitrary", "arbitrary"),
        name="ffn",
    )(x, mods, g, w1, w3, w2)


def _mla_down_kernel(x_ref, mod_ref, g_ref, w_ref, qg_ref, kvg_ref, cq_ref, ckv_ref, kr_ref):
    mod = mod_ref[...]
    h = _rms(x_ref[...], g_ref[...]) * (1.0 + _mod_part(mod, 1)) + _mod_part(mod, 0)
    d = _dot(h.astype(BF16), w_ref[...])
    cq_ref[...] = _rms(d[:, :Q_LORA_RANK], qg_ref[...]).astype(BF16)
    ckv_ref[...] = _rms(d[:, Q_LORA_RANK:Q_LORA_RANK + KV_LORA_RANK], kvg_ref[...])
    kr_ref[...] = d[:, Q_LORA_RANK + KV_LORA_RANK:]


def _mla_down(x, mods, g, w_down, qg, kvg, layer):
    tm = 512
    n_down = Q_LORA_RANK + KV_LORA_RANK + LANE
    return pl.pallas_call(
        _mla_down_kernel,
        grid=(N_TOK // tm,),
        in_specs=[
            pl.BlockSpec((tm, D_MODEL), lambda i: (i, 0)),
            _mod_spec(layer, tm),
            pl.BlockSpec((1, D_MODEL), lambda i: (0, 0)),
            pl.BlockSpec((D_MODEL, n_down), lambda i: (0, 0)),
            pl.BlockSpec((1, Q_LORA_RANK), lambda i: (0, 0)),
            pl.BlockSpec((1, KV_LORA_RANK), lambda i: (0, 0)),
        ],
        out_specs=[
            pl.BlockSpec((tm, Q_LORA_RANK), lambda i: (i, 0)),
            pl.BlockSpec((tm, KV_LORA_RANK), lambda i: (i, 0)),
            pl.BlockSpec((tm, LANE), lambda i: (i, 0)),
        ],
        out_shape=[
            jax.ShapeDtypeStruct((N_TOK, Q_LORA_RANK), BF16),
            jax.ShapeDtypeStruct((N_TOK, KV_LORA_RANK), F32),
            jax.ShapeDtypeStruct((N_TOK, LANE), F32),
        ],
        compiler_params=_cparams("arbitrary"),
        name="mla_down",
    )(x, mods, g, w_down, qg, kvg)


ATTN_TQ = 256


def _attn_kernel(*refs, n_new, n_cache, rope):
    it = iter(refs)
    cq_ref, ckv_ref, kr_ref = next(it), next(it), next(it)
    if n_cache:
        cckv_ref, ckr_ref = next(it), next(it)
    wuq_ref, wukv_ref, gq_ref, gk_ref = next(it), next(it), next(it), next(it)
    if rope:
        cos_ref, sin_ref = next(it), next(it)
    o_ref, k_scr, v_scr, o_scr = next(it), next(it), next(it), next(it)

    inv_dim = 1.0 / QK_HEAD_DIM
    gq, gk = gq_ref[...], gk_ref[...]

    def normed(nope, rot2, g, tables, out_scale):
        ssq = jnp.sum(nope * nope, axis=-1, keepdims=True) + 0.5 * jnp.sum(rot2 * rot2, axis=-1, keepdims=True)
        r = lax.rsqrt(ssq * inv_dim + EPS) * out_scale
        if tables is None:
            rot = rot2 * g[1:2, :]
        else:
            cos, sin = tables
            rot = rot2 * (g[1:2, :] * cos) + pltpu.roll(rot2, ROPE_DIM, 1) * (g[2:3, :] * sin)
        return (nope * r * g[0:1, :]).astype(BF16), (rot * r).astype(BF16)

    def head(h, carry):
        wukv = wukv_ref[h]
        kv = _dot(ckv_ref[...].astype(BF16), wukv)
        tables = (cos_ref[...], sin_ref[...]) if rope else None
        kn, kr = normed(kv[:, :LANE], kr_ref[...], gk, tables, 1.0)
        k_scr[0:n_new, 0:LANE] = kn
        k_scr[0:n_new, LANE:] = kr
        v_scr[0:n_new, :] = kv[:, LANE:].astype(BF16)
        if n_cache:
            kvc = _dot(cckv_ref[...].astype(BF16), wukv)
            kn, kr = normed(kvc[:, :LANE], ckr_ref[...], gk, None, 1.0)
            k_scr[n_new:, 0:LANE] = kn
            k_scr[n_new:, LANE:] = kr
            v_scr[n_new:, :] = kvc[:, LANE:].astype(BF16)
        wuq = wuq_ref[h]

        def qblock(b, c):
            r0 = pl.multiple_of(b * ATTN_TQ, ATTN_TQ)
            q = _dot(cq_ref[pl.ds(r0, ATTN_TQ), :], wuq)
            tabs = (cos_ref[pl.ds(r0, ATTN_TQ), :], sin_ref[pl.ds(r0, ATTN_TQ), :]) if rope else None
            qn, qr = normed(q[:, :LANE], q[:, LANE:], gq, tabs, QK_HEAD_DIM ** -0.5)
            qf = jnp.concatenate([qn, qr], axis=-1)
            s = lax.dot_general(qf, k_scr[...], (((1,), (1,)), ((), ())), preferred_element_type=F32)
            p = jnp.exp(s - jnp.max(s, axis=-1, keepdims=True))
            l = jnp.sum(p, axis=-1, keepdims=True)
            o = _dot(p.astype(BF16), v_scr[...])
            o_scr[h, pl.ds(r0, ATTN_TQ), :] = (o / l).astype(BF16)
            return c

        return lax.fori_loop(0, n_new // ATTN_TQ, qblock, carry)

    lax.fori_loop(0, N_HEADS, head, 0)
    for h in range(N_HEADS):
        o_ref[:, h * V_HEAD_DIM:(h + 1) * V_HEAD_DIM] = o_scr[h]


def _attention(cq, ckv, kr2, cache, wuq, wukv, gq, gk, tables, *, tok0, n_batch, n_new):
    b0 = tok0 // n_new
    n_cache = 0 if cache is None else cache[0].shape[1]
    rope = tables is not None
    row = lambda b: (b0 + b, 0)
    const2 = lambda b: (0, 0)
    const3 = lambda b: (0, 0, 0)
    in_specs = [
        pl.BlockSpec((n_new, Q_LORA_RANK), row),
        pl.BlockSpec((n_new, KV_LORA_RANK), row),
        pl.BlockSpec((n_new, LANE), row),
    ]
    args = [cq, ckv, kr2]
    if n_cache:
        in_specs += [pl.BlockSpec((None, n_cache, KV_LORA_RANK), lambda b: (b, 0, 0)),
                     pl.BlockSpec((None, n_cache, LANE), lambda b: (b, 0, 0))]
        args += list(cache)
    in_specs += [
        pl.BlockSpec((N_HEADS, Q_LORA_RANK, HEAD_PAD), const3),
        pl.BlockSpec((N_HEADS, KV_LORA_RANK, HEAD_PAD), const3),
        pl.BlockSpec((8, LANE), const2),
        pl.BlockSpec((8, LANE), const2),
    ]
    args += [wuq, wukv, gq, gk]
    if rope:
        in_specs += [pl.BlockSpec((n_new, LANE), const2), pl.BlockSpec((n_new, LANE), const2)]
        args += list(tables)
    n_keys = n_new + n_cache
    return pl.pallas_call(
        functools.partial(_attn_kernel, n_new=n_new, n_cache=n_cache, rope=rope),
        grid=(n_batch,),
        in_specs=in_specs,
        out_specs=pl.BlockSpec((n_new, N_HEADS * V_HEAD_DIM), lambda b: (b, 0)),
        out_shape=jax.ShapeDtypeStruct((n_batch * n_new, N_HEADS * V_HEAD_DIM), BF16),
        scratch_shapes=[
            pltpu.VMEM((n_keys, HEAD_PAD), BF16),
            pltpu.VMEM((n_keys, V_HEAD_DIM), BF16),
            pltpu.VMEM((N_HEADS, n_new, V_HEAD_DIM), BF16),
        ],
        compiler_params=_cparams("arbitrary"),
        name="attn_rope" if rope else "attn",
    )(*args)


def _attn_out_kernel(o_ref, x_ref, mod_ref, g_ref, wo_ref, wr_ref, x3_ref, h_ref, gates_ref):
    mod = mod_ref[...]
    x3 = x_ref[...] + _mod_part(mod, 2) * _dot(o_ref[...], wo_ref[...])
    x3_ref[...] = x3
    h = _rms(x3, g_ref[...]) * (1.0 + _mod_part(mod, 4)) + _mod_part(mod, 3)
    hb = h.astype(BF16)
    h_ref[...] = hb
    wh, wl = _split_bf16(wr_ref[...])
    hl = (h - hb.astype(F32)).astype(BF16)
    logits = _dot(hb, wh) + (_dot(hl, wh) + _dot(hb, wl))
    lane = lax.broadcasted_iota(jnp.int32, logits.shape, 1)
    neg = jnp.float32(-jnp.inf)
    logits = jnp.where(lane < N_EXPERTS, logits, neg)
    v1 = jnp.max(logits, axis=-1, keepdims=True)
    i1 = jnp.min(jnp.where(logits == v1, lane, LANE), axis=-1, keepdims=True)
    rest = jnp.where(lane == i1, neg, logits)
    v2 = jnp.max(rest, axis=-1, keepdims=True)
    i2 = jnp.min(jnp.where(rest == v2, lane, LANE), axis=-1, keepdims=True)
    e2 = jnp.exp(v2 - v1)
    w1 = 1.0 / (1.0 + e2)
    gates_ref[...] = jnp.where(lane == i1, w1, 0.0) + jnp.where(lane == i2, e2 * w1, 0.0)


def _attn_out(o, x, mods, g, wo, wr, layer):
    tm = 512
    return pl.pallas_call(
        _attn_out_kernel,
        grid=(N_TOK // tm,),
        in_specs=[
            pl.BlockSpec((tm, N_HEADS * V_HEAD_DIM), lambda i: (i, 0)),
            pl.BlockSpec((tm, D_MODEL), lambda i: (i, 0)),
            _mod_spec(layer, tm),
            pl.BlockSpec((1, D_MODEL), lambda i: (0, 0)),
            pl.BlockSpec((N_HEADS * V_HEAD_DIM, D_MODEL), lambda i: (0, 0)),
            pl.BlockSpec((D_MODEL, LANE), lambda i: (0, 0)),
        ],
        out_specs=[
            pl.BlockSpec((tm, D_MODEL), lambda i: (i, 0)),
            pl.BlockSpec((tm, D_MODEL), lambda i: (i, 0)),
            pl.BlockSpec((tm, LANE), lambda i: (i, 0)),
        ],
        out_shape=[
            jax.ShapeDtypeStruct((N_TOK, D_MODEL), F32),
            jax.ShapeDtypeStruct((N_TOK, D_MODEL), BF16),
            jax.ShapeDtypeStruct((N_TOK, LANE), F32),
        ],
        compiler_params=_cparams("arbitrary"),
        name="attn_out",
    )(o, x, mods, g, wo, wr)


def _moe_kernel(h_ref, gates_ref, x_ref, mod_ref, w1_ref, w3_ref, w2_ref, o_ref, acc_ref):
    e = pl.program_id(1)
    h = h_ref[...]
    t = _silu(_dot(h, w1_ref[...])) * _dot(h, w3_ref[...])
    y = _dot(t.astype(BF16), w2_ref[...])
    gates = gates_ref[...]
    lane = lax.broadcasted_iota(jnp.int32, gates.shape, 1)
    gate = jnp.sum(jnp.where(lane == e, gates, 0.0), axis=-1, keepdims=True)

    @pl.when(e == 0)
    def _():
        acc_ref[...] = gate * y

    @pl.when(e > 0)
    def _():
        acc_ref[...] += gate * y

    @pl.when(e == pl.num_programs(1) - 1)
    def _():
        o_ref[...] = x_ref[...] + _mod_part(mod_ref[...], 5) * acc_ref[...]


def _moe(h, gates, x, mods, w1, w3, w2, layer):
    tm = 512
    return pl.pallas_call(
        _moe_kernel,
        grid=(N_TOK // tm, N_EXPERTS),
        in_specs=[
            pl.BlockSpec((tm, D_MODEL), lambda i, e: (i, 0)),
            pl.BlockSpec((tm, LANE), lambda i, e: (i, 0)),
            pl.BlockSpec((tm, D_MODEL), lambda i, e: (i, 0)),
            _mod_spec(layer, tm),
            pl.BlockSpec((None, D_MODEL, D_FF_EXPERT), lambda i, e: (e, 0, 0)),
            pl.BlockSpec((None, D_MODEL, D_FF_EXPERT), lambda i, e: (e, 0, 0)),
            pl.BlockSpec((None, D_FF_EXPERT, D_MODEL), lambda i, e: (e, 0, 0)),
        ],
        out_specs=pl.BlockSpec((tm, D_MODEL), lambda i, e: (i, 0)),
        out_shape=jax.ShapeDtypeStruct((N_TOK, D_MODEL), F32),
        scratch_shapes=[pltpu.VMEM((tm, D_MODEL), F32)],
        compiler_params=_cparams("arbitrary", "arbitrary"),
        name="moe",
    )(h, gates, x, mods, w1, w3, w2)


def _rope_partner(t):
    half = AXIS_ROPE_DIM // 2
    s = t.shape[:-1]
    return t.reshape(s + (2, 2, half))[..., ::-1, :].reshape(s + (ROPE_DIM,))


def _rope_tables(n_tokens):
    rows = n_tokens // GRID_W
    row = jnp.repeat(jnp.arange(rows), GRID_W).astype(F32)
    col = jnp.tile(jnp.arange(GRID_W), rows).astype(F32)
    inv = ROPE_BASE ** (-jnp.arange(0, AXIS_ROPE_DIM, 2, dtype=F32) / AXIS_ROPE_DIM)
    ar, ac = row[:, None] * inv, col[:, None] * inv
    cos = jnp.concatenate([jnp.cos(ar), jnp.cos(ar), jnp.cos(ac), jnp.cos(ac)], axis=-1)
    sin = jnp.concatenate([-jnp.sin(ar), jnp.sin(ar), -jnp.sin(ac), jnp.sin(ac)], axis=-1)
    zeros = jnp.zeros_like(cos)
    return jnp.concatenate([cos, zeros], axis=-1), jnp.concatenate([sin, zeros], axis=-1)


def _qk_gain_rows(g):
    z = jnp.zeros((ROPE_DIM,), F32)
    rows = jnp.stack([g[:QK_NOPE_DIM],
                      jnp.concatenate([g[QK_NOPE_DIM:], z]),
                      jnp.concatenate([_rope_partner(g[QK_NOPE_DIM:]), z])])
    return jnp.concatenate([rows, jnp.zeros((5, LANE), F32)])


def kernel(x_prompt, x_sample, c, cache_ckv, cache_krope, c_ctx, ada_w, ada_b, norm1_g, norm2_g, conv_pw1, conv_dw, conv_dw_b, conv_ln_g, conv_ln_b, conv_pw2, ffn_w1, ffn_w3, ffn_w2, mla_wdq, mla_q_norm_g, mla_wuq, mla_wdkv, mla_kv_norm_g, mla_wukv, mla_q_qk_g, mla_k_qk_g, mla_wo, moe_router, moe_w1, moe_w3, moe_w2):
    x = jnp.concatenate([x_prompt.reshape(NP_TOK, D_MODEL), x_sample.reshape(NS_TOK, D_MODEL)])
    cond16 = jnp.concatenate([c_ctx[None, :], c, jnp.zeros((MOD_ROWS - 1 - DEC_BATCH, D_MODEL), F32)])
    mods = _adaln(cond16, ada_w, ada_b).reshape(2, MOD_ROWS, 1, N_MOD * D_MODEL)
    vec = lambda a: a.reshape(1, -1)

    u = _glu(x, mods, vec(norm1_g[0]), conv_pw1[0].astype(BF16), 0)
    dw = jnp.concatenate([conv_dw[0], jnp.zeros((1, D_MODEL), F32)])
    x = _conv(u, x, mods, dw, vec(conv_dw_b[0]), vec(conv_ln_g[0]), vec(conv_ln_b[0]),
              conv_pw2[0].astype(BF16), 0)
    x = _ffn(x, mods, vec(norm2_g[0]), ffn_w1[0].astype(BF16), ffn_w3[0].astype(BF16),
             ffn_w2[0].astype(BF16), 0)

    wdkv = mla_wdkv[0]
    w_down = jnp.concatenate([mla_wdq[0], wdkv, _rope_partner(wdkv[:, KV_LORA_RANK:])], axis=1).astype(BF16)
    cq, ckv, kr2 = _mla_down(x, mods, vec(norm1_g[1]), w_down, vec(mla_q_norm_g[0]), vec(mla_kv_norm_g[0]), 1)

    wuq = mla_wuq[0].reshape(Q_LORA_RANK, N_HEADS, QK_HEAD_DIM)
    wuq = jnp.concatenate([wuq, _rope_partner(wuq[..., QK_NOPE_DIM:])], axis=-1)
    wuq = wuq.transpose(1, 0, 2).astype(BF16)
    wukv = mla_wukv[0].reshape(KV_LORA_RANK, N_HEADS, HEAD_PAD).transpose(1, 0, 2).astype(BF16)
    gq, gk = _qk_gain_rows(mla_q_qk_g[0]), _qk_gain_rows(mla_k_qk_g[0])
    ckr = cache_krope[:, 0]
    cache = (cache_ckv[:, 0], jnp.concatenate([ckr, _rope_partner(ckr)], axis=-1))
    o_p = _attention(cq, ckv, kr2, None, wuq, wukv, gq, gk, None, tok0=0, n_batch=BATCH, n_new=SEQ)
    o_s = _attention(cq, ckv, kr2, cache, wuq, wukv, gq, gk, _rope_tables(DEC_SEQ),
                     tok0=NP_TOK, n_batch=DEC_BATCH, n_new=DEC_SEQ)
    o = jnp.concatenate([o_p, o_s])

    wr = jnp.concatenate([moe_router[0], jnp.zeros((D_MODEL, LANE - N_EXPERTS), F32)], axis=1)
    x, h, gates = _attn_out(o, x, mods, vec(norm2_g[1]), mla_wo[0].astype(BF16), wr, 1)
    x = _moe(h, gates, x, mods, moe_w1[0].astype(BF16), moe_w3[0].astype(BF16), moe_w2[0].astype(BF16), 1)

    new_ckv = ckv[:NP_TOK].reshape(BATCH, 1, SEQ, KV_LORA_RANK)
    new_krope = kr2[:NP_TOK, :ROPE_DIM].reshape(BATCH, 1, SEQ, ROPE_DIM)
    return (x[:NP_TOK].reshape(BATCH, SEQ, D_MODEL), x[NP_TOK:].reshape(DEC_BATCH, DEC_SEQ, D_MODEL),
            new_ckv, new_krope)
```

```python
import functools

import jax
import jax.numpy as jnp
from jax import lax
from jax.experimental import pallas as pl
from jax.experimental.pallas import tpu as pltpu

D_MODEL = 1024
BATCH = 32
SEQ = 256
DEC_BATCH = 8
DEC_SEQ = 1024
PAST_LEN = 512
GRID_W = 64
N_MOD = 6
CONV_WIDTH = 31
CONV_PAD = CONV_WIDTH // 2
N_HEADS = 16
QK_NOPE_DIM = 128
ROPE_DIM = 64
QK_HEAD_DIM = QK_NOPE_DIM + ROPE_DIM
V_HEAD_DIM = 128
Q_LORA_RANK = 512
KV_LORA_RANK = 256
AXIS_ROPE_DIM = ROPE_DIM // 2
ROPE_BASE = 10000.0
D_FF = 2816
N_EXPERTS = 8
TOP_K = 2
D_FF_EXPERT = 1536
EPS = 1e-6
F32 = jnp.float32
BF16 = jnp.bfloat16
I32 = jnp.int32

NP_TOK = BATCH * SEQ
NS_TOK = DEC_BATCH * DEC_SEQ
N_TOK = NP_TOK + NS_TOK
MOD_ROWS = 16
LANE = 128
SUBLANE = 8
HEAD_PAD = 2 * LANE
VMEM_LIMIT = 56 * 1024 * 1024

TOK_CHUNK = 256
N_CHUNKS = N_TOK // TOK_CHUNK
SLOT_TILE = 512
N_SLOT_TILES = (TOP_K * N_TOK + N_EXPERTS * (SLOT_TILE - 1)) // SLOT_TILE
N_SLOTS = N_SLOT_TILES * SLOT_TILE
SLOT_CHUNK = 256
MAX_PAIRS = 2 * N_EXPERTS


def _cparams(*sem):
    return pltpu.CompilerParams(dimension_semantics=sem, vmem_limit_bytes=VMEM_LIMIT)


def _mod_row(tile, tm):
    start = tile * tm
    return jnp.where(start < NP_TOK, 0, 1 + (start - NP_TOK) // DEC_SEQ)


def _mod_spec(layer, tm):
    return pl.BlockSpec((None, None, 1, N_MOD * D_MODEL),
                        lambda i, *_: (layer, _mod_row(i, tm), 0, 0))


def _split_specs(tm):
    n_p = NP_TOK // tm
    return [pl.BlockSpec((tm, D_MODEL), lambda i, *_: (jnp.minimum(i, n_p - 1), 0)),
            pl.BlockSpec((tm, D_MODEL), lambda i, *_: (jnp.maximum(i - n_p, 0), 0))]


def _pick_tokens(tm, xp_ref, xs_ref):
    return jnp.where(pl.program_id(0) * tm < NP_TOK, xp_ref[...], xs_ref[...])


def _mod_part(mod, k):
    return mod[:, k * D_MODEL:(k + 1) * D_MODEL]


def _rms(x, g):
    return x * lax.rsqrt(jnp.mean(x * x, axis=-1, keepdims=True) + EPS) * g


def _silu(x):
    return x * jax.nn.sigmoid(x)


def _split_bf16(x):
    hi = x.astype(BF16)
    lo = (x - hi.astype(F32)).astype(BF16)
    return hi, lo


def _dot(a, b):
    return jnp.dot(a, b, preferred_element_type=F32)


def _dot3(a, b):
    ah, al = _split_bf16(a)
    bh, bl = _split_bf16(b)
    return _dot(ah, bh) + (_dot(al, bh) + _dot(ah, bl))


def _lane_pick(x, idx):
    lane = lax.broadcasted_iota(I32, x.shape, 1)
    return jnp.sum(jnp.where(lane == idx, x, 0.0), axis=-1, keepdims=True)


def _adaln_kernel(cond_ref, w_ref, b_ref, o_ref):
    o_ref[...] = _dot3(_silu(cond_ref[...]), w_ref[...]) + b_ref[...]


def _adaln(cond16, ada_w, ada_b):
    depth = ada_w.shape[0]
    tn = 1536
    return pl.pallas_call(
        _adaln_kernel,
        grid=(depth, N_MOD * D_MODEL // tn),
        in_specs=[
            pl.BlockSpec((MOD_ROWS, D_MODEL), lambda l, j: (0, 0)),
            pl.BlockSpec((None, D_MODEL, tn), lambda l, j: (l, 0, j)),
            pl.BlockSpec((None, 1, tn), lambda l, j: (l, 0, j)),
        ],
        out_specs=pl.BlockSpec((None, MOD_ROWS, tn), lambda l, j: (l, 0, j)),
        out_shape=jax.ShapeDtypeStruct((depth, MOD_ROWS, N_MOD * D_MODEL), F32),
        compiler_params=_cparams("arbitrary", "arbitrary"),
        name="adaln",
    )(cond16, ada_w, ada_b.reshape(depth, 1, N_MOD * D_MODEL))


def _glu_kernel(xp_ref, xs_ref, mod_ref, g_ref, w_ref, u_ref):
    mod = mod_ref[...]
    x = _pick_tokens(u_ref.shape[0], xp_ref, xs_ref)
    h = _rms(x, g_ref[...]) * (1.0 + _mod_part(mod, 1)) + _mod_part(mod, 0)
    ag = _dot(h.astype(BF16), w_ref[...])
    u_ref[...] = ag[:, :D_MODEL] * jax.nn.sigmoid(ag[:, D_MODEL:])


def _glu(xp, xs, mods, g, pw1, layer):
    tm = 512
    return pl.pallas_call(
        _glu_kernel,
        grid=(N_TOK // tm,),
        in_specs=_split_specs(tm) + [
            _mod_spec(layer, tm),
            pl.BlockSpec((1, D_MODEL), lambda i: (0, 0)),
            pl.BlockSpec((D_MODEL, 2 * D_MODEL), lambda i: (0, 0)),
        ],
        out_specs=pl.BlockSpec((tm, D_MODEL), lambda i: (i, 0)),
        out_shape=jax.ShapeDtypeStruct((N_TOK, D_MODEL), F32),
        compiler_params=_cparams("arbitrary"),
        name="glu",
    )(xp, xs, mods, g, pw1)


CONV_CHUNK = 256
CONV_HALO = 16
CONV_ROWS = 64


def _conv_kernel(uc_ref, up_ref, un_ref, dw_ref, dwb_ref, lng_ref, lnb_ref, w_ref, xp_ref, xs_ref, mod_ref,
                 o_ref, pad_ref, conv_ref):
    i = pl.program_id(0)
    start = i * CONV_CHUNK
    seq_len = jnp.where(start < NP_TOK, SEQ, DEC_SEQ)
    off = jnp.where(start < NP_TOK, start, start - NP_TOK) % seq_len
    prev_ok = off > 0
    next_ok = off + CONV_CHUNK < seq_len
    pad_ref[0:CONV_HALO, :] = jnp.where(prev_ok, up_ref[...], 0.0)
    pad_ref[CONV_HALO:CONV_HALO + CONV_CHUNK, :] = uc_ref[...]
    pad_ref[CONV_HALO + CONV_CHUNK:, :] = jnp.where(next_ok, un_ref[...], 0.0)

    base = CONV_HALO - CONV_PAD
    for c in range(D_MODEL // LANE):
        cs = slice(c * LANE, (c + 1) * LANE)
        wcol = dw_ref[:, cs]
        bias = dwb_ref[:, cs]
        for r in range(CONV_CHUNK // CONV_ROWS):
            acc = jnp.broadcast_to(bias, (CONV_ROWS, LANE))
            for k in range(CONV_WIDTH):
                lo = r * CONV_ROWS + base + k
                acc = acc + wcol[k:k + 1, :] * pad_ref[lo:lo + CONV_ROWS, cs]
            conv_ref[r * CONV_ROWS:(r + 1) * CONV_ROWS, cs] = acc

    t = conv_ref[...]
    mu = jnp.mean(t, axis=-1, keepdims=True)
    tc = t - mu
    y = tc * lax.rsqrt(jnp.mean(tc * tc, axis=-1, keepdims=True) + EPS) * lng_ref[...] + lnb_ref[...]
    res = _dot(_silu(y).astype(BF16), w_ref[...])
    o_ref[...] = _pick_tokens(CONV_CHUNK, xp_ref, xs_ref) + _mod_part(mod_ref[...], 2) * res


def _conv(u, xp, xs, mods, dw, dwb, lng, lnb, pw2, layer):
    n_chunks = N_TOK // CONV_CHUNK
    halo_per_chunk = CONV_CHUNK // CONV_HALO
    n_halo = N_TOK // CONV_HALO
    row = lambda i: (i, 0)
    const = lambda i: (0, 0)
    return pl.pallas_call(
        _conv_kernel,
        grid=(n_chunks,),
        in_specs=[
            pl.BlockSpec((CONV_CHUNK, D_MODEL), row),
            pl.BlockSpec((CONV_HALO, D_MODEL), lambda i: (jnp.maximum(i * halo_per_chunk - 1, 0), 0)),
            pl.BlockSpec((CONV_HALO, D_MODEL),
                         lambda i: (jnp.minimum((i + 1) * halo_per_chunk, n_halo - 1), 0)),
            pl.BlockSpec((CONV_WIDTH + 1, D_MODEL), const),
            pl.BlockSpec((1, D_MODEL), const),
            pl.BlockSpec((1, D_MODEL), const),
            pl.BlockSpec((1, D_MODEL), const),
            pl.BlockSpec((D_MODEL, D_MODEL), const),
        ] + _split_specs(CONV_CHUNK) + [
            _mod_spec(layer, CONV_CHUNK),
        ],
        out_specs=pl.BlockSpec((CONV_CHUNK, D_MODEL), row),
        out_shape=jax.ShapeDtypeStruct((N_TOK, D_MODEL), F32),
        scratch_shapes=[
            pltpu.VMEM((CONV_CHUNK + 2 * CONV_HALO, D_MODEL), F32),
            pltpu.VMEM((CONV_CHUNK, D_MODEL), F32),
        ],
        compiler_params=_cparams("arbitrary"),
        name="conv",
    )(u, u, u, dw, dwb, lng, lnb, pw2, xp, xs, mods)


def _ffn_kernel(x_ref, mod_ref, g_ref, w1_ref, w3_ref, w2_ref, o_ref, h_ref, acc_ref):
    j = pl.program_id(1)

    @pl.when(j == 0)
    def _():
        mod = mod_ref[...]
        h = _rms(x_ref[...], g_ref[...]) * (1.0 + _mod_part(mod, 4)) + _mod_part(mod, 3)
        h_ref[...] = h.astype(BF16)

    h = h_ref[...]
    t = _silu(_dot(h, w1_ref[...])) * _dot(h, w3_ref[...])
    y = _dot(t.astype(BF16), w2_ref[...])

    @pl.when(j == 0)
    def _():
        acc_ref[...] = y

    @pl.when(j > 0)
    def _():
        acc_ref[...] += y

    @pl.when(j == pl.num_programs(1) - 1)
    def _():
        o_ref[...] = x_ref[...] + _mod_part(mod_ref[...], 5) * acc_ref[...]


def _ffn(x, mods, g, w1, w3, w2, layer):
    tm = 512
    tf = D_FF // 2
    return pl.pallas_call(
        _ffn_kernel,
        grid=(N_TOK // tm, D_FF // tf),
        in_specs=[
            pl.BlockSpec((tm, D_MODEL), lambda i, j: (i, 0)),
            _mod_spec(layer, tm),
            pl.BlockSpec((1, D_MODEL), lambda i, j: (0, 0)),
            pl.BlockSpec((D_MODEL, tf), lambda i, j: (0, j)),
            pl.BlockSpec((D_MODEL, tf), lambda i, j: (0, j)),
            pl.BlockSpec((tf, D_MODEL), lambda i, j: (j, 0)),
        ],
        out_specs=pl.BlockSpec((tm, D_MODEL), lambda i, j: (i, 0)),
        out_shape=jax.ShapeDtypeStruct((N_TOK, D_MODEL), F32),
        scratch_shapes=[pltpu.VMEM((tm, D_MODEL), BF16), pltpu.VMEM((tm, D_MODEL), F32)],
        compiler_params=_cparams("arbitrary", "arbitrary"),
        name="ffn",
    )(x, mods, g, w1, w3, w2)


def _mla_down_kernel(x_ref, mod_ref, g_ref, w_ref, qg_ref, kvg_ref, cq_ref, ckv_ref, ckvb_ref, kr_ref):
    mod = mod_ref[...]
    h = _rms(x_ref[...], g_ref[...]) * (1.0 + _mod_part(mod, 1)) + _mod_part(mod, 0)
    d = _dot(h.astype(BF16), w_ref[...])
    cq_ref[...] = _rms(d[:, :Q_LORA_RANK], qg_ref[...]).astype(BF16)
    ckv = _rms(d[:, Q_LORA_RANK:Q_LORA_RANK + KV_LORA_RANK], kvg_ref[...])
    ckv_ref[...] = ckv
    ckvb_ref[...] = ckv.astype(BF16)
    kr_ref[...] = d[:, Q_LORA_RANK + KV_LORA_RANK:]


def _mla_down(x, mods, g, w_down, qg, kvg, layer):
    tm = 512
    n_down = Q_LORA_RANK + KV_LORA_RANK + LANE
    return pl.pallas_call(
        _mla_down_kernel,
        grid=(N_TOK // tm,),
        in_specs=[
            pl.BlockSpec((tm, D_MODEL), lambda i: (i, 0)),
            _mod_spec(layer, tm),
            pl.BlockSpec((1, D_MODEL), lambda i: (0, 0)),
            pl.BlockSpec((D_MODEL, n_down), lambda i: (0, 0)),
            pl.BlockSpec((1, Q_LORA_RANK), lambda i: (0, 0)),
            pl.BlockSpec((1, KV_LORA_RANK), lambda i: (0, 0)),
        ],
        out_specs=[
            pl.BlockSpec((tm, Q_LORA_RANK), lambda i: (i, 0)),
            pl.BlockSpec((tm, KV_LORA_RANK), lambda i: (i, 0)),
            pl.BlockSpec((tm, KV_LORA_RANK), lambda i: (i, 0)),
            pl.BlockSpec((tm, LANE), lambda i: (i, 0)),
        ],
        out_shape=[
            jax.ShapeDtypeStruct((N_TOK, Q_LORA_RANK), BF16),
            jax.ShapeDtypeStruct((N_TOK, KV_LORA_RANK), F32),
            jax.ShapeDtypeStruct((N_TOK, KV_LORA_RANK), BF16),
            jax.ShapeDtypeStruct((N_TOK, LANE), F32),
        ],
        compiler_params=_cparams("arbitrary"),
        name="mla_down",
    )(x, mods, g, w_down, qg, kvg)


ATTN_BLOCK = 1024
ATTN_TQ = 256
HEAD_GROUP = 2
N_HEAD_GROUPS = N_HEADS // HEAD_GROUP
LOG2E = 1.4426950408889634


def _attn_kernel(*refs, seq, n_cache, rope):
    it = iter(refs)
    cq_ref, ckv_ref, kr_ref = next(it), next(it), next(it)
    if n_cache:
        cckv_ref, ckr_ref = next(it), next(it)
    wuq_ref, wukv_ref, gq_ref, gk_ref = next(it), next(it), next(it), next(it)
    if rope:
        cos_ref, sin_ref = next(it), next(it)
    o_ref = next(it)

    inv_dim = 1.0 / QK_HEAD_DIM
    gq, gk = gq_ref[...], gk_ref[...]
    cq, ckv, kr = cq_ref[...], ckv_ref[...], kr_ref[...]
    tables = (cos_ref[...], sin_ref[...]) if rope else None
    ones_new = jnp.ones((ATTN_BLOCK, LANE), BF16)
    if n_cache:
        cckv, ckr = cckv_ref[...].astype(BF16), ckr_ref[...]
        ones_cache = jnp.ones((n_cache, LANE), BF16)

    def normed(nope, rot2, g, tabs, out_scale):
        ssq = jnp.sum(nope * nope, axis=-1, keepdims=True) + 0.5 * jnp.sum(rot2 * rot2, axis=-1, keepdims=True)
        r = lax.rsqrt(ssq * inv_dim + EPS) * out_scale
        if tabs is None:
            rot = rot2 * g[1:2, :]
        else:
            rot = rot2 * (g[1:2, :] * tabs[0]) + pltpu.roll(rot2, ROPE_DIM, 1) * (g[2:3, :] * tabs[1])
        return jnp.concatenate([(nope * r * g[0:1, :]).astype(BF16), (rot * r).astype(BF16)], axis=1)

    def one_head(h):
        wukv = wukv_ref[h]
        kv = _dot(ckv, wukv)
        k = normed(kv[:, :LANE], kr, gk, tables, 1.0)
        v = jnp.concatenate([kv[:, LANE:].astype(BF16), ones_new], axis=1)
        if n_cache:
            kvc = _dot(cckv, wukv)
            k = jnp.concatenate([k, normed(kvc[:, :LANE], ckr, gk, None, 1.0)], axis=0)
            v = jnp.concatenate([v, jnp.concatenate([kvc[:, LANE:].astype(BF16), ones_cache], axis=1)], axis=0)
        q = _dot(cq, wuq_ref[h])
        qf = normed(q[:, :LANE], q[:, LANE:], gq, tables, QK_HEAD_DIM ** -0.5 * LOG2E)
        outs = []
        for i in range(ATTN_BLOCK // ATTN_TQ):
            rows = slice(i * ATTN_TQ, (i + 1) * ATTN_TQ)
            kb, vb = (k, v) if seq == ATTN_BLOCK else (k[rows], v[rows])
            s = lax.dot_general(qf[rows], kb, (((1,), (1,)), ((), ())), preferred_element_type=F32)
            p = jnp.exp2(s - jnp.max(s, axis=-1, keepdims=True)).astype(BF16)
            oe = _dot(p, vb)
            outs.append((oe[:, :LANE] / oe[:, LANE:]).astype(BF16))
        return jnp.concatenate(outs, axis=0)

    def group(gi, carry):
        o_ref[gi] = jnp.concatenate([one_head(gi * HEAD_GROUP + j) for j in range(HEAD_GROUP)], axis=1)
        return carry

    lax.fori_loop(0, N_HEAD_GROUPS, group, 0)


def _attention(cq, ckv, kr2, cache, wuq, wukv, gq, gk, tables, *, tok0, n_tok, seq):
    assert seq in (ATTN_BLOCK, ATTN_TQ) and tok0 % ATTN_BLOCK == 0 and n_tok % ATTN_BLOCK == 0
    b0 = tok0 // ATTN_BLOCK
    n_cache = 0 if cache is None else cache[0].shape[1]
    assert n_cache == 0 or seq == ATTN_BLOCK
    rope = tables is not None
    row = lambda b: (b0 + b, 0)
    const2 = lambda b: (0, 0)
    const3 = lambda b: (0, 0, 0)
    in_specs = [
        pl.BlockSpec((ATTN_BLOCK, Q_LORA_RANK), row),
        pl.BlockSpec((ATTN_BLOCK, KV_LORA_RANK), row),
        pl.BlockSpec((ATTN_BLOCK, LANE), row),
    ]
    args = [cq, ckv, kr2]
    if n_cache:
        in_specs += [pl.BlockSpec((None, n_cache, KV_LORA_RANK), lambda b: (b, 0, 0)),
                     pl.BlockSpec((None, n_cache, LANE), lambda b: (b, 0, 0))]
        args += list(cache)
    in_specs += [
        pl.BlockSpec((N_HEADS, Q_LORA_RANK, HEAD_PAD), const3),
        pl.BlockSpec((N_HEADS, KV_LORA_RANK, HEAD_PAD), const3),
        pl.BlockSpec((SUBLANE, LANE), const2),
        pl.BlockSpec((SUBLANE, LANE), const2),
    ]
    args += [wuq, wukv, gq, gk]
    if rope:
        in_specs += [pl.BlockSpec((ATTN_BLOCK, LANE), const2), pl.BlockSpec((ATTN_BLOCK, LANE), const2)]
        args += list(tables)
    return pl.pallas_call(
        functools.partial(_attn_kernel, seq=seq, n_cache=n_cache, rope=rope),
        grid=(n_tok // ATTN_BLOCK,),
        in_specs=in_specs,
        out_specs=pl.BlockSpec((N_HEAD_GROUPS, ATTN_BLOCK, HEAD_GROUP * V_HEAD_DIM), lambda b: (0, b, 0)),
        out_shape=jax.ShapeDtypeStruct((N_HEAD_GROUPS, n_tok, HEAD_GROUP * V_HEAD_DIM), BF16),
        compiler_params=_cparams("arbitrary"),
        name="attn_rope" if rope else "attn",
    )(*args)


ROUTE_TM = 2 * TOK_CHUNK


def _attn_out_kernel(op_ref, os_ref, x_ref, mod_ref, g_ref, wo_ref, wr_ref,
                     x3_ref, h_ref, gates_ref, route_ref, cstart_ref, total_ref, carry_ref):
    i = pl.program_id(0)
    mod = mod_ref[...]
    is_prompt = i * x_ref.shape[0] < NP_TOK
    att = _dot(jnp.where(is_prompt, op_ref[0], os_ref[0]), wo_ref[0])
    for gi in range(1, N_HEAD_GROUPS):
        att += _dot(jnp.where(is_prompt, op_ref[gi], os_ref[gi]), wo_ref[gi])
    x3 = x_ref[...] + _mod_part(mod, 2) * att
    x3_ref[...] = x3
    h = _rms(x3, g_ref[...]) * (1.0 + _mod_part(mod, 4)) + _mod_part(mod, 3)
    hb = h.astype(BF16)
    h_ref[...] = hb
    wh, wl = _split_bf16(wr_ref[...])
    hl = (h - hb.astype(F32)).astype(BF16)
    logits = _dot(hb, wh) + (_dot(hl, wh) + _dot(hb, wl))
    lane = lax.broadcasted_iota(I32, logits.shape, 1)
    lanef = lane.astype(F32)
    neg = jnp.float32(-jnp.inf)
    logits = jnp.where(lane < N_EXPERTS, logits, neg)
    v1 = jnp.max(logits, axis=-1, keepdims=True)
    i1 = jnp.min(jnp.where(logits == v1, lanef, float(LANE)), axis=-1, keepdims=True)
    rest = jnp.where(lanef == i1, neg, logits)
    v2 = jnp.max(rest, axis=-1, keepdims=True)
    i2 = jnp.min(jnp.where(rest == v2, lanef, float(LANE)), axis=-1, keepdims=True)
    e2 = jnp.exp(v2 - v1)
    w1 = 1.0 / (1.0 + e2)
    hot1, hot2 = lanef == i1, lanef == i2
    gates_ref[...] = jnp.where(hot1, w1, 0.0) + jnp.where(hot2, e2 * w1, 0.0)

    @pl.when(i == 0)
    def _():
        carry_ref[...] = jnp.zeros_like(carry_ref)

    tm = logits.shape[0]
    hot = jnp.where(hot1 | hot2, 1.0, 0.0)
    r_id = lax.broadcasted_iota(I32, (tm, tm), 0)
    c_id = lax.broadcasted_iota(I32, (tm, tm), 1)
    tri = jnp.where(c_id < r_id, 1.0, 0.0).astype(BF16)
    before = _dot(tri, hot.astype(BF16)) + carry_ref[0:1, :]
    rank1 = jnp.sum(jnp.where(hot1, before, 0.0), axis=-1, keepdims=True)
    rank2 = jnp.sum(jnp.where(hot2, before, 0.0), axis=-1, keepdims=True)
    route_ref[...] = jnp.where(lane == 0, i1, jnp.where(lane == 1, i2, jnp.where(lane == 2, rank1,
                               jnp.where(lane == 3, rank2, 0.0))))
    for k in range(tm // TOK_CHUNK):
        cstart_ref[k] = jnp.broadcast_to(before[k * TOK_CHUNK:k * TOK_CHUNK + 1, :], (SUBLANE, LANE))
    total = before[tm - 1:tm, :] + hot[tm - 1:tm, :]
    carry_ref[...] = jnp.broadcast_to(total, (SUBLANE, LANE))
    total_ref[...] = jnp.broadcast_to(total, (SUBLANE, LANE))


def _attn_out(o_p, o_s, x, mods, g, wo, wr, layer):
    tm = ROUTE_TM
    per = tm // TOK_CHUNK
    n_p = NP_TOK // tm
    o_block = (N_HEAD_GROUPS, tm, HEAD_GROUP * V_HEAD_DIM)
    return pl.pallas_call(
        _attn_out_kernel,
        grid=(N_TOK // tm,),
        in_specs=[
            pl.BlockSpec(o_block, lambda i: (0, jnp.minimum(i, n_p - 1), 0)),
            pl.BlockSpec(o_block, lambda i: (0, jnp.maximum(i - n_p, 0), 0)),
            pl.BlockSpec((tm, D_MODEL), lambda i: (i, 0)),
            _mod_spec(layer, tm),
            pl.BlockSpec((1, D_MODEL), lambda i: (0, 0)),
            pl.BlockSpec((N_HEAD_GROUPS, HEAD_GROUP * V_HEAD_DIM, D_MODEL), lambda i: (0, 0, 0)),
            pl.BlockSpec((D_MODEL, LANE), lambda i: (0, 0)),
        ],
        out_specs=[
            pl.BlockSpec((tm, D_MODEL), lambda i: (i, 0)),
            pl.BlockSpec((tm, D_MODEL), lambda i: (i, 0)),
            pl.BlockSpec((tm, LANE), lambda i: (i, 0)),
            pl.BlockSpec((tm, LANE), lambda i: (i, 0)),
            pl.BlockSpec((per, SUBLANE, LANE), lambda i: (i, 0, 0)),
            pl.BlockSpec((SUBLANE, LANE), lambda i: (0, 0)),
        ],
        out_shape=[
            jax.ShapeDtypeStruct((N_TOK, D_MODEL), F32),
            jax.ShapeDtypeStruct((N_TOK, D_MODEL), BF16),
            jax.ShapeDtypeStruct((N_TOK, LANE), F32),
            jax.ShapeDtypeStruct((N_TOK, LANE), F32),
            jax.ShapeDtypeStruct((N_CHUNKS, SUBLANE, LANE), F32),
            jax.ShapeDtypeStruct((SUBLANE, LANE), F32),
        ],
        scratch_shapes=[pltpu.VMEM((SUBLANE, LANE), F32)],
        compiler_params=_cparams("arbitrary"),
        name="attn_out",
    )(o_p, o_s, x, mods, g, wo, wr)


def _routing_tables(route, cstart, total):
    counts = total[0, :N_EXPERTS].astype(I32)
    padded = (counts + SLOT_TILE - 1) // SLOT_TILE * SLOT_TILE
    ends = jnp.cumsum(padded)
    offs = ends - padded
    e1, e2 = route[:, 0].astype(I32), route[:, 1].astype(I32)
    slot1 = offs[e1] + route[:, 2].astype(I32)
    slot2 = offs[e2] + route[:, 3].astype(I32)

    n_active = ends[-1] // SLOT_TILE
    tile_start = jnp.arange(N_SLOT_TILES, dtype=I32) * SLOT_TILE
    tile_expert = jnp.sum(tile_start[:, None] >= ends[None, :], axis=1).astype(I32)
    last_expert = jnp.sum((n_active - 1) * SLOT_TILE >= ends).astype(I32)
    active = tile_start < ends[-1]
    tile_expert = jnp.where(active, tile_expert, last_expert)

    cc = jnp.concatenate([cstart[:, 0, :N_EXPERTS], total[0:1, :N_EXPERTS]]).astype(I32)

    rank0 = tile_start - offs[tile_expert]
    cc_tile = cc[:, tile_expert]
    c_lo = jnp.sum(cc_tile[1:] <= rank0[None, :], axis=0).astype(I32)
    rank_end = jnp.minimum(rank0 + SLOT_TILE, counts[tile_expert])
    c_hi = jnp.sum(cc_tile[:-1] < rank_end[None, :], axis=0).astype(I32) - 1
    c_lo = jnp.where(active, c_lo, 1)
    c_hi = jnp.where(active, c_hi, 0)

    lo = offs[None, :] + cc[:-1]
    hi = offs[None, :] + cc[1:]
    first, last = lo // SLOT_CHUNK, (hi - 1) // SLOT_CHUNK
    ids = jnp.concatenate([first, last], axis=1)
    valid = jnp.concatenate([hi > lo, (hi > lo) & (last != first)], axis=1)
    experts = jnp.tile(jnp.arange(N_EXPERTS, dtype=I32), (N_CHUNKS, 2))
    order = jnp.argsort(jnp.logical_not(valid), axis=1, stable=True)
    ids = jnp.take_along_axis(ids, order, axis=1).astype(I32)
    experts = jnp.take_along_axis(experts, order, axis=1)
    n_pairs = jnp.sum(valid, axis=1).astype(I32)
    pair_start = (jnp.cumsum(n_pairs) - n_pairs).astype(I32)

    pad = jnp.zeros((N_CHUNKS, SUBLANE - TOP_K, TOK_CHUNK), I32)
    slots_lane = jnp.concatenate([slot1.reshape(N_CHUNKS, 1, TOK_CHUNK), slot2.reshape(N_CHUNKS, 1, TOK_CHUNK), pad],
                                 axis=1)
    slots_tok = jnp.concatenate([slot1[:, None], slot2[:, None], jnp.zeros((N_TOK, LANE - TOP_K), I32)], axis=1)
    return dict(tile_expert=tile_expert, n_active=n_active.reshape(1).astype(I32), c_lo=c_lo, c_hi=c_hi,
                ids=ids.reshape(-1), experts=experts.reshape(-1), n_pairs=n_pairs, pair_start=pair_start,
                slots_lane=slots_lane, slots_tok=slots_tok)


def _gather_kernel(clo_ref, chi_ref, slots_ref, h_ref, o_ref, acc_ref):
    g = pl.program_id(0)
    slot_id = g * SLOT_TILE + lax.broadcasted_iota(I32, (SLOT_TILE, TOK_CHUNK), 0)
    acc_ref[...] = jnp.zeros_like(acc_ref)

    def body(c, carry):
        sl = slots_ref[c]
        hit = (sl[0:1, :] == slot_id) | (sl[1:2, :] == slot_id)
        rows = h_ref[pl.ds(pl.multiple_of(c * TOK_CHUNK, TOK_CHUNK), TOK_CHUNK), :]
        acc_ref[...] += _dot(jnp.where(hit, 1.0, 0.0).astype(BF16), rows)
        return carry

    lax.fori_loop(clo_ref[g], chi_ref[g] + 1, body, 0)
    o_ref[...] = acc_ref[...].astype(BF16)


def _gather(h, rt):
    return pl.pallas_call(
        _gather_kernel,
        grid_spec=pltpu.PrefetchScalarGridSpec(
            num_scalar_prefetch=2,
            grid=(N_SLOT_TILES,),
            in_specs=[
                pl.BlockSpec((N_CHUNKS, SUBLANE, TOK_CHUNK), lambda g, *_: (0, 0, 0)),
                pl.BlockSpec((N_TOK, D_MODEL), lambda g, *_: (0, 0), pipeline_mode=pl.Buffered(1)),
            ],
            out_specs=pl.BlockSpec((SLOT_TILE, D_MODEL), lambda g, *_: (g, 0)),
            scratch_shapes=[pltpu.VMEM((SLOT_TILE, D_MODEL), F32)],
        ),
        out_shape=jax.ShapeDtypeStruct((N_SLOTS, D_MODEL), BF16),
        compiler_params=_cparams("arbitrary"),
        name="moe_gather",
    )(rt["c_lo"], rt["c_hi"], rt["slots_lane"], h)


def _experts_kernel(te_ref, na_ref, h_ref, w1_ref, w3_ref, w2_ref, o_ref):
    g = pl.program_id(0)

    @pl.when(g < na_ref[0])
    def _():
        h = h_ref[...]
        t = _silu(_dot(h, w1_ref[...])) * _dot(h, w3_ref[...])
        o_ref[...] = _dot(t.astype(BF16), w2_ref[...]).astype(BF16)

    @pl.when(g >= na_ref[0])
    def _():
        o_ref[...] = jnp.zeros_like(o_ref)


def _experts(hs, rt, w1, w3, w2):
    tile = lambda g, te, na: (jnp.minimum(g, na[0] - 1), 0)
    by_expert = lambda g, te, na: (te[g], 0, 0)
    return pl.pallas_call(
        _experts_kernel,
        grid_spec=pltpu.PrefetchScalarGridSpec(
            num_scalar_prefetch=2,
            grid=(N_SLOT_TILES,),
            in_specs=[
                pl.BlockSpec((SLOT_TILE, D_MODEL), tile),
                pl.BlockSpec((None, D_MODEL, D_FF_EXPERT), by_expert),
                pl.BlockSpec((None, D_MODEL, D_FF_EXPERT), by_expert),
                pl.BlockSpec((None, D_FF_EXPERT, D_MODEL), by_expert),
            ],
            out_specs=pl.BlockSpec((SLOT_TILE, D_MODEL), lambda g, te, na: (g, 0)),
        ),
        out_shape=jax.ShapeDtypeStruct((N_SLOTS, D_MODEL), BF16),
        compiler_params=_cparams("arbitrary"),
        name="moe_experts",
    )(rt["tile_expert"], rt["n_active"], hs, w1, w3, w2)


def _combine_kernel(np_ref, ps_ref, ids_ref, ex_ref, slots_ref, gates_ref, x_ref, mod_ref, y_hbm,
                    op_ref, os_ref, buf_ref, acc_ref, sem):
    c = pl.program_id(0)
    n = np_ref[c]
    p0 = ps_ref[c]

    def chunk_copy(pair, slot):
        src = y_hbm.at[pl.ds(pl.multiple_of(ids_ref[pair] * SLOT_CHUNK, SLOT_CHUNK), SLOT_CHUNK)]
        return pltpu.make_async_copy(src, buf_ref.at[slot], sem.at[slot])

    @pl.when(c == 0)
    def _():
        chunk_copy(0, 0).start()

    slots = slots_ref[...]
    lane = lax.broadcasted_iota(I32, (TOK_CHUNK, SLOT_CHUNK), 1)
    s1 = jnp.broadcast_to(slots[:, 0:1], (TOK_CHUNK, SLOT_CHUNK)) - lane
    s2 = jnp.broadcast_to(slots[:, 1:2], (TOK_CHUNK, SLOT_CHUNK)) - lane
    gates = gates_ref[...]
    acc_ref[...] = jnp.zeros_like(acc_ref)

    def body(j, carry):
        pair = c * MAX_PAIRS + j
        slot = (p0 + j) % 2
        chunk_copy(pair, slot).wait()
        nxt = jnp.where(j + 1 < n, pair + 1, (c + 1) * MAX_PAIRS)

        @pl.when((j + 1 < n) | (c + 1 < pl.num_programs(0)))
        def _():
            chunk_copy(nxt, 1 - slot).start()

        base = ids_ref[pair] * SLOT_CHUNK
        hit = (s1 == base) | (s2 == base)
        part = _dot(jnp.where(hit, 1.0, 0.0).astype(BF16), buf_ref[slot])
        acc_ref[...] += _lane_pick(gates, ex_ref[pair]) * part
        return carry

    lax.fori_loop(0, n, body, 0)
    res = x_ref[...] + _mod_part(mod_ref[...], 5) * acc_ref[...]

    @pl.when(c * TOK_CHUNK < NP_TOK)
    def _():
        op_ref[...] = res

    @pl.when(c * TOK_CHUNK >= NP_TOK)
    def _():
        os_ref[...] = res


def _combine(ys, gates, x, mods, rt, layer):
    tok = lambda c, *_: (c, 0)
    return pl.pallas_call(
        _combine_kernel,
        grid_spec=pltpu.PrefetchScalarGridSpec(
            num_scalar_prefetch=4,
            grid=(N_CHUNKS,),
            in_specs=[
                pl.BlockSpec((TOK_CHUNK, LANE), tok),
                pl.BlockSpec((TOK_CHUNK, LANE), tok),
                pl.BlockSpec((TOK_CHUNK, D_MODEL), tok),
                _mod_spec(layer, TOK_CHUNK),
                pl.BlockSpec(memory_space=pl.ANY),
            ],
            out_specs=_split_specs(TOK_CHUNK),
            scratch_shapes=[
                pltpu.VMEM((2, SLOT_CHUNK, D_MODEL), BF16),
                pltpu.VMEM((TOK_CHUNK, D_MODEL), F32),
                pltpu.SemaphoreType.DMA((2,)),
            ],
        ),
        out_shape=[jax.ShapeDtypeStruct((NP_TOK, D_MODEL), F32), jax.ShapeDtypeStruct((NS_TOK, D_MODEL), F32)],
        compiler_params=_cparams("arbitrary"),
        name="moe_combine",
    )(rt["n_pairs"], rt["pair_start"], rt["ids"], rt["experts"], rt["slots_tok"], gates, x, mods, ys)


def _rope_partner(t):
    half = AXIS_ROPE_DIM // 2
    s = t.shape[:-1]
    return t.reshape(s + (2, 2, half))[..., ::-1, :].reshape(s + (ROPE_DIM,))


def _rope_tables(n_tokens):
    rows = n_tokens // GRID_W
    row = jnp.repeat(jnp.arange(rows), GRID_W).astype(F32)
    col = jnp.tile(jnp.arange(GRID_W), rows).astype(F32)
    inv = ROPE_BASE ** (-jnp.arange(0, AXIS_ROPE_DIM, 2, dtype=F32) / AXIS_ROPE_DIM)
    ar, ac = row[:, None] * inv, col[:, None] * inv
    cos = jnp.concatenate([jnp.cos(ar), jnp.cos(ar), jnp.cos(ac), jnp.cos(ac)], axis=-1)
    sin = jnp.concatenate([-jnp.sin(ar), jnp.sin(ar), -jnp.sin(ac), jnp.sin(ac)], axis=-1)
    zeros = jnp.zeros_like(cos)
    return jnp.concatenate([cos, zeros], axis=-1), jnp.concatenate([sin, zeros], axis=-1)


def _qk_gain_rows(g):
    z = jnp.zeros((ROPE_DIM,), F32)
    rows = jnp.stack([g[:QK_NOPE_DIM],
                      jnp.concatenate([g[QK_NOPE_DIM:], z]),
                      jnp.concatenate([_rope_partner(g[QK_NOPE_DIM:]), z])])
    return jnp.concatenate([rows, jnp.zeros((SUBLANE - 3, LANE), F32)])


def kernel(x_prompt, x_sample, c, cache_ckv, cache_krope, c_ctx, ada_w, ada_b, norm1_g, norm2_g, conv_pw1, conv_dw, conv_dw_b, conv_ln_g, conv_ln_b, conv_pw2, ffn_w1, ffn_w3, ffn_w2, mla_wdq, mla_q_norm_g, mla_wuq, mla_wdkv, mla_kv_norm_g, mla_wukv, mla_q_qk_g, mla_k_qk_g, mla_wo, moe_router, moe_w1, moe_w3, moe_w2):
    xp, xs = x_prompt.reshape(NP_TOK, D_MODEL), x_sample.reshape(NS_TOK, D_MODEL)
    cond16 = jnp.concatenate([c_ctx[None, :], c, jnp.zeros((MOD_ROWS - 1 - DEC_BATCH, D_MODEL), F32)])
    mods = _adaln(cond16, ada_w, ada_b).reshape(2, MOD_ROWS, 1, N_MOD * D_MODEL)
    vec = lambda a: a.reshape(1, -1)

    u = _glu(xp, xs, mods, vec(norm1_g[0]), conv_pw1[0].astype(BF16), 0)
    dw = jnp.concatenate([conv_dw[0], jnp.zeros((1, D_MODEL), F32)])
    x = _conv(u, xp, xs, mods, dw, vec(conv_dw_b[0]), vec(conv_ln_g[0]), vec(conv_ln_b[0]),
              conv_pw2[0].astype(BF16), 0)
    x = _ffn(x, mods, vec(norm2_g[0]), ffn_w1[0].astype(BF16), ffn_w3[0].astype(BF16),
             ffn_w2[0].astype(BF16), 0)

    wdkv = mla_wdkv[0]
    w_down = jnp.concatenate([mla_wdq[0], wdkv, _rope_partner(wdkv[:, KV_LORA_RANK:])], axis=1).astype(BF16)
    cq, ckv, ckv_b, kr2 = _mla_down(x, mods, vec(norm1_g[1]), w_down, vec(mla_q_norm_g[0]),
                                    vec(mla_kv_norm_g[0]), 1)

    wuq = mla_wuq[0].reshape(Q_LORA_RANK, N_HEADS, QK_HEAD_DIM)
    wuq = jnp.concatenate([wuq, _rope_partner(wuq[..., QK_NOPE_DIM:])], axis=-1)
    wuq = wuq.transpose(1, 0, 2).astype(BF16)
    wukv = mla_wukv[0].reshape(KV_LORA_RANK, N_HEADS, HEAD_PAD).transpose(1, 0, 2).astype(BF16)
    gq, gk = _qk_gain_rows(mla_q_qk_g[0]), _qk_gain_rows(mla_k_qk_g[0])
    ckr = cache_krope[:, 0]
    cache = (cache_ckv[:, 0], jnp.concatenate([ckr, _rope_partner(ckr)], axis=-1))
    o_p = _attention(cq, ckv_b, kr2, None, wuq, wukv, gq, gk, None, tok0=0, n_tok=NP_TOK, seq=SEQ)
    o_s = _attention(cq, ckv_b, kr2, cache, wuq, wukv, gq, gk, _rope_tables(DEC_SEQ),
                     tok0=NP_TOK, n_tok=NS_TOK, seq=DEC_SEQ)

    wr = jnp.concatenate([moe_router[0], jnp.zeros((D_MODEL, LANE - N_EXPERTS), F32)], axis=1)
    wo = mla_wo[0].astype(BF16).reshape(N_HEAD_GROUPS, HEAD_GROUP * V_HEAD_DIM, D_MODEL)
    x, h, gates, route, cstart, total = _attn_out(o_p, o_s, x, mods, vec(norm2_g[1]), wo, wr, 1)
    rt = _routing_tables(route, cstart, total)
    hs = _gather(h, rt)
    ys = _experts(hs, rt, moe_w1[0].astype(BF16), moe_w3[0].astype(BF16), moe_w2[0].astype(BF16))
    yp, ysamp = _combine(ys, gates, x, mods, rt, 1)

    new_ckv = ckv[:NP_TOK].reshape(BATCH, 1, SEQ, KV_LORA_RANK)
    new_krope = kr2[:NP_TOK, :ROPE_DIM].reshape(BATCH, 1, SEQ, ROPE_DIM)
    return (yp.reshape(BATCH, SEQ, D_MODEL), ysamp.reshape(DEC_BATCH, DEC_SEQ, D_MODEL), new_ckv, new_krope)
```

```python
import functools

import jax
import jax.numpy as jnp
from jax import lax
from jax.experimental import pallas as pl
from jax.experimental.pallas import tpu as pltpu

D_MODEL = 1024
BATCH = 32
SEQ = 256
DEC_BATCH = 8
DEC_SEQ = 1024
PAST_LEN = 512
GRID_W = 64
N_MOD = 6
CONV_WIDTH = 31
CONV_PAD = CONV_WIDTH // 2
N_HEADS = 16
QK_NOPE_DIM = 128
ROPE_DIM = 64
QK_HEAD_DIM = QK_NOPE_DIM + ROPE_DIM
V_HEAD_DIM = 128
Q_LORA_RANK = 512
KV_LORA_RANK = 256
AXIS_ROPE_DIM = ROPE_DIM // 2
ROPE_BASE = 10000.0
D_FF = 2816
N_EXPERTS = 8
TOP_K = 2
D_FF_EXPERT = 1536
EPS = 1e-6
F32 = jnp.float32
BF16 = jnp.bfloat16
I32 = jnp.int32

NP_TOK = BATCH * SEQ
NS_TOK = DEC_BATCH * DEC_SEQ
N_TOK = NP_TOK + NS_TOK
MOD_ROWS = 16
LANE = 128
SUBLANE = 8
HEAD_PAD = 2 * LANE
VMEM_LIMIT = 56 * 1024 * 1024

TOK_CHUNK = 256
N_CHUNKS = N_TOK // TOK_CHUNK
SLOT_TILE = 512
N_SLOT_TILES = (TOP_K * N_TOK + N_EXPERTS * (SLOT_TILE - 1)) // SLOT_TILE
N_SLOTS = N_SLOT_TILES * SLOT_TILE
SLOT_CHUNK = 256
MAX_PAIRS = 2 * N_EXPERTS


def _cparams(*sem):
    return pltpu.CompilerParams(dimension_semantics=sem, vmem_limit_bytes=VMEM_LIMIT)


def _mod_row(tile, tm):
    start = tile * tm
    return jnp.where(start < NP_TOK, 0, 1 + (start - NP_TOK) // DEC_SEQ)


def _mod_spec(layer, tm):
    return pl.BlockSpec((None, None, 1, N_MOD * D_MODEL),
                        lambda i, *_: (layer, _mod_row(i, tm), 0, 0))


def _split_specs(tm):
    n_p = NP_TOK // tm
    return [pl.BlockSpec((tm, D_MODEL), lambda i, *_: (jnp.minimum(i, n_p - 1), 0)),
            pl.BlockSpec((tm, D_MODEL), lambda i, *_: (jnp.maximum(i - n_p, 0), 0))]


def _pick_tokens(tm, xp_ref, xs_ref):
    return jnp.where(pl.program_id(0) * tm < NP_TOK, xp_ref[...], xs_ref[...])


def _mod_part(mod, k):
    return mod[:, k * D_MODEL:(k + 1) * D_MODEL]


def _rms(x, g):
    return x * lax.rsqrt(jnp.mean(x * x, axis=-1, keepdims=True) + EPS) * g


def _silu(x):
    return x * jax.nn.sigmoid(x)


def _split_bf16(x):
    hi = x.astype(BF16)
    lo = (x - hi.astype(F32)).astype(BF16)
    return hi, lo


def _dot(a, b):
    return jnp.dot(a, b, preferred_element_type=F32)


def _dot3(a, b):
    ah, al = _split_bf16(a)
    bh, bl = _split_bf16(b)
    return _dot(ah, bh) + (_dot(al, bh) + _dot(ah, bl))


def _lane_pick(x, idx):
    lane = lax.broadcasted_iota(I32, x.shape, 1)
    return jnp.sum(jnp.where(lane == idx, x, 0.0), axis=-1, keepdims=True)


def _adaln_kernel(cond_ref, w_ref, b_ref, o_ref):
    o_ref[...] = _dot3(_silu(cond_ref[...]), w_ref[...]) + b_ref[...]


def _adaln(cond16, ada_w, ada_b):
    depth = ada_w.shape[0]
    tn = 1536
    return pl.pallas_call(
        _adaln_kernel,
        grid=(depth, N_MOD * D_MODEL // tn),
        in_specs=[
            pl.BlockSpec((MOD_ROWS, D_MODEL), lambda l, j: (0, 0)),
            pl.BlockSpec((None, D_MODEL, tn), lambda l, j: (l, 0, j)),
            pl.BlockSpec((None, 1, tn), lambda l, j: (l, 0, j)),
        ],
        out_specs=pl.BlockSpec((None, MOD_ROWS, tn), lambda l, j: (l, 0, j)),
        out_shape=jax.ShapeDtypeStruct((depth, MOD_ROWS, N_MOD * D_MODEL), F32),
        compiler_params=_cparams("arbitrary", "arbitrary"),
        name="adaln",
    )(cond16, ada_w, ada_b.reshape(depth, 1, N_MOD * D_MODEL))


def _glu_kernel(xp_ref, xs_ref, mod_ref, g_ref, w_ref, u_ref):
    mod = mod_ref[...]
    x = _pick_tokens(u_ref.shape[0], xp_ref, xs_ref)
    h = _rms(x, g_ref[...]) * (1.0 + _mod_part(mod, 1)) + _mod_part(mod, 0)
    ag = _dot(h.astype(BF16), w_ref[...])
    u_ref[...] = ag[:, :D_MODEL] * jax.nn.sigmoid(ag[:, D_MODEL:])


def _glu(xp, xs, mods, g, pw1, layer):
    tm = 512
    return pl.pallas_call(
        _glu_kernel,
        grid=(N_TOK // tm,),
        in_specs=_split_specs(tm) + [
            _mod_spec(layer, tm),
            pl.BlockSpec((1, D_MODEL), lambda i: (0, 0)),
            pl.BlockSpec((D_MODEL, 2 * D_MODEL), lambda i: (0, 0)),
        ],
        out_specs=pl.BlockSpec((tm, D_MODEL), lambda i: (i, 0)),
        out_shape=jax.ShapeDtypeStruct((N_TOK, D_MODEL), F32),
        compiler_params=_cparams("arbitrary"),
        name="glu",
    )(xp, xs, mods, g, pw1)


CONV_CHUNK = 256
CONV_HALO = 16
CONV_ROWS = 64
CONV_SHIFT_ROWS = CONV_CHUNK + (CONV_HALO - CONV_PAD + CONV_WIDTH - 1) // SUBLANE * SUBLANE


def _conv_kernel(uc_ref, up_ref, un_ref, dw_ref, dwb_ref, lng_ref, lnb_ref, w_ref, xp_ref, xs_ref, mod_ref,
                 o_ref, pad_ref, shift_ref, conv_ref):
    i = pl.program_id(0)
    start = i * CONV_CHUNK
    seq_len = jnp.where(start < NP_TOK, SEQ, DEC_SEQ)
    off = jnp.where(start < NP_TOK, start, start - NP_TOK) % seq_len
    prev_ok = off > 0
    next_ok = off + CONV_CHUNK < seq_len
    pad_ref[0:CONV_HALO, :] = jnp.where(prev_ok, up_ref[...], 0.0)
    pad_ref[CONV_HALO:CONV_HALO + CONV_CHUNK, :] = uc_ref[...]
    pad_ref[CONV_HALO + CONV_CHUNK:, :] = jnp.where(next_ok, un_ref[...], 0.0)

    base = CONV_HALO - CONV_PAD
    for b in range(1, SUBLANE):
        shift_ref[b - 1] = pad_ref[b:b + CONV_SHIFT_ROWS, :]
    for c in range(D_MODEL // LANE):
        cs = slice(c * LANE, (c + 1) * LANE)
        wcol = dw_ref[:, cs]
        bias = dwb_ref[:, cs]
        for r in range(CONV_CHUNK // CONV_ROWS):
            acc = jnp.broadcast_to(bias, (CONV_ROWS, LANE))
            for k in range(CONV_WIDTH):
                a, b = divmod(base + k, SUBLANE)
                lo = r * CONV_ROWS + SUBLANE * a
                src = pad_ref if b == 0 else shift_ref.at[b - 1]
                acc = acc + wcol[k:k + 1, :] * src[lo:lo + CONV_ROWS, cs]
            conv_ref[r * CONV_ROWS:(r + 1) * CONV_ROWS, cs] = acc

    t = conv_ref[...]
    mu = jnp.mean(t, axis=-1, keepdims=True)
    tc = t - mu
    y = tc * lax.rsqrt(jnp.mean(tc * tc, axis=-1, keepdims=True) + EPS) * lng_ref[...] + lnb_ref[...]
    res = _dot(_silu(y).astype(BF16), w_ref[...])
    o_ref[...] = _pick_tokens(CONV_CHUNK, xp_ref, xs_ref) + _mod_part(mod_ref[...], 2) * res


def _conv(u, xp, xs, mods, dw, dwb, lng, lnb, pw2, layer):
    n_chunks = N_TOK // CONV_CHUNK
    halo_per_chunk = CONV_CHUNK // CONV_HALO
    n_halo = N_TOK // CONV_HALO
    row = lambda i: (i, 0)
    const = lambda i: (0, 0)
    return pl.pallas_call(
        _conv_kernel,
        grid=(n_chunks,),
        in_specs=[
            pl.BlockSpec((CONV_CHUNK, D_MODEL), row),
            pl.BlockSpec((CONV_HALO, D_MODEL), lambda i: (jnp.maximum(i * halo_per_chunk - 1, 0), 0)),
            pl.BlockSpec((CONV_HALO, D_MODEL),
                         lambda i: (jnp.minimum((i + 1) * halo_per_chunk, n_halo - 1), 0)),
            pl.BlockSpec((CONV_WIDTH + 1, D_MODEL), const),
            pl.BlockSpec((1, D_MODEL), const),
            pl.BlockSpec((1, D_MODEL), const),
            pl.BlockSpec((1, D_MODEL), const),
            pl.BlockSpec((D_MODEL, D_MODEL), const),
        ] + _split_specs(CONV_CHUNK) + [
            _mod_spec(layer, CONV_CHUNK),
        ],
        out_specs=pl.BlockSpec((CONV_CHUNK, D_MODEL), row),
        out_shape=jax.ShapeDtypeStruct((N_TOK, D_MODEL), F32),
        scratch_shapes=[
            pltpu.VMEM((CONV_CHUNK + 2 * CONV_HALO, D_MODEL), F32),
            pltpu.VMEM((SUBLANE - 1, CONV_SHIFT_ROWS, D_MODEL), F32),
            pltpu.VMEM((CONV_CHUNK, D_MODEL), F32),
        ],
        compiler_params=_cparams("arbitrary"),
        name="conv",
    )(u, u, u, dw, dwb, lng, lnb, pw2, xp, xs, mods)


def _ffn_kernel(x_ref, mod_ref, g_ref, w1_ref, w3_ref, w2_ref, o_ref, h_ref, acc_ref):
    j = pl.program_id(1)

    @pl.when(j == 0)
    def _():
        mod = mod_ref[...]
        h = _rms(x_ref[...], g_ref[...]) * (1.0 + _mod_part(mod, 4)) + _mod_part(mod, 3)
        h_ref[...] = h.astype(BF16)

    h = h_ref[...]
    t = _silu(_dot(h, w1_ref[...])) * _dot(h, w3_ref[...])
    y = _dot(t.astype(BF16), w2_ref[...])

    @pl.when(j == 0)
    def _():
        acc_ref[...] = y

    @pl.when(j > 0)
    def _():
        acc_ref[...] += y

    @pl.when(j == pl.num_programs(1) - 1)
    def _():
        o_ref[...] = x_ref[...] + _mod_part(mod_ref[...], 5) * acc_ref[...]


def _ffn(x, mods, g, w1, w3, w2, layer):
    tm = 512
    tf = D_FF // 2
    return pl.pallas_call(
        _ffn_kernel,
        grid=(N_TOK // tm, D_FF // tf),
        in_specs=[
            pl.BlockSpec((tm, D_MODEL), lambda i, j: (i, 0)),
            _mod_spec(layer, tm),
            pl.BlockSpec((1, D_MODEL), lambda i, j: (0, 0)),
            pl.BlockSpec((D_MODEL, tf), lambda i, j: (0, j)),
            pl.BlockSpec((D_MODEL, tf), lambda i, j: (0, j)),
            pl.BlockSpec((tf, D_MODEL), lambda i, j: (j, 0)),
        ],
        out_specs=pl.BlockSpec((tm, D_MODEL), lambda i, j: (i, 0)),
        out_shape=jax.ShapeDtypeStruct((N_TOK, D_MODEL), F32),
        scratch_shapes=[pltpu.VMEM((tm, D_MODEL), BF16), pltpu.VMEM((tm, D_MODEL), F32)],
        compiler_params=_cparams("arbitrary", "arbitrary"),
        name="ffn",
    )(x, mods, g, w1, w3, w2)


def _mla_down_kernel(x_ref, mod_ref, g_ref, w_ref, qg_ref, kvg_ref, cq_ref, ckv_ref, ckvb_ref, kr_ref):
    mod = mod_ref[...]
    h = _rms(x_ref[...], g_ref[...]) * (1.0 + _mod_part(mod, 1)) + _mod_part(mod, 0)
    d = _dot(h.astype(BF16), w_ref[...])
    cq_ref[...] = _rms(d[:, :Q_LORA_RANK], qg_ref[...]).astype(BF16)
    ckv = _rms(d[:, Q_LORA_RANK:Q_LORA_RANK + KV_LORA_RANK], kvg_ref[...])
    ckv_ref[...] = ckv
    ckvb_ref[...] = ckv.astype(BF16)
    kr_ref[...] = d[:, Q_LORA_RANK + KV_LORA_RANK:]


def _mla_down(x, mods, g, w_down, qg, kvg, layer):
    tm = 512
    n_down = Q_LORA_RANK + KV_LORA_RANK + LANE
    return pl.pallas_call(
        _mla_down_kernel,
        grid=(N_TOK // tm,),
        in_specs=[
            pl.BlockSpec((tm, D_MODEL), lambda i: (i, 0)),
            _mod_spec(layer, tm),
            pl.BlockSpec((1, D_MODEL), lambda i: (0, 0)),
            pl.BlockSpec((D_MODEL, n_down), lambda i: (0, 0)),
            pl.BlockSpec((1, Q_LORA_RANK), lambda i: (0, 0)),
            pl.BlockSpec((1, KV_LORA_RANK), lambda i: (0, 0)),
        ],
        out_specs=[
            pl.BlockSpec((tm, Q_LORA_RANK), lambda i: (i, 0)),
            pl.BlockSpec((tm, KV_LORA_RANK), lambda i: (i, 0)),
            pl.BlockSpec((tm, KV_LORA_RANK), lambda i: (i, 0)),
            pl.BlockSpec((tm, LANE), lambda i: (i, 0)),
        ],
        out_shape=[
            jax.ShapeDtypeStruct((N_TOK, Q_LORA_RANK), BF16),
            jax.ShapeDtypeStruct((N_TOK, KV_LORA_RANK), F32),
            jax.ShapeDtypeStruct((N_TOK, KV_LORA_RANK), BF16),
            jax.ShapeDtypeStruct((N_TOK, LANE), F32),
        ],
        compiler_params=_cparams("arbitrary"),
        name="mla_down",
    )(x, mods, g, w_down, qg, kvg)


ATTN_BLOCK = 1024
ATTN_TQ = 256
HEAD_GROUP = 2
N_HEAD_GROUPS = N_HEADS // HEAD_GROUP
LOG2E = 1.4426950408889634


def _attn_kernel(*refs, seq, n_cache, rope):
    it = iter(refs)
    cq_ref, ckv_ref, kr_ref = next(it), next(it), next(it)
    if n_cache:
        cckv_ref, ckr_ref = next(it), next(it)
    wuq_ref, wukv_ref, gq_ref, gk_ref = next(it), next(it), next(it), next(it)
    if rope:
        cos_ref, sin_ref = next(it), next(it)
    o_ref, k_scr, v_scr, q_scr = next(it), next(it), next(it), next(it)

    inv_dim = 1.0 / QK_HEAD_DIM
    gq, gk = gq_ref[...], gk_ref[...]
    cq, ckv, kr = cq_ref[...], ckv_ref[...], kr_ref[...]
    tables = (cos_ref[...], sin_ref[...]) if rope else None
    ones_new = jnp.ones((ATTN_BLOCK, LANE), BF16)
    if n_cache:
        cckv, ckr = cckv_ref[...].astype(BF16), ckr_ref[...]
        ones_cache = jnp.ones((n_cache, LANE), BF16)

    def normed(nope, rot2, g, tabs, out_scale):
        ssq = jnp.sum(nope * nope, axis=-1, keepdims=True) + 0.5 * jnp.sum(rot2 * rot2, axis=-1, keepdims=True)
        r = lax.rsqrt(ssq * inv_dim + EPS) * out_scale
        if tabs is None:
            rot = rot2 * g[1:2, :]
        else:
            rot = rot2 * (g[1:2, :] * tabs[0]) + pltpu.roll(rot2, ROPE_DIM, 1) * (g[2:3, :] * tabs[1])
        return jnp.concatenate([(nope * r * g[0:1, :]).astype(BF16), (rot * r).astype(BF16)], axis=1)

    def build(gi, slot):
        for j in range(HEAD_GROUP):
            h = gi * HEAD_GROUP + j
            wukv = wukv_ref[h]
            kv = _dot(ckv, wukv)
            k_scr[slot, j, 0:ATTN_BLOCK, :] = normed(kv[:, :LANE], kr, gk, tables, 1.0)
            v_scr[slot, j, 0:ATTN_BLOCK, :] = jnp.concatenate([kv[:, LANE:].astype(BF16), ones_new], axis=1)
            if n_cache:
                kvc = _dot(cckv, wukv)
                k_scr[slot, j, ATTN_BLOCK:, :] = normed(kvc[:, :LANE], ckr, gk, None, 1.0)
                v_scr[slot, j, ATTN_BLOCK:, :] = jnp.concatenate([kvc[:, LANE:].astype(BF16), ones_cache], axis=1)
            q = _dot(cq, wuq_ref[h])
            q_scr[slot, j] = normed(q[:, :LANE], q[:, LANE:], gq, tables, QK_HEAD_DIM ** -0.5 * LOG2E)

    def attend(gi, slot):
        heads = []
        for j in range(HEAD_GROUP):
            outs = []
            for i in range(ATTN_BLOCK // ATTN_TQ):
                rows = slice(i * ATTN_TQ, (i + 1) * ATTN_TQ)
                keys = slice(None) if seq == ATTN_BLOCK else rows
                s = lax.dot_general(q_scr[slot, j, rows, :], k_scr[slot, j, keys, :], (((1,), (1,)), ((), ())),
                                    preferred_element_type=F32)
                p = jnp.exp2(s - jnp.max(s, axis=-1, keepdims=True)).astype(BF16)
                oe = _dot(p, v_scr[slot, j, keys, :])
                outs.append((oe[:, :LANE] / oe[:, LANE:]).astype(BF16))
            heads.append(jnp.concatenate(outs, axis=0))
        o_ref[gi] = jnp.concatenate(heads, axis=1)

    build(0, 0)

    def two_groups(t, carry):
        g0 = 2 * t
        build(g0 + 1, 1)
        attend(g0, 0)
        build(jnp.minimum(g0 + 2, N_HEAD_GROUPS - 1), 0)
        attend(g0 + 1, 1)
        return carry

    lax.fori_loop(0, N_HEAD_GROUPS // 2, two_groups, 0)


def _attention(cq, ckv, kr2, cache, wuq, wukv, gq, gk, tables, *, tok0, n_tok, seq):
    assert seq in (ATTN_BLOCK, ATTN_TQ) and tok0 % ATTN_BLOCK == 0 and n_tok % ATTN_BLOCK == 0
    b0 = tok0 // ATTN_BLOCK
    n_cache = 0 if cache is None else cache[0].shape[1]
    assert n_cache == 0 or seq == ATTN_BLOCK
    rope = tables is not None
    row = lambda b: (b0 + b, 0)
    const2 = lambda b: (0, 0)
    const3 = lambda b: (0, 0, 0)
    in_specs = [
        pl.BlockSpec((ATTN_BLOCK, Q_LORA_RANK), row),
        pl.BlockSpec((ATTN_BLOCK, KV_LORA_RANK), row),
        pl.BlockSpec((ATTN_BLOCK, LANE), row),
    ]
    args = [cq, ckv, kr2]
    if n_cache:
        in_specs += [pl.BlockSpec((None, n_cache, KV_LORA_RANK), lambda b: (b, 0, 0)),
                     pl.BlockSpec((None, n_cache, LANE), lambda b: (b, 0, 0))]
        args += list(cache)
    in_specs += [
        pl.BlockSpec((N_HEADS, Q_LORA_RANK, HEAD_PAD), const3),
        pl.BlockSpec((N_HEADS, KV_LORA_RANK, HEAD_PAD), const3),
        pl.BlockSpec((SUBLANE, LANE), const2),
        pl.BlockSpec((SUBLANE, LANE), const2),
    ]
    args += [wuq, wukv, gq, gk]
    if rope:
        in_specs += [pl.BlockSpec((ATTN_BLOCK, LANE), const2), pl.BlockSpec((ATTN_BLOCK, LANE), const2)]
        args += list(tables)
    return pl.pallas_call(
        functools.partial(_attn_kernel, seq=seq, n_cache=n_cache, rope=rope),
        grid=(n_tok // ATTN_BLOCK,),
        in_specs=in_specs,
        out_specs=pl.BlockSpec((N_HEAD_GROUPS, ATTN_BLOCK, HEAD_GROUP * V_HEAD_DIM), lambda b: (0, b, 0)),
        out_shape=jax.ShapeDtypeStruct((N_HEAD_GROUPS, n_tok, HEAD_GROUP * V_HEAD_DIM), BF16),
        scratch_shapes=[
            pltpu.VMEM((2, HEAD_GROUP, ATTN_BLOCK + n_cache, HEAD_PAD), BF16),
            pltpu.VMEM((2, HEAD_GROUP, ATTN_BLOCK + n_cache, HEAD_PAD), BF16),
            pltpu.VMEM((2, HEAD_GROUP, ATTN_BLOCK, HEAD_PAD), BF16),
        ],
        compiler_params=_cparams("arbitrary"),
        name="attn_rope" if rope else "attn",
    )(*args)


ROUTE_TM = 2 * TOK_CHUNK
ROUTE_E1, ROUTE_E2, ROUTE_RANK1, ROUTE_RANK2 = N_EXPERTS, N_EXPERTS + 1, N_EXPERTS + 2, N_EXPERTS + 3
ROUTE_ROWS = 16


def _attn_out_kernel(op_ref, os_ref, x_ref, mod_ref, g_ref, wo_ref, wr_ref,
                     x3_ref, h_ref, route_ref, route_t_ref, cstart_ref, total_ref, carry_ref):
    i = pl.program_id(0)
    mod = mod_ref[...]
    is_prompt = i * x_ref.shape[0] < NP_TOK
    att = _dot(jnp.where(is_prompt, op_ref[0], os_ref[0]), wo_ref[0])
    for gi in range(1, N_HEAD_GROUPS):
        att += _dot(jnp.where(is_prompt, op_ref[gi], os_ref[gi]), wo_ref[gi])
    x3 = x_ref[...] + _mod_part(mod, 2) * att
    x3_ref[...] = x3
    h = _rms(x3, g_ref[...]) * (1.0 + _mod_part(mod, 4)) + _mod_part(mod, 3)
    hb = h.astype(BF16)
    h_ref[...] = hb
    wh, wl = _split_bf16(wr_ref[...])
    hl = (h - hb.astype(F32)).astype(BF16)
    logits = _dot(hb, wh) + (_dot(hl, wh) + _dot(hb, wl))
    lane = lax.broadcasted_iota(I32, logits.shape, 1)
    lanef = lane.astype(F32)
    neg = jnp.float32(-jnp.inf)
    logits = jnp.where(lane < N_EXPERTS, logits, neg)
    v1 = jnp.max(logits, axis=-1, keepdims=True)
    i1 = jnp.min(jnp.where(logits == v1, lanef, float(LANE)), axis=-1, keepdims=True)
    rest = jnp.where(lanef == i1, neg, logits)
    v2 = jnp.max(rest, axis=-1, keepdims=True)
    i2 = jnp.min(jnp.where(rest == v2, lanef, float(LANE)), axis=-1, keepdims=True)
    e2 = jnp.exp(v2 - v1)
    w1 = 1.0 / (1.0 + e2)
    hot1, hot2 = lanef == i1, lanef == i2
    gates = jnp.where(hot1, w1, 0.0) + jnp.where(hot2, e2 * w1, 0.0)

    @pl.when(i == 0)
    def _():
        carry_ref[...] = jnp.zeros_like(carry_ref)

    tm = logits.shape[0]
    hot = jnp.where(hot1 | hot2, 1.0, 0.0)
    r_id = lax.broadcasted_iota(I32, (tm, tm), 0)
    c_id = lax.broadcasted_iota(I32, (tm, tm), 1)
    tri = jnp.where(c_id < r_id, 1.0, 0.0).astype(BF16)
    before = _dot(tri, hot.astype(BF16)) + carry_ref[0:1, :]
    rank1 = jnp.sum(jnp.where(hot1, before, 0.0), axis=-1, keepdims=True)
    rank2 = jnp.sum(jnp.where(hot2, before, 0.0), axis=-1, keepdims=True)
    route = jnp.where(lane == ROUTE_E1, i1, jnp.where(lane == ROUTE_E2, i2, jnp.where(
        lane == ROUTE_RANK1, rank1, jnp.where(lane == ROUTE_RANK2, rank2, gates))))
    route_ref[...] = route
    route_t_ref[...] = route.T[:ROUTE_ROWS, :]
    for k in range(tm // TOK_CHUNK):
        cstart_ref[k] = jnp.broadcast_to(before[k * TOK_CHUNK:k * TOK_CHUNK + 1, :], (SUBLANE, LANE))
    total = before[tm - 1:tm, :] + hot[tm - 1:tm, :]
    carry_ref[...] = jnp.broadcast_to(total, (SUBLANE, LANE))
    total_ref[...] = jnp.broadcast_to(total, (SUBLANE, LANE))


def _attn_out(o_p, o_s, x, mods, g, wo, wr, layer):
    tm = ROUTE_TM
    per = tm // TOK_CHUNK
    n_p = NP_TOK // tm
    o_block = (N_HEAD_GROUPS, tm, HEAD_GROUP * V_HEAD_DIM)
    return pl.pallas_call(
        _attn_out_kernel,
        grid=(N_TOK // tm,),
        in_specs=[
            pl.BlockSpec(o_block, lambda i: (0, jnp.minimum(i, n_p - 1), 0)),
            pl.BlockSpec(o_block, lambda i: (0, jnp.maximum(i - n_p, 0), 0)),
            pl.BlockSpec((tm, D_MODEL), lambda i: (i, 0)),
            _mod_spec(layer, tm),
            pl.BlockSpec((1, D_MODEL), lambda i: (0, 0)),
            pl.BlockSpec((N_HEAD_GROUPS, HEAD_GROUP * V_HEAD_DIM, D_MODEL), lambda i: (0, 0, 0)),
            pl.BlockSpec((D_MODEL, LANE), lambda i: (0, 0)),
        ],
        out_specs=[
            pl.BlockSpec((tm, D_MODEL), lambda i: (i, 0)),
            pl.BlockSpec((tm, D_MODEL), lambda i: (i, 0)),
            pl.BlockSpec((tm, LANE), lambda i: (i, 0)),
            pl.BlockSpec((ROUTE_ROWS, tm), lambda i: (0, i)),
            pl.BlockSpec((per, SUBLANE, LANE), lambda i: (i, 0, 0)),
            pl.BlockSpec((SUBLANE, LANE), lambda i: (0, 0)),
        ],
        out_shape=[
            jax.ShapeDtypeStruct((N_TOK, D_MODEL), F32),
            jax.ShapeDtypeStruct((N_TOK, D_MODEL), BF16),
            jax.ShapeDtypeStruct((N_TOK, LANE), F32),
            jax.ShapeDtypeStruct((ROUTE_ROWS, N_TOK), F32),
            jax.ShapeDtypeStruct((N_CHUNKS, SUBLANE, LANE), F32),
            jax.ShapeDtypeStruct((SUBLANE, LANE), F32),
        ],
        scratch_shapes=[pltpu.VMEM((SUBLANE, LANE), F32)],
        compiler_params=_cparams("arbitrary"),
        name="attn_out",
    )(o_p, o_s, x, mods, g, wo, wr)


def _routing_tables(route_t, cstart, total):
    counts = total[0, :N_EXPERTS].astype(I32)
    padded = (counts + SLOT_TILE - 1) // SLOT_TILE * SLOT_TILE
    ends = jnp.cumsum(padded)
    offs = ends - padded
    e1, e2 = route_t[ROUTE_E1].astype(I32), route_t[ROUTE_E2].astype(I32)
    slot1 = offs[e1] + route_t[ROUTE_RANK1].astype(I32)
    slot2 = offs[e2] + route_t[ROUTE_RANK2].astype(I32)

    n_active = ends[-1] // SLOT_TILE
    tile_start = jnp.arange(N_SLOT_TILES, dtype=I32) * SLOT_TILE
    tile_expert = jnp.sum(tile_start[:, None] >= ends[None, :], axis=1).astype(I32)
    last_expert = jnp.sum((n_active - 1) * SLOT_TILE >= ends).astype(I32)
    tile_expert = jnp.where(tile_start < ends[-1], tile_expert, last_expert)

    cc = jnp.concatenate([cstart[:, 0, :N_EXPERTS], total[0:1, :N_EXPERTS]]).astype(I32)

    g_start = jnp.arange(N_SLOTS // GATHER_TILE, dtype=I32) * GATHER_TILE
    g_expert = jnp.minimum(jnp.sum(g_start[:, None] >= ends[None, :], axis=1), N_EXPERTS - 1).astype(I32)
    rank0 = g_start - offs[g_expert]
    cc_tile = cc[:, g_expert]
    c_lo = jnp.sum(cc_tile[1:] <= rank0[None, :], axis=0).astype(I32)
    rank_end = jnp.minimum(rank0 + GATHER_TILE, counts[g_expert])
    c_hi = jnp.sum(cc_tile[:-1] < rank_end[None, :], axis=0).astype(I32) - 1
    idle = (g_start >= ends[-1]) | (c_hi < c_lo)
    c_lo = jnp.where(idle, 1, c_lo)
    c_hi = jnp.where(idle, 0, c_hi)

    lo = offs[None, :] + cc[:-1]
    hi = offs[None, :] + cc[1:]
    first, last = lo // SLOT_CHUNK, (hi - 1) // SLOT_CHUNK
    ids = jnp.concatenate([first, last], axis=1)
    valid = jnp.concatenate([hi > lo, (hi > lo) & (last != first)], axis=1)
    experts = jnp.tile(jnp.arange(N_EXPERTS, dtype=I32), (N_CHUNKS, 2))
    order = jnp.argsort(jnp.logical_not(valid), axis=1, stable=True)
    ids = jnp.take_along_axis(ids, order, axis=1).astype(I32)
    experts = jnp.take_along_axis(experts, order, axis=1)
    n_pairs = jnp.sum(valid, axis=1).astype(I32)

    pad = jnp.zeros((N_CHUNKS, SUBLANE - TOP_K, TOK_CHUNK), I32)
    slots_lane = jnp.concatenate([slot1.reshape(N_CHUNKS, 1, TOK_CHUNK), slot2.reshape(N_CHUNKS, 1, TOK_CHUNK), pad],
                                 axis=1)
    offs_row = jnp.concatenate([offs.astype(F32), jnp.zeros((LANE - N_EXPERTS,), F32)]).reshape(1, LANE)
    return dict(tile_expert=tile_expert, n_active=n_active.reshape(1).astype(I32), c_lo=c_lo, c_hi=c_hi,
                ids=ids.reshape(-1), experts=experts.reshape(-1), n_pairs=n_pairs,
                slots_lane=slots_lane, offs_row=offs_row)


GATHER_TILE = 256


def _gather_kernel(clo_ref, chi_ref, slots_ref, h_ref, o_ref, acc_ref):
    g = pl.program_id(0)
    slot_id = g * GATHER_TILE + lax.broadcasted_iota(I32, (GATHER_TILE, TOK_CHUNK), 0)
    acc_ref[...] = jnp.zeros_like(acc_ref)

    def body(c, carry):
        sl = slots_ref[c]
        hit = (sl[0:1, :] == slot_id) | (sl[1:2, :] == slot_id)
        rows = h_ref[pl.ds(pl.multiple_of(c * TOK_CHUNK, TOK_CHUNK), TOK_CHUNK), :]
        acc_ref[...] += _dot(jnp.where(hit, 1.0, 0.0).astype(BF16), rows)
        return carry

    lax.fori_loop(clo_ref[g], chi_ref[g] + 1, body, 0)
    o_ref[...] = acc_ref[...].astype(BF16)


def _gather(h, rt):
    return pl.pallas_call(
        _gather_kernel,
        grid_spec=pltpu.PrefetchScalarGridSpec(
            num_scalar_prefetch=2,
            grid=(N_SLOTS // GATHER_TILE,),
            in_specs=[
                pl.BlockSpec((N_CHUNKS, SUBLANE, TOK_CHUNK), lambda g, *_: (0, 0, 0)),
                pl.BlockSpec((N_TOK, D_MODEL), lambda g, *_: (0, 0), pipeline_mode=pl.Buffered(1)),
            ],
            out_specs=pl.BlockSpec((GATHER_TILE, D_MODEL), lambda g, *_: (g, 0)),
            scratch_shapes=[pltpu.VMEM((GATHER_TILE, D_MODEL), F32)],
        ),
        out_shape=jax.ShapeDtypeStruct((N_SLOTS, D_MODEL), BF16),
        compiler_params=_cparams("arbitrary"),
        name="moe_gather",
    )(rt["c_lo"], rt["c_hi"], rt["slots_lane"], h)


def _experts_kernel(te_ref, na_ref, h_ref, w1_ref, w3_ref, w2_ref, o_ref):
    g = pl.program_id(0)

    @pl.when(g < na_ref[0])
    def _():
        h = h_ref[...].astype(w1_ref.dtype)
        t = _silu(_dot(h, w1_ref[...])) * _dot(h, w3_ref[...])
        o_ref[...] = _dot(t.astype(w2_ref.dtype), w2_ref[...]).astype(BF16)

    @pl.when(g >= na_ref[0])
    def _():
        o_ref[...] = jnp.zeros_like(o_ref)


def _experts(hs, rt, w1, w3, w2):
    tile = lambda g, te, na: (jnp.minimum(g, na[0] - 1), 0)
    by_expert = lambda g, te, na: (te[g], 0, 0)
    return pl.pallas_call(
        _experts_kernel,
        grid_spec=pltpu.PrefetchScalarGridSpec(
            num_scalar_prefetch=2,
            grid=(N_SLOT_TILES,),
            in_specs=[
                pl.BlockSpec((SLOT_TILE, D_MODEL), tile),
                pl.BlockSpec((None, D_MODEL, D_FF_EXPERT), by_expert),
                pl.BlockSpec((None, D_MODEL, D_FF_EXPERT), by_expert),
                pl.BlockSpec((None, D_FF_EXPERT, D_MODEL), by_expert),
            ],
            out_specs=pl.BlockSpec((SLOT_TILE, D_MODEL), lambda g, te, na: (g, 0)),
        ),
        out_shape=jax.ShapeDtypeStruct((N_SLOTS, D_MODEL), BF16),
        compiler_params=_cparams("arbitrary"),
        name="moe_experts",
    )(rt["tile_expert"], rt["n_active"], hs, w1, w3, w2)


def _combine_kernel(np_ref, ids_ref, ex_ref, route_ref, offs_ref, x_ref, mod_ref, y_hbm,
                    op_ref, os_ref, buf_ref, acc_ref, sem):
    c = pl.program_id(0)
    cur = c % 2

    def chunk_copy(step, j, half):
        chunk = ids_ref[step * MAX_PAIRS + j]
        src = y_hbm.at[pl.ds(pl.multiple_of(chunk * SLOT_CHUNK, SLOT_CHUNK), SLOT_CHUNK)]
        return pltpu.make_async_copy(src, buf_ref.at[half, j], sem.at[half, j])

    def start_all(step, half):
        def go(j, carry):
            chunk_copy(step, j, half).start()
            return carry

        lax.fori_loop(0, np_ref[step], go, 0)

    @pl.when(c == 0)
    def _():
        start_all(0, 0)

    @pl.when(c + 1 < pl.num_programs(0))
    def _():
        start_all(c + 1, 1 - cur)

    route = route_ref[...]
    lanef = lax.broadcasted_iota(I32, route.shape, 1).astype(F32)
    offs = offs_ref[...]

    def slot_of(e_lane, rank_lane):
        start = jnp.sum(jnp.where(lanef == route[:, e_lane:e_lane + 1], offs, 0.0), axis=-1, keepdims=True)
        return start + route[:, rank_lane:rank_lane + 1]

    col = lax.broadcasted_iota(I32, (TOK_CHUNK, SLOT_CHUNK), 1).astype(F32)
    s1 = slot_of(ROUTE_E1, ROUTE_RANK1) - col
    s2 = slot_of(ROUTE_E2, ROUTE_RANK2) - col
    acc_ref[...] = jnp.zeros_like(acc_ref)

    def body(j, carry):
        pair = c * MAX_PAIRS + j
        chunk_copy(c, j, cur).wait()
        base = (ids_ref[pair] * SLOT_CHUNK).astype(F32)
        hit = (s1 == base) | (s2 == base)
        part = _dot(jnp.where(hit, 1.0, 0.0).astype(BF16), buf_ref[cur, j])
        acc_ref[...] += _lane_pick(route, ex_ref[pair]) * part
        return carry

    lax.fori_loop(0, np_ref[c], body, 0)
    res = x_ref[...] + _mod_part(mod_ref[...], 5) * acc_ref[...]

    @pl.when(c * TOK_CHUNK < NP_TOK)
    def _():
        op_ref[...] = res

    @pl.when(c * TOK_CHUNK >= NP_TOK)
    def _():
        os_ref[...] = res


def _combine(ys, route, x, mods, rt, layer):
    tok = lambda c, *_: (c, 0)
    return pl.pallas_call(
        _combine_kernel,
        grid_spec=pltpu.PrefetchScalarGridSpec(
            num_scalar_prefetch=3,
            grid=(N_CHUNKS,),
            in_specs=[
                pl.BlockSpec((TOK_CHUNK, LANE), tok),
                pl.BlockSpec((1, LANE), lambda c, *_: (0, 0)),
                pl.BlockSpec((TOK_CHUNK, D_MODEL), tok),
                _mod_spec(layer, TOK_CHUNK),
                pl.BlockSpec(memory_space=pl.ANY),
            ],
            out_specs=_split_specs(TOK_CHUNK),
            scratch_shapes=[
                pltpu.VMEM((2, MAX_PAIRS, SLOT_CHUNK, D_MODEL), BF16),
                pltpu.VMEM((TOK_CHUNK, D_MODEL), F32),
                pltpu.SemaphoreType.DMA((2, MAX_PAIRS)),
            ],
        ),
        out_shape=[jax.ShapeDtypeStruct((NP_TOK, D_MODEL), F32), jax.ShapeDtypeStruct((NS_TOK, D_MODEL), F32)],
        compiler_params=_cparams("arbitrary"),
        name="moe_combine",
    )(rt["n_pairs"], rt["ids"], rt["experts"], route, rt["offs_row"], x, mods, ys)


def _rope_partner(t):
    half = AXIS_ROPE_DIM // 2
    s = t.shape[:-1]
    return t.reshape(s + (2, 2, half))[..., ::-1, :].reshape(s + (ROPE_DIM,))


def _rope_tables(n_tokens):
    rows = n_tokens // GRID_W
    row = jnp.repeat(jnp.arange(rows), GRID_W).astype(F32)
    col = jnp.tile(jnp.arange(GRID_W), rows).astype(F32)
    inv = ROPE_BASE ** (-jnp.arange(0, AXIS_ROPE_DIM, 2, dtype=F32) / AXIS_ROPE_DIM)
    ar, ac = row[:, None] * inv, col[:, None] * inv
    cos = jnp.concatenate([jnp.cos(ar), jnp.cos(ar), jnp.cos(ac), jnp.cos(ac)], axis=-1)
    sin = jnp.concatenate([-jnp.sin(ar), jnp.sin(ar), -jnp.sin(ac), jnp.sin(ac)], axis=-1)
    zeros = jnp.zeros_like(cos)
    return jnp.concatenate([cos, zeros], axis=-1), jnp.concatenate([sin, zeros], axis=-1)


def _qk_gain_rows(g):
    z = jnp.zeros((ROPE_DIM,), F32)
    rows = jnp.stack([g[:QK_NOPE_DIM],
                      jnp.concatenate([g[QK_NOPE_DIM:], z]),
                      jnp.concatenate([_rope_partner(g[QK_NOPE_DIM:]), z])])
    return jnp.concatenate([rows, jnp.zeros((SUBLANE - 3, LANE), F32)])


def kernel(x_prompt, x_sample, c, cache_ckv, cache_krope, c_ctx, ada_w, ada_b, norm1_g, norm2_g, conv_pw1, conv_dw, conv_dw_b, conv_ln_g, conv_ln_b, conv_pw2, ffn_w1, ffn_w3, ffn_w2, mla_wdq, mla_q_norm_g, mla_wuq, mla_wdkv, mla_kv_norm_g, mla_wukv, mla_q_qk_g, mla_k_qk_g, mla_wo, moe_router, moe_w1, moe_w3, moe_w2):
    xp, xs = x_prompt.reshape(NP_TOK, D_MODEL), x_sample.reshape(NS_TOK, D_MODEL)
    cond16 = jnp.concatenate([c_ctx[None, :], c, jnp.zeros((MOD_ROWS - 1 - DEC_BATCH, D_MODEL), F32)])
    mods = _adaln(cond16, ada_w, ada_b).reshape(2, MOD_ROWS, 1, N_MOD * D_MODEL)
    vec = lambda a: a.reshape(1, -1)

    u = _glu(xp, xs, mods, vec(norm1_g[0]), conv_pw1[0].astype(BF16), 0)
    dw = jnp.concatenate([conv_dw[0], jnp.zeros((1, D_MODEL), F32)])
    x = _conv(u, xp, xs, mods, dw, vec(conv_dw_b[0]), vec(conv_ln_g[0]), vec(conv_ln_b[0]),
              conv_pw2[0].astype(BF16), 0)
    x = _ffn(x, mods, vec(norm2_g[0]), ffn_w1[0].astype(BF16), ffn_w3[0].astype(BF16),
             ffn_w2[0].astype(BF16), 0)

    wdkv = mla_wdkv[0]
    w_down = jnp.concatenate([mla_wdq[0], wdkv, _rope_partner(wdkv[:, KV_LORA_RANK:])], axis=1).astype(BF16)
    cq, ckv, ckv_b, kr2 = _mla_down(x, mods, vec(norm1_g[1]), w_down, vec(mla_q_norm_g[0]),
                                    vec(mla_kv_norm_g[0]), 1)

    wuq = mla_wuq[0].reshape(Q_LORA_RANK, N_HEADS, QK_HEAD_DIM)
    wuq = jnp.concatenate([wuq, _rope_partner(wuq[..., QK_NOPE_DIM:])], axis=-1)
    wuq = wuq.transpose(1, 0, 2).astype(BF16)
    wukv = mla_wukv[0].reshape(KV_LORA_RANK, N_HEADS, HEAD_PAD).transpose(1, 0, 2).astype(BF16)
    gq, gk = _qk_gain_rows(mla_q_qk_g[0]), _qk_gain_rows(mla_k_qk_g[0])
    ckr = cache_krope[:, 0]
    cache = (cache_ckv[:, 0], jnp.concatenate([ckr, _rope_partner(ckr)], axis=-1))
    o_p = _attention(cq, ckv_b, kr2, None, wuq, wukv, gq, gk, None, tok0=0, n_tok=NP_TOK, seq=SEQ)
    o_s = _attention(cq, ckv_b, kr2, cache, wuq, wukv, gq, gk, _rope_tables(DEC_SEQ),
                     tok0=NP_TOK, n_tok=NS_TOK, seq=DEC_SEQ)

    wr = jnp.concatenate([moe_router[0], jnp.zeros((D_MODEL, LANE - N_EXPERTS), F32)], axis=1)
    wo = mla_wo[0].astype(BF16).reshape(N_HEAD_GROUPS, HEAD_GROUP * V_HEAD_DIM, D_MODEL)
    x, h, route, route_t, cstart, total = _attn_out(o_p, o_s, x, mods, vec(norm2_g[1]), wo, wr, 1)
    rt = _routing_tables(route_t, cstart, total)
    hs = _gather(h, rt)
    ys = _experts(hs, rt, moe_w1[0], moe_w3[0], moe_w2[0])
    yp, ysamp = _combine(ys, route, x, mods, rt, 1)

    new_ckv = ckv[:NP_TOK].reshape(BATCH, 1, SEQ, KV_LORA_RANK)
    new_krope = kr2[:NP_TOK, :ROPE_DIM].reshape(BATCH, 1, SEQ, ROPE_DIM)
    return (yp.reshape(BATCH, SEQ, D_MODEL), ysamp.reshape(DEC_BATCH, DEC_SEQ, D_MODEL), new_ckv, new_krope)
```

```python
import functools

import jax
import jax.numpy as jnp
from jax import lax
from jax.experimental import pallas as pl
from jax.experimental.pallas import tpu as pltpu

D_MODEL = 1024
BATCH = 32
SEQ = 256
DEC_BATCH = 8
DEC_SEQ = 1024
PAST_LEN = 512
GRID_W = 64
N_MOD = 6
CONV_WIDTH = 31
CONV_PAD = CONV_WIDTH // 2
N_HEADS = 16
QK_NOPE_DIM = 128
ROPE_DIM = 64
QK_HEAD_DIM = QK_NOPE_DIM + ROPE_DIM
V_HEAD_DIM = 128
Q_LORA_RANK = 512
KV_LORA_RANK = 256
AXIS_ROPE_DIM = ROPE_DIM // 2
ROPE_BASE = 10000.0
D_FF = 2816
N_EXPERTS = 8
TOP_K = 2
D_FF_EXPERT = 1536
EPS = 1e-6
F32 = jnp.float32
BF16 = jnp.bfloat16
I32 = jnp.int32

NP_TOK = BATCH * SEQ
NS_TOK = DEC_BATCH * DEC_SEQ
N_TOK = NP_TOK + NS_TOK
MOD_ROWS = 16
LANE = 128
SUBLANE = 8
HEAD_PAD = 2 * LANE
VMEM_LIMIT = 56 * 1024 * 1024

TOK_CHUNK = 256
N_CHUNKS = N_TOK // TOK_CHUNK
SLOT_TILE = 512
N_SLOT_TILES = (TOP_K * N_TOK + N_EXPERTS * (SLOT_TILE - 1)) // SLOT_TILE
N_SLOTS = N_SLOT_TILES * SLOT_TILE
SLOT_CHUNK = 256
MAX_PAIRS = 2 * N_EXPERTS


def _cparams(*sem):
    return pltpu.CompilerParams(dimension_semantics=sem, vmem_limit_bytes=VMEM_LIMIT)


def _mod_row(tile, tm):
    start = tile * tm
    return jnp.where(start < NP_TOK, 0, 1 + (start - NP_TOK) // DEC_SEQ)


def _mod_spec(layer, tm):
    return pl.BlockSpec((None, None, 1, N_MOD * D_MODEL),
                        lambda i, *_: (layer, _mod_row(i, tm), 0, 0))


def _split_specs(tm):
    n_p = NP_TOK // tm
    return [pl.BlockSpec((tm, D_MODEL), lambda i, *_: (jnp.minimum(i, n_p - 1), 0)),
            pl.BlockSpec((tm, D_MODEL), lambda i, *_: (jnp.maximum(i - n_p, 0), 0))]


def _pick_tokens(tm, xp_ref, xs_ref):
    return jnp.where(pl.program_id(0) * tm < NP_TOK, xp_ref[...], xs_ref[...])


def _mod_part(mod, k):
    return mod[:, k * D_MODEL:(k + 1) * D_MODEL]


def _rms(x, g):
    return x * lax.rsqrt(jnp.mean(x * x, axis=-1, keepdims=True) + EPS) * g


def _silu(x):
    return x * jax.nn.sigmoid(x)


def _split_bf16(x):
    hi = x.astype(BF16)
    lo = (x - hi.astype(F32)).astype(BF16)
    return hi, lo


def _dot(a, b):
    return jnp.dot(a, b, preferred_element_type=F32)


def _dot3(a, b):
    ah, al = _split_bf16(a)
    bh, bl = _split_bf16(b)
    return _dot(ah, bh) + (_dot(al, bh) + _dot(ah, bl))


def _lane_pick(x, idx):
    lane = lax.broadcasted_iota(I32, x.shape, 1)
    return jnp.sum(jnp.where(lane == idx, x, 0.0), axis=-1, keepdims=True)


def _adaln_kernel(cond_ref, w_ref, b_ref, o_ref):
    o_ref[...] = _dot3(_silu(cond_ref[...]), w_ref[...]) + b_ref[...]


def _adaln(cond16, ada_w, ada_b):
    depth = ada_w.shape[0]
    tn = 1536
    return pl.pallas_call(
        _adaln_kernel,
        grid=(depth, N_MOD * D_MODEL // tn),
        in_specs=[
            pl.BlockSpec((MOD_ROWS, D_MODEL), lambda l, j: (0, 0)),
            pl.BlockSpec((None, D_MODEL, tn), lambda l, j: (l, 0, j)),
            pl.BlockSpec((None, 1, tn), lambda l, j: (l, 0, j)),
        ],
        out_specs=pl.BlockSpec((None, MOD_ROWS, tn), lambda l, j: (l, 0, j)),
        out_shape=jax.ShapeDtypeStruct((depth, MOD_ROWS, N_MOD * D_MODEL), F32),
        compiler_params=_cparams("arbitrary", "arbitrary"),
        name="adaln",
    )(cond16, ada_w, ada_b.reshape(depth, 1, N_MOD * D_MODEL))


def _glu_kernel(xp_ref, xs_ref, mod_ref, g_ref, w_ref, u_ref):
    mod = mod_ref[...]
    x = _pick_tokens(u_ref.shape[0], xp_ref, xs_ref)
    h = _rms(x, g_ref[...]) * (1.0 + _mod_part(mod, 1)) + _mod_part(mod, 0)
    ag = _dot(h.astype(BF16), w_ref[...])
    u_ref[...] = ag[:, :D_MODEL] * jax.nn.sigmoid(ag[:, D_MODEL:])


def _glu(xp, xs, mods, g, pw1, layer):
    tm = 512
    return pl.pallas_call(
        _glu_kernel,
        grid=(N_TOK // tm,),
        in_specs=_split_specs(tm) + [
            _mod_spec(layer, tm),
            pl.BlockSpec((1, D_MODEL), lambda i: (0, 0)),
            pl.BlockSpec((D_MODEL, 2 * D_MODEL), lambda i: (0, 0)),
        ],
        out_specs=pl.BlockSpec((tm, D_MODEL), lambda i: (i, 0)),
        out_shape=jax.ShapeDtypeStruct((N_TOK, D_MODEL), F32),
        compiler_params=_cparams("arbitrary"),
        name="glu",
    )(xp, xs, mods, g, pw1)


CONV_CHUNK = 256
CONV_HALO = 16
CONV_ROWS = 64
CONV_SHIFT_ROWS = CONV_CHUNK + (CONV_HALO - CONV_PAD + CONV_WIDTH - 1) // SUBLANE * SUBLANE


def _conv_kernel(uc_ref, up_ref, un_ref, dw_ref, dwb_ref, lng_ref, lnb_ref, w_ref, xp_ref, xs_ref, mod_ref,
                 o_ref, pad_ref, shift_ref, conv_ref):
    i = pl.program_id(0)
    start = i * CONV_CHUNK
    seq_len = jnp.where(start < NP_TOK, SEQ, DEC_SEQ)
    off = jnp.where(start < NP_TOK, start, start - NP_TOK) % seq_len
    prev_ok = off > 0
    next_ok = off + CONV_CHUNK < seq_len
    pad_ref[0:CONV_HALO, :] = jnp.where(prev_ok, up_ref[...], 0.0)
    pad_ref[CONV_HALO:CONV_HALO + CONV_CHUNK, :] = uc_ref[...]
    pad_ref[CONV_HALO + CONV_CHUNK:, :] = jnp.where(next_ok, un_ref[...], 0.0)

    base = CONV_HALO - CONV_PAD
    for b in range(1, SUBLANE):
        shift_ref[b - 1] = pad_ref[b:b + CONV_SHIFT_ROWS, :]
    for c in range(D_MODEL // LANE):
        cs = slice(c * LANE, (c + 1) * LANE)
        wcol = dw_ref[:, cs]
        bias = dwb_ref[:, cs]
        for r in range(CONV_CHUNK // CONV_ROWS):
            acc = jnp.broadcast_to(bias, (CONV_ROWS, LANE))
            for k in range(CONV_WIDTH):
                a, b = divmod(base + k, SUBLANE)
                lo = r * CONV_ROWS + SUBLANE * a
                src = pad_ref if b == 0 else shift_ref.at[b - 1]
                acc = acc + wcol[k:k + 1, :] * src[lo:lo + CONV_ROWS, cs]
            conv_ref[r * CONV_ROWS:(r + 1) * CONV_ROWS, cs] = acc

    t = conv_ref[...]
    mu = jnp.mean(t, axis=-1, keepdims=True)
    tc = t - mu
    y = tc * lax.rsqrt(jnp.mean(tc * tc, axis=-1, keepdims=True) + EPS) * lng_ref[...] + lnb_ref[...]
    res = _dot(_silu(y).astype(BF16), w_ref[...])
    o_ref[...] = _pick_tokens(CONV_CHUNK, xp_ref, xs_ref) + _mod_part(mod_ref[...], 2) * res


def _conv(u, xp, xs, mods, dw, dwb, lng, lnb, pw2, layer):
    n_chunks = N_TOK // CONV_CHUNK
    halo_per_chunk = CONV_CHUNK // CONV_HALO
    n_halo = N_TOK // CONV_HALO
    row = lambda i: (i, 0)
    const = lambda i: (0, 0)
    return pl.pallas_call(
        _conv_kernel,
        grid=(n_chunks,),
        in_specs=[
            pl.BlockSpec((CONV_CHUNK, D_MODEL), row),
            pl.BlockSpec((CONV_HALO, D_MODEL), lambda i: (jnp.maximum(i * halo_per_chunk - 1, 0), 0)),
            pl.BlockSpec((CONV_HALO, D_MODEL),
                         lambda i: (jnp.minimum((i + 1) * halo_per_chunk, n_halo - 1), 0)),
            pl.BlockSpec((CONV_WIDTH + 1, D_MODEL), const),
            pl.BlockSpec((1, D_MODEL), const),
            pl.BlockSpec((1, D_MODEL), const),
            pl.BlockSpec((1, D_MODEL), const),
            pl.BlockSpec((D_MODEL, D_MODEL), const),
        ] + _split_specs(CONV_CHUNK) + [
            _mod_spec(layer, CONV_CHUNK),
        ],
        out_specs=pl.BlockSpec((CONV_CHUNK, D_MODEL), row),
        out_shape=jax.ShapeDtypeStruct((N_TOK, D_MODEL), F32),
        scratch_shapes=[
            pltpu.VMEM((CONV_CHUNK + 2 * CONV_HALO, D_MODEL), F32),
            pltpu.VMEM((SUBLANE - 1, CONV_SHIFT_ROWS, D_MODEL), F32),
            pltpu.VMEM((CONV_CHUNK, D_MODEL), F32),
        ],
        compiler_params=_cparams("arbitrary"),
        name="conv",
    )(u, u, u, dw, dwb, lng, lnb, pw2, xp, xs, mods)


FFN_CHUNK = 256


def _ffn_kernel(x_ref, mod_ref, g_ref, w1_ref, w3_ref, w2_ref, o_ref):
    mod = mod_ref[...]
    x = x_ref[...]
    h = (_rms(x, g_ref[...]) * (1.0 + _mod_part(mod, 4)) + _mod_part(mod, 3)).astype(BF16)
    y = None
    for j in range(D_FF // FFN_CHUNK):
        cols = slice(j * FFN_CHUNK, (j + 1) * FFN_CHUNK)
        t = _silu(_dot(h, w1_ref[:, cols])) * _dot(h, w3_ref[:, cols])
        part = _dot(t.astype(BF16), w2_ref[cols, :])
        y = part if y is None else y + part
    o_ref[...] = x + _mod_part(mod, 5) * y


def _ffn(x, mods, g, w1, w3, w2, layer):
    tm = 512
    const = lambda i: (0, 0)
    resident = dict(pipeline_mode=pl.Buffered(1))
    return pl.pallas_call(
        _ffn_kernel,
        grid=(N_TOK // tm,),
        in_specs=[
            pl.BlockSpec((tm, D_MODEL), lambda i: (i, 0)),
            _mod_spec(layer, tm),
            pl.BlockSpec((1, D_MODEL), const),
            pl.BlockSpec((D_MODEL, D_FF), const, **resident),
            pl.BlockSpec((D_MODEL, D_FF), const, **resident),
            pl.BlockSpec((D_FF, D_MODEL), const, **resident),
        ],
        out_specs=pl.BlockSpec((tm, D_MODEL), lambda i: (i, 0)),
        out_shape=jax.ShapeDtypeStruct((N_TOK, D_MODEL), F32),
        compiler_params=_cparams("arbitrary"),
        name="ffn",
    )(x, mods, g, w1, w3, w2)


def _mla_down_kernel(x_ref, mod_ref, g_ref, w_ref, qg_ref, kvg_ref, cq_ref, ckvb_ref, kr_ref, new_ckv_ref, new_kr_ref):
    mod = mod_ref[...]
    h = _rms(x_ref[...], g_ref[...]) * (1.0 + _mod_part(mod, 1)) + _mod_part(mod, 0)
    d = _dot(h.astype(BF16), w_ref[...])
    cq_ref[...] = _rms(d[:, :Q_LORA_RANK], qg_ref[...]).astype(BF16)
    ckv = _rms(d[:, Q_LORA_RANK:Q_LORA_RANK + KV_LORA_RANK], kvg_ref[...])
    ckvb_ref[...] = ckv.astype(BF16)
    kr = d[:, Q_LORA_RANK + KV_LORA_RANK:]
    kr_ref[...] = kr

    @pl.when(pl.program_id(0) * x_ref.shape[0] < NP_TOK)
    def _():
        new_ckv_ref[...] = ckv
        new_kr_ref[...] = kr[:, :ROPE_DIM]


def _mla_down(x, mods, g, w_down, qg, kvg, layer):
    tm = 512
    n_p = NP_TOK // tm
    n_down = Q_LORA_RANK + KV_LORA_RANK + LANE
    return pl.pallas_call(
        _mla_down_kernel,
        grid=(N_TOK // tm,),
        in_specs=[
            pl.BlockSpec((tm, D_MODEL), lambda i: (i, 0)),
            _mod_spec(layer, tm),
            pl.BlockSpec((1, D_MODEL), lambda i: (0, 0)),
            pl.BlockSpec((D_MODEL, n_down), lambda i: (0, 0)),
            pl.BlockSpec((1, Q_LORA_RANK), lambda i: (0, 0)),
            pl.BlockSpec((1, KV_LORA_RANK), lambda i: (0, 0)),
        ],
        out_specs=[
            pl.BlockSpec((tm, Q_LORA_RANK), lambda i: (i, 0)),
            pl.BlockSpec((tm, KV_LORA_RANK), lambda i: (i, 0)),
            pl.BlockSpec((tm, LANE), lambda i: (i, 0)),
            pl.BlockSpec((tm, KV_LORA_RANK), lambda i: (jnp.minimum(i, n_p - 1), 0)),
            pl.BlockSpec((tm, ROPE_DIM), lambda i: (jnp.minimum(i, n_p - 1), 0)),
        ],
        out_shape=[
            jax.ShapeDtypeStruct((N_TOK, Q_LORA_RANK), BF16),
            jax.ShapeDtypeStruct((N_TOK, KV_LORA_RANK), BF16),
            jax.ShapeDtypeStruct((N_TOK, LANE), F32),
            jax.ShapeDtypeStruct((NP_TOK, KV_LORA_RANK), F32),
            jax.ShapeDtypeStruct((NP_TOK, ROPE_DIM), F32),
        ],
        compiler_params=_cparams("arbitrary"),
        name="mla_down",
    )(x, mods, g, w_down, qg, kvg)


ATTN_BLOCK = 1024
ATTN_TQ = 256
HEAD_GROUP = 2
N_HEAD_GROUPS = N_HEADS // HEAD_GROUP
LOG2E = 1.4426950408889634


def _attn_kernel(*refs, seq, n_cache, rope):
    it = iter(refs)
    cq_ref, ckv_ref, kr_ref = next(it), next(it), next(it)
    if n_cache:
        cckv_ref, ckr_ref = next(it), next(it)
    wuq_ref, wukv_ref, gq_ref, gk_ref = next(it), next(it), next(it), next(it)
    if rope:
        cos_ref, sin_ref = next(it), next(it)
    o_ref, k_scr, v_scr, q_scr = next(it), next(it), next(it), next(it)

    inv_dim = 1.0 / QK_HEAD_DIM
    gq, gk = gq_ref[...], gk_ref[...]
    cq, ckv, kr = cq_ref[...], ckv_ref[...], kr_ref[...]
    tables = (cos_ref[...], sin_ref[...]) if rope else None
    ones_new = jnp.ones((ATTN_BLOCK, LANE), BF16)
    if n_cache:
        cckv, ckr = cckv_ref[...].astype(BF16), ckr_ref[...]
        ones_cache = jnp.ones((n_cache, LANE), BF16)

    def normed(nope, rot2, g, tabs, out_scale):
        ssq = jnp.sum(nope * nope, axis=-1, keepdims=True) + 0.5 * jnp.sum(rot2 * rot2, axis=-1, keepdims=True)
        r = lax.rsqrt(ssq * inv_dim + EPS) * out_scale
        if tabs is None:
            rot = rot2 * g[1:2, :]
        else:
            rot = rot2 * (g[1:2, :] * tabs[0]) + pltpu.roll(rot2, ROPE_DIM, 1) * (g[2:3, :] * tabs[1])
        return jnp.concatenate([(nope * r * g[0:1, :]).astype(BF16), (rot * r).astype(BF16)], axis=1)

    def build(gi, slot):
        for j in range(HEAD_GROUP):
            h = gi * HEAD_GROUP + j
            wukv = wukv_ref[h]
            kv = _dot(ckv, wukv)
            k_scr[slot, j, 0:ATTN_BLOCK, :] = normed(kv[:, :LANE], kr, gk, tables, 1.0)
            v_scr[slot, j, 0:ATTN_BLOCK, :] = jnp.concatenate([kv[:, LANE:].astype(BF16), ones_new], axis=1)
            if n_cache:
                kvc = _dot(cckv, wukv)
                k_scr[slot, j, ATTN_BLOCK:, :] = normed(kvc[:, :LANE], ckr, gk, None, 1.0)
                v_scr[slot, j, ATTN_BLOCK:, :] = jnp.concatenate([kvc[:, LANE:].astype(BF16), ones_cache], axis=1)
            q = _dot(cq, wuq_ref[h])
            q_scr[slot, j] = normed(q[:, :LANE], q[:, LANE:], gq, tables, QK_HEAD_DIM ** -0.5 * LOG2E)

    def attend(gi, slot):
        heads = []
        for j in range(HEAD_GROUP):
            outs = []
            for i in range(ATTN_BLOCK // ATTN_TQ):
                rows = slice(i * ATTN_TQ, (i + 1) * ATTN_TQ)
                keys = slice(None) if seq == ATTN_BLOCK else rows
                s = lax.dot_general(q_scr[slot, j, rows, :], k_scr[slot, j, keys, :], (((1,), (1,)), ((), ())),
                                    preferred_element_type=F32)
                p = jnp.exp2(s - jnp.max(s, axis=-1, keepdims=True)).astype(BF16)
                oe = _dot(p, v_scr[slot, j, keys, :])
                outs.append((oe[:, :LANE] / oe[:, LANE:]).astype(BF16))
            heads.append(jnp.concatenate(outs, axis=0))
        o_ref[gi] = jnp.concatenate(heads, axis=1)

    build(0, 0)

    def two_groups(t, carry):
        g0 = 2 * t
        build(g0 + 1, 1)
        attend(g0, 0)
        build(jnp.minimum(g0 + 2, N_HEAD_GROUPS - 1), 0)
        attend(g0 + 1, 1)
        return carry

    lax.fori_loop(0, N_HEAD_GROUPS // 2, two_groups, 0)


def _attention(cq, ckv, kr2, cache, wuq, wukv, gq, gk, tables, *, tok0, n_tok, seq):
    assert seq in (ATTN_BLOCK, ATTN_TQ) and tok0 % ATTN_BLOCK == 0 and n_tok % ATTN_BLOCK == 0
    b0 = tok0 // ATTN_BLOCK
    n_cache = 0 if cache is None else cache[0].shape[1]
    assert n_cache == 0 or seq == ATTN_BLOCK
    rope = tables is not None
    row = lambda b: (b0 + b, 0)
    const2 = lambda b: (0, 0)
    const3 = lambda b: (0, 0, 0)
    in_specs = [
        pl.BlockSpec((ATTN_BLOCK, Q_LORA_RANK), row),
        pl.BlockSpec((ATTN_BLOCK, KV_LORA_RANK), row),
        pl.BlockSpec((ATTN_BLOCK, LANE), row),
    ]
    args = [cq, ckv, kr2]
    if n_cache:
        in_specs += [pl.BlockSpec((None, n_cache, KV_LORA_RANK), lambda b: (b, 0, 0)),
                     pl.BlockSpec((None, n_cache, LANE), lambda b: (b, 0, 0))]
        args += list(cache)
    in_specs += [
        pl.BlockSpec((N_HEADS, Q_LORA_RANK, HEAD_PAD), const3),
        pl.BlockSpec((N_HEADS, KV_LORA_RANK, HEAD_PAD), const3),
        pl.BlockSpec((SUBLANE, LANE), const2),
        pl.BlockSpec((SUBLANE, LANE), const2),
    ]
    args += [wuq, wukv, gq, gk]
    if rope:
        in_specs += [pl.BlockSpec((ATTN_BLOCK, LANE), const2), pl.BlockSpec((ATTN_BLOCK, LANE), const2)]
        args += list(tables)
    return pl.pallas_call(
        functools.partial(_attn_kernel, seq=seq, n_cache=n_cache, rope=rope),
        grid=(n_tok // ATTN_BLOCK,),
        in_specs=in_specs,
        out_specs=pl.BlockSpec((N_HEAD_GROUPS, ATTN_BLOCK, HEAD_GROUP * V_HEAD_DIM), lambda b: (0, b, 0)),
        out_shape=jax.ShapeDtypeStruct((N_HEAD_GROUPS, n_tok, HEAD_GROUP * V_HEAD_DIM), BF16),
        scratch_shapes=[
            pltpu.VMEM((2, HEAD_GROUP, ATTN_BLOCK + n_cache, HEAD_PAD), BF16),
            pltpu.VMEM((2, HEAD_GROUP, ATTN_BLOCK + n_cache, HEAD_PAD), BF16),
            pltpu.VMEM((2, HEAD_GROUP, ATTN_BLOCK, HEAD_PAD), BF16),
        ],
        compiler_params=_cparams("arbitrary"),
        name="attn_rope" if rope else "attn",
    )(*args)


ROUTE_TM = 2 * TOK_CHUNK
ROUTE_E1, ROUTE_E2, ROUTE_RANK1, ROUTE_RANK2 = N_EXPERTS, N_EXPERTS + 1, N_EXPERTS + 2, N_EXPERTS + 3
ROUTE_ROWS = 16


def _attn_out_kernel(op_ref, os_ref, x_ref, mod_ref, g_ref, wo_ref, wr_ref,
                     x3_ref, h_ref, route_ref, route_t_ref, cstart_ref, total_ref, carry_ref):
    i = pl.program_id(0)
    mod = mod_ref[...]
    is_prompt = i * x_ref.shape[0] < NP_TOK
    att = _dot(jnp.where(is_prompt, op_ref[0], os_ref[0]), wo_ref[0])
    for gi in range(1, N_HEAD_GROUPS):
        att += _dot(jnp.where(is_prompt, op_ref[gi], os_ref[gi]), wo_ref[gi])
    x3 = x_ref[...] + _mod_part(mod, 2) * att
    x3_ref[...] = x3
    h = _rms(x3, g_ref[...]) * (1.0 + _mod_part(mod, 4)) + _mod_part(mod, 3)
    hb = h.astype(BF16)
    h_ref[...] = hb
    wh, wl = _split_bf16(wr_ref[...])
    hl = (h - hb.astype(F32)).astype(BF16)
    logits = _dot(hb, wh) + (_dot(hl, wh) + _dot(hb, wl))
    lane = lax.broadcasted_iota(I32, logits.shape, 1)
    lanef = lane.astype(F32)
    neg = jnp.float32(-jnp.inf)
    logits = jnp.where(lane < N_EXPERTS, logits, neg)
    v1 = jnp.max(logits, axis=-1, keepdims=True)
    i1 = jnp.min(jnp.where(logits == v1, lanef, float(LANE)), axis=-1, keepdims=True)
    rest = jnp.where(lanef == i1, neg, logits)
    v2 = jnp.max(rest, axis=-1, keepdims=True)
    i2 = jnp.min(jnp.where(rest == v2, lanef, float(LANE)), axis=-1, keepdims=True)
    e2 = jnp.exp(v2 - v1)
    w1 = 1.0 / (1.0 + e2)
    hot1, hot2 = lanef == i1, lanef == i2
    gates = jnp.where(hot1, w1, 0.0) + jnp.where(hot2, e2 * w1, 0.0)

    @pl.when(i == 0)
    def _():
        carry_ref[...] = jnp.zeros_like(carry_ref)

    tm = logits.shape[0]
    hot = jnp.where(hot1 | hot2, 1.0, 0.0)
    r_id = lax.broadcasted_iota(I32, (tm, tm), 0)
    c_id = lax.broadcasted_iota(I32, (tm, tm), 1)
    tri = jnp.where(c_id < r_id, 1.0, 0.0).astype(BF16)
    before = _dot(tri, hot.astype(BF16)) + carry_ref[0:1, :]
    rank1 = jnp.sum(jnp.where(hot1, before, 0.0), axis=-1, keepdims=True)
    rank2 = jnp.sum(jnp.where(hot2, before, 0.0), axis=-1, keepdims=True)
    route = jnp.where(lane == ROUTE_E1, i1, jnp.where(lane == ROUTE_E2, i2, jnp.where(
        lane == ROUTE_RANK1, rank1, jnp.where(lane == ROUTE_RANK2, rank2, gates))))
    route_ref[...] = route
    route_t_ref[...] = route.T[:ROUTE_ROWS, :]
    for k in range(tm // TOK_CHUNK):
        cstart_ref[k] = jnp.broadcast_to(before[k * TOK_CHUNK:k * TOK_CHUNK + 1, :], (SUBLANE, LANE))
    total = before[tm - 1:tm, :] + hot[tm - 1:tm, :]
    carry_ref[...] = jnp.broadcast_to(total, (SUBLANE, LANE))
    total_ref[...] = jnp.broadcast_to(total, (SUBLANE, LANE))


def _attn_out(o_p, o_s, x, mods, g, wo, wr, layer):
    tm = ROUTE_TM
    per = tm // TOK_CHUNK
    n_p = NP_TOK // tm
    o_block = (N_HEAD_GROUPS, tm, HEAD_GROUP * V_HEAD_DIM)
    return pl.pallas_call(
        _attn_out_kernel,
        grid=(N_TOK // tm,),
        in_specs=[
            pl.BlockSpec(o_block, lambda i: (0, jnp.minimum(i, n_p - 1), 0)),
            pl.BlockSpec(o_block, lambda i: (0, jnp.maximum(i - n_p, 0), 0)),
            pl.BlockSpec((tm, D_MODEL), lambda i: (i, 0)),
            _mod_spec(layer, tm),
            pl.BlockSpec((1, D_MODEL), lambda i: (0, 0)),
            pl.BlockSpec((N_HEAD_GROUPS, HEAD_GROUP * V_HEAD_DIM, D_MODEL), lambda i: (0, 0, 0)),
            pl.BlockSpec((D_MODEL, LANE), lambda i: (0, 0)),
        ],
        out_specs=[
            pl.BlockSpec((tm, D_MODEL), lambda i: (i, 0)),
            pl.BlockSpec((tm, D_MODEL), lambda i: (i, 0)),
            pl.BlockSpec((tm, LANE), lambda i: (i, 0)),
            pl.BlockSpec((ROUTE_ROWS, tm), lambda i: (0, i)),
            pl.BlockSpec((per, SUBLANE, LANE), lambda i: (i, 0, 0)),
            pl.BlockSpec((SUBLANE, LANE), lambda i: (0, 0)),
        ],
        out_shape=[
            jax.ShapeDtypeStruct((N_TOK, D_MODEL), F32),
            jax.ShapeDtypeStruct((N_TOK, D_MODEL), BF16),
            jax.ShapeDtypeStruct((N_TOK, LANE), F32),
            jax.ShapeDtypeStruct((ROUTE_ROWS, N_TOK), F32),
            jax.ShapeDtypeStruct((N_CHUNKS, SUBLANE, LANE), F32),
            jax.ShapeDtypeStruct((SUBLANE, LANE), F32),
        ],
        scratch_shapes=[pltpu.VMEM((SUBLANE, LANE), F32)],
        compiler_params=_cparams("arbitrary"),
        name="attn_out",
    )(o_p, o_s, x, mods, g, wo, wr)


def _routing_tables(route_t, cstart, total):
    counts = total[0, :N_EXPERTS].astype(I32)
    padded = (counts + SLOT_TILE - 1) // SLOT_TILE * SLOT_TILE
    ends = jnp.cumsum(padded)
    offs = ends - padded
    expert_ids = jnp.arange(N_EXPERTS, dtype=I32)[:, None]

    def region_start(e_row):
        return jnp.sum(jnp.where(e_row[None, :].astype(I32) == expert_ids, offs[:, None], 0), axis=0)

    slot1 = region_start(route_t[ROUTE_E1]) + route_t[ROUTE_RANK1].astype(I32)
    slot2 = region_start(route_t[ROUTE_E2]) + route_t[ROUTE_RANK2].astype(I32)

    n_active = ends[-1] // SLOT_TILE
    tile_start = jnp.arange(N_SLOT_TILES, dtype=I32) * SLOT_TILE
    tile_expert = jnp.sum(tile_start[:, None] >= ends[None, :], axis=1).astype(I32)
    last_expert = jnp.sum((n_active - 1) * SLOT_TILE >= ends).astype(I32)
    tile_expert = jnp.where(tile_start < ends[-1], tile_expert, last_expert)

    cc = jnp.concatenate([cstart[:, 0, :N_EXPERTS], total[0:1, :N_EXPERTS]]).astype(I32)

    g_start = jnp.arange(N_SLOTS // GATHER_TILE, dtype=I32) * GATHER_TILE
    g_expert = jnp.minimum(jnp.sum(g_start[:, None] >= ends[None, :], axis=1), N_EXPERTS - 1).astype(I32)
    rank0 = g_start - offs[g_expert]
    cc_tile = cc[:, g_expert]
    c_lo = jnp.sum(cc_tile[1:] <= rank0[None, :], axis=0).astype(I32)
    rank_end = jnp.minimum(rank0 + GATHER_TILE, counts[g_expert])
    c_hi = jnp.sum(cc_tile[:-1] < rank_end[None, :], axis=0).astype(I32) - 1
    idle = (g_start >= ends[-1]) | (c_hi < c_lo)
    c_lo = jnp.where(idle, 1, c_lo)
    c_hi = jnp.where(idle, 0, c_hi)

    lo = offs[None, :] + cc[:-1]
    hi = offs[None, :] + cc[1:]
    first, last = lo // SLOT_CHUNK, (hi - 1) // SLOT_CHUNK
    ids = jnp.concatenate([first, last], axis=1)
    valid = jnp.concatenate([hi > lo, (hi > lo) & (last != first)], axis=1)
    experts = jnp.tile(jnp.arange(N_EXPERTS, dtype=I32), (N_CHUNKS, 2))
    order = jnp.argsort(jnp.logical_not(valid), axis=1, stable=True)
    ids = jnp.take_along_axis(ids, order, axis=1).astype(I32)
    experts = jnp.take_along_axis(experts, order, axis=1)
    n_pairs = jnp.sum(valid, axis=1).astype(I32)
    unused = jnp.arange(MAX_PAIRS, dtype=I32)[None, :] >= n_pairs[:, None]
    ids = jnp.where(unused, ids[:, 0:1], ids)
    experts = jnp.where(unused, -1, experts)
    n_pairs = n_pairs + n_pairs % 2

    pad = jnp.zeros((N_CHUNKS, SUBLANE - TOP_K, TOK_CHUNK), I32)
    slots_lane = jnp.concatenate([slot1.reshape(N_CHUNKS, 1, TOK_CHUNK), slot2.reshape(N_CHUNKS, 1, TOK_CHUNK), pad],
                                 axis=1)
    offs_row = jnp.concatenate([offs.astype(F32), jnp.zeros((LANE - N_EXPERTS,), F32)]).reshape(1, LANE)
    return dict(tile_expert=tile_expert, n_active=n_active.reshape(1).astype(I32), c_lo=c_lo, c_hi=c_hi,
                ids=ids.reshape(-1), experts=experts.reshape(-1), n_pairs=n_pairs,
                slots_lane=slots_lane, offs_row=offs_row)


GATHER_TILE = 256


def _gather_kernel(clo_ref, chi_ref, slots_ref, h_ref, o_ref, acc_ref):
    g = pl.program_id(0)
    slot_id = g * GATHER_TILE + lax.broadcasted_iota(I32, (GATHER_TILE, TOK_CHUNK), 0)
    acc_ref[...] = jnp.zeros_like(acc_ref)

    c_lo, c_hi = clo_ref[g], chi_ref[g]

    def one_hot(c, value):
        sl = slots_ref[c]
        hit = (sl[0:1, :] == slot_id) | (sl[1:2, :] == slot_id)
        return jnp.where(hit, value, 0.0).astype(BF16)

    def rows(c):
        return h_ref[pl.ds(pl.multiple_of(c * TOK_CHUNK, TOK_CHUNK), TOK_CHUNK), :]

    def body(t, carry):
        c0 = c_lo + 2 * t
        c1 = jnp.minimum(c0 + 1, c_hi)
        live1 = jnp.where(c0 + 1 <= c_hi, 1.0, 0.0)
        acc_ref[...] += _dot(one_hot(c0, 1.0), rows(c0)) + _dot(one_hot(c1, live1), rows(c1))
        return carry

    lax.fori_loop(0, (c_hi - c_lo + 2) // 2, body, 0)
    o_ref[...] = acc_ref[...].astype(BF16)


def _gather(h, rt):
    return pl.pallas_call(
        _gather_kernel,
        grid_spec=pltpu.PrefetchScalarGridSpec(
            num_scalar_prefetch=2,
            grid=(N_SLOTS // GATHER_TILE,),
            in_specs=[
                pl.BlockSpec((N_CHUNKS, SUBLANE, TOK_CHUNK), lambda g, *_: (0, 0, 0)),
                pl.BlockSpec((N_TOK, D_MODEL), lambda g, *_: (0, 0), pipeline_mode=pl.Buffered(1)),
            ],
            out_specs=pl.BlockSpec((GATHER_TILE, D_MODEL), lambda g, *_: (g, 0)),
            scratch_shapes=[pltpu.VMEM((GATHER_TILE, D_MODEL), F32)],
        ),
        out_shape=jax.ShapeDtypeStruct((N_SLOTS, D_MODEL), BF16),
        compiler_params=_cparams("arbitrary"),
        name="moe_gather",
    )(rt["c_lo"], rt["c_hi"], rt["slots_lane"], h)


def _experts_kernel(te_ref, na_ref, h_ref, w1_ref, w3_ref, w2_ref, o_ref):
    g = pl.program_id(0)

    @pl.when(g < na_ref[0])
    def _():
        h = h_ref[...].astype(w1_ref.dtype)
        t = _silu(_dot(h, w1_ref[...])) * _dot(h, w3_ref[...])
        o_ref[...] = _dot(t.astype(w2_ref.dtype), w2_ref[...]).astype(BF16)

    @pl.when(g >= na_ref[0])
    def _():
        o_ref[...] = jnp.zeros_like(o_ref)


def _experts(hs, rt, w1, w3, w2):
    tile = lambda g, te, na: (jnp.minimum(g, na[0] - 1), 0)
    by_expert = lambda g, te, na: (te[g], 0, 0)
    return pl.pallas_call(
        _experts_kernel,
        grid_spec=pltpu.PrefetchScalarGridSpec(
            num_scalar_prefetch=2,
            grid=(N_SLOT_TILES,),
            in_specs=[
                pl.BlockSpec((SLOT_TILE, D_MODEL), tile),
                pl.BlockSpec((None, D_MODEL, D_FF_EXPERT), by_expert),
                pl.BlockSpec((None, D_MODEL, D_FF_EXPERT), by_expert),
                pl.BlockSpec((None, D_FF_EXPERT, D_MODEL), by_expert),
            ],
            out_specs=pl.BlockSpec((SLOT_TILE, D_MODEL), lambda g, te, na: (g, 0)),
        ),
        out_shape=jax.ShapeDtypeStruct((N_SLOTS, D_MODEL), BF16),
        compiler_params=_cparams("arbitrary"),
        name="moe_experts",
    )(rt["tile_expert"], rt["n_active"], hs, w1, w3, w2)


def _combine_kernel(np_ref, ids_ref, ex_ref, route_ref, offs_ref, x_ref, mod_ref, y_hbm,
                    op_ref, os_ref, buf_ref, acc_ref, sem):
    c = pl.program_id(0)
    cur = c % 2

    def chunk_copy(step, j, half):
        chunk = ids_ref[step * MAX_PAIRS + j]
        src = y_hbm.at[pl.ds(pl.multiple_of(chunk * SLOT_CHUNK, SLOT_CHUNK), SLOT_CHUNK)]
        return pltpu.make_async_copy(src, buf_ref.at[half, j], sem.at[half, j])

    def start_all(step, half):
        def go(j, carry):
            chunk_copy(step, j, half).start()
            return carry

        lax.fori_loop(0, np_ref[step], go, 0)

    @pl.when(c == 0)
    def _():
        start_all(0, 0)

    @pl.when(c + 1 < pl.num_programs(0))
    def _():
        start_all(c + 1, 1 - cur)

    route = route_ref[...]
    lanef = lax.broadcasted_iota(I32, route.shape, 1).astype(F32)
    offs = offs_ref[...]

    def slot_of(e_lane, rank_lane):
        start = jnp.sum(jnp.where(lanef == route[:, e_lane:e_lane + 1], offs, 0.0), axis=-1, keepdims=True)
        return start + route[:, rank_lane:rank_lane + 1]

    col = lax.broadcasted_iota(I32, (TOK_CHUNK, SLOT_CHUNK), 1).astype(F32)
    s1 = slot_of(ROUTE_E1, ROUTE_RANK1) - col
    s2 = slot_of(ROUTE_E2, ROUTE_RANK2) - col
    acc_ref[...] = jnp.zeros_like(acc_ref)

    def part(j):
        pair = c * MAX_PAIRS + j
        base = (ids_ref[pair] * SLOT_CHUNK).astype(F32)
        hit = (s1 == base) | (s2 == base)
        rows = _dot(jnp.where(hit, 1.0, 0.0).astype(BF16), buf_ref[cur, j])
        return _lane_pick(route, ex_ref[pair]) * rows

    def body(t, carry):
        chunk_copy(c, 2 * t, cur).wait()
        chunk_copy(c, 2 * t + 1, cur).wait()
        acc_ref[...] += part(2 * t) + part(2 * t + 1)
        return carry

    lax.fori_loop(0, np_ref[c] // 2, body, 0)
    res = x_ref[...] + _mod_part(mod_ref[...], 5) * acc_ref[...]

    @pl.when(c * TOK_CHUNK < NP_TOK)
    def _():
        op_ref[...] = res

    @pl.when(c * TOK_CHUNK >= NP_TOK)
    def _():
        os_ref[...] = res


def _combine(ys, route, x, mods, rt, layer):
    tok = lambda c, *_: (c, 0)
    return pl.pallas_call(
        _combine_kernel,
        grid_spec=pltpu.PrefetchScalarGridSpec(
            num_scalar_prefetch=3,
            grid=(N_CHUNKS,),
            in_specs=[
                pl.BlockSpec((TOK_CHUNK, LANE), tok),
                pl.BlockSpec((1, LANE), lambda c, *_: (0, 0)),
                pl.BlockSpec((TOK_CHUNK, D_MODEL), tok),
                _mod_spec(layer, TOK_CHUNK),
                pl.BlockSpec(memory_space=pl.ANY),
            ],
            out_specs=_split_specs(TOK_CHUNK),
            scratch_shapes=[
                pltpu.VMEM((2, MAX_PAIRS, SLOT_CHUNK, D_MODEL), BF16),
                pltpu.VMEM((TOK_CHUNK, D_MODEL), F32),
                pltpu.SemaphoreType.DMA((2, MAX_PAIRS)),
            ],
        ),
        out_shape=[jax.ShapeDtypeStruct((NP_TOK, D_MODEL), F32), jax.ShapeDtypeStruct((NS_TOK, D_MODEL), F32)],
        compiler_params=_cparams("arbitrary"),
        name="moe_combine",
    )(rt["n_pairs"], rt["ids"], rt["experts"], route, rt["offs_row"], x, mods, ys)


def _rope_partner(t):
    half = AXIS_ROPE_DIM // 2
    s = t.shape[:-1]
    return t.reshape(s + (2, 2, half))[..., ::-1, :].reshape(s + (ROPE_DIM,))


def _rope_tables(n_tokens):
    rows = n_tokens // GRID_W
    row = jnp.repeat(jnp.arange(rows), GRID_W).astype(F32)
    col = jnp.tile(jnp.arange(GRID_W), rows).astype(F32)
    inv = ROPE_BASE ** (-jnp.arange(0, AXIS_ROPE_DIM, 2, dtype=F32) / AXIS_ROPE_DIM)
    ar, ac = row[:, None] * inv, col[:, None] * inv
    cos = jnp.concatenate([jnp.cos(ar), jnp.cos(ar), jnp.cos(ac), jnp.cos(ac)], axis=-1)
    sin = jnp.concatenate([-jnp.sin(ar), jnp.sin(ar), -jnp.sin(ac), jnp.sin(ac)], axis=-1)
    zeros = jnp.zeros_like(cos)
    return jnp.concatenate([cos, zeros], axis=-1), jnp.concatenate([sin, zeros], axis=-1)


def _qk_gain_rows(g):
    z = jnp.zeros((ROPE_DIM,), F32)
    rows = jnp.stack([g[:QK_NOPE_DIM],
                      jnp.concatenate([g[QK_NOPE_DIM:], z]),
                      jnp.concatenate([_rope_partner(g[QK_NOPE_DIM:]), z])])
    return jnp.concatenate([rows, jnp.zeros((SUBLANE - 3, LANE), F32)])


def kernel(x_prompt, x_sample, c, cache_ckv, cache_krope, c_ctx, ada_w, ada_b, norm1_g, norm2_g, conv_pw1, conv_dw, conv_dw_b, conv_ln_g, conv_ln_b, conv_pw2, ffn_w1, ffn_w3, ffn_w2, mla_wdq, mla_q_norm_g, mla_wuq, mla_wdkv, mla_kv_norm_g, mla_wukv, mla_q_qk_g, mla_k_qk_g, mla_wo, moe_router, moe_w1, moe_w3, moe_w2):
    xp, xs = x_prompt.reshape(NP_TOK, D_MODEL), x_sample.reshape(NS_TOK, D_MODEL)
    cond16 = jnp.concatenate([c_ctx[None, :], c, jnp.zeros((MOD_ROWS - 1 - DEC_BATCH, D_MODEL), F32)])
    mods = _adaln(cond16, ada_w, ada_b).reshape(2, MOD_ROWS, 1, N_MOD * D_MODEL)
    vec = lambda a: a.reshape(1, -1)

    u = _glu(xp, xs, mods, vec(norm1_g[0]), conv_pw1[0].astype(BF16), 0)
    dw = jnp.concatenate([conv_dw[0], jnp.zeros((1, D_MODEL), F32)])
    x = _conv(u, xp, xs, mods, dw, vec(conv_dw_b[0]), vec(conv_ln_g[0]), vec(conv_ln_b[0]),
              conv_pw2[0].astype(BF16), 0)
    x = _ffn(x, mods, vec(norm2_g[0]), ffn_w1[0].astype(BF16), ffn_w3[0].astype(BF16),
             ffn_w2[0].astype(BF16), 0)

    wdkv = mla_wdkv[0]
    w_down = jnp.concatenate([mla_wdq[0], wdkv, _rope_partner(wdkv[:, KV_LORA_RANK:])], axis=1).astype(BF16)
    cq, ckv_b, kr2, new_ckv, new_krope = _mla_down(x, mods, vec(norm1_g[1]), w_down, vec(mla_q_norm_g[0]),
                                                   vec(mla_kv_norm_g[0]), 1)

    wuq = mla_wuq[0].reshape(Q_LORA_RANK, N_HEADS, QK_HEAD_DIM)
    wuq = jnp.concatenate([wuq, _rope_partner(wuq[..., QK_NOPE_DIM:])], axis=-1)
    wuq = wuq.transpose(1, 0, 2).astype(BF16)
    wukv = mla_wukv[0].reshape(KV_LORA_RANK, N_HEADS, HEAD_PAD).transpose(1, 0, 2).astype(BF16)
    gq, gk = _qk_gain_rows(mla_q_qk_g[0]), _qk_gain_rows(mla_k_qk_g[0])
    ckr = cache_krope[:, 0]
    cache = (cache_ckv[:, 0], jnp.concatenate([ckr, _rope_partner(ckr)], axis=-1))
    o_p = _attention(cq, ckv_b, kr2, None, wuq, wukv, gq, gk, None, tok0=0, n_tok=NP_TOK, seq=SEQ)
    o_s = _attention(cq, ckv_b, kr2, cache, wuq, wukv, gq, gk, _rope_tables(DEC_SEQ),
                     tok0=NP_TOK, n_tok=NS_TOK, seq=DEC_SEQ)

    wr = jnp.concatenate([moe_router[0], jnp.zeros((D_MODEL, LANE - N_EXPERTS), F32)], axis=1)
    wo = mla_wo[0].astype(BF16).reshape(N_HEAD_GROUPS, HEAD_GROUP * V_HEAD_DIM, D_MODEL)
    x, h, route, route_t, cstart, total = _attn_out(o_p, o_s, x, mods, vec(norm2_g[1]), wo, wr, 1)
    rt = _routing_tables(route_t, cstart, total)
    hs = _gather(h, rt)
    ys = _experts(hs, rt, moe_w1[0], moe_w3[0], moe_w2[0])
    yp, ysamp = _combine(ys, route, x, mods, rt, 1)

    return (yp.reshape(BATCH, SEQ, D_MODEL), ysamp.reshape(DEC_BATCH, DEC_SEQ, D_MODEL),
            new_ckv.reshape(BATCH, 1, SEQ, KV_LORA_RANK), new_krope.reshape(BATCH, 1, SEQ, ROPE_DIM))
```

```python
import functools

import jax
import jax.numpy as jnp
import numpy as np
from jax import lax
from jax.experimental import pallas as pl
from jax.experimental.pallas import tpu as pltpu

D_MODEL = 1024
BATCH = 32
SEQ = 256
DEC_BATCH = 8
DEC_SEQ = 1024
PAST_LEN = 512
GRID_W = 64
N_MOD = 6
CONV_WIDTH = 31
CONV_PAD = CONV_WIDTH // 2
N_HEADS = 16
QK_NOPE_DIM = 128
ROPE_DIM = 64
QK_HEAD_DIM = QK_NOPE_DIM + ROPE_DIM
V_HEAD_DIM = 128
Q_LORA_RANK = 512
KV_LORA_RANK = 256
AXIS_ROPE_DIM = ROPE_DIM // 2
ROPE_BASE = 10000.0
D_FF = 2816
N_EXPERTS = 8
TOP_K = 2
D_FF_EXPERT = 1536
EPS = 1e-6
F32 = jnp.float32
BF16 = jnp.bfloat16
I32 = jnp.int32

NP_TOK = BATCH * SEQ
NS_TOK = DEC_BATCH * DEC_SEQ
N_TOK = NP_TOK + NS_TOK
MOD_ROWS = 16
LANE = 128
SUBLANE = 8
HEAD_PAD = 2 * LANE
VMEM_LIMIT = 56 * 1024 * 1024

TOK_CHUNK = 256
N_CHUNKS = N_TOK // TOK_CHUNK
SLOT_TILE = 512
N_SLOT_TILES = (TOP_K * N_TOK + N_EXPERTS * (SLOT_TILE - 1)) // SLOT_TILE
N_SLOTS = N_SLOT_TILES * SLOT_TILE
SLOT_CHUNK = 256
MAX_PAIRS = 2 * N_EXPERTS


def _cparams(*sem):
    return pltpu.CompilerParams(dimension_semantics=sem, vmem_limit_bytes=VMEM_LIMIT)


def _mod_row(tile, tm):
    start = tile * tm
    return jnp.where(start < NP_TOK, 0, 1 + (start - NP_TOK) // DEC_SEQ)


def _mod_spec(layer, tm):
    return pl.BlockSpec((None, None, 1, N_MOD * D_MODEL),
                        lambda i, *_: (layer, _mod_row(i, tm), 0, 0))


def _split_specs(tm):
    n_p = NP_TOK // tm
    return [pl.BlockSpec((tm, D_MODEL), lambda i, *_: (jnp.minimum(i, n_p - 1), 0)),
            pl.BlockSpec((tm, D_MODEL), lambda i, *_: (jnp.maximum(i - n_p, 0), 0))]


def _pick_tokens(tm, xp_ref, xs_ref):
    return jnp.where(pl.program_id(0) * tm < NP_TOK, xp_ref[...], xs_ref[...])


def _mod_part(mod, k):
    return mod[:, k * D_MODEL:(k + 1) * D_MODEL]


def _rms(x, g):
    return x * lax.rsqrt(jnp.mean(x * x, axis=-1, keepdims=True) + EPS) * g


def _silu(x):
    return x * jax.nn.sigmoid(x)


def _split_bf16(x):
    hi = x.astype(BF16)
    lo = (x - hi.astype(F32)).astype(BF16)
    return hi, lo


def _dot(a, b):
    return jnp.dot(a, b, preferred_element_type=F32)


def _dot3(a, b):
    ah, al = _split_bf16(a)
    bh, bl = _split_bf16(b)
    return _dot(ah, bh) + (_dot(al, bh) + _dot(ah, bl))


def _lane_pick(x, idx):
    lane = lax.broadcasted_iota(I32, x.shape, 1)
    return jnp.sum(jnp.where(lane == idx, x, 0.0), axis=-1, keepdims=True)


def _adaln_kernel(cond_ref, w_ref, b_ref, o_ref):
    o_ref[...] = _dot3(_silu(cond_ref[...]), w_ref[...]) + b_ref[...]


def _adaln(cond16, ada_w, ada_b):
    depth = ada_w.shape[0]
    tn = 1536
    return pl.pallas_call(
        _adaln_kernel,
        grid=(depth, N_MOD * D_MODEL // tn),
        in_specs=[
            pl.BlockSpec((MOD_ROWS, D_MODEL), lambda l, j: (0, 0)),
            pl.BlockSpec((None, D_MODEL, tn), lambda l, j: (l, 0, j)),
            pl.BlockSpec((None, 1, tn), lambda l, j: (l, 0, j)),
        ],
        out_specs=pl.BlockSpec((None, MOD_ROWS, tn), lambda l, j: (l, 0, j)),
        out_shape=jax.ShapeDtypeStruct((depth, MOD_ROWS, N_MOD * D_MODEL), F32),
        compiler_params=_cparams("arbitrary", "arbitrary"),
        name="adaln",
    )(cond16, ada_w, ada_b.reshape(depth, 1, N_MOD * D_MODEL))


def _glu_kernel(xp_ref, xs_ref, mod_ref, g_ref, w_ref, u_ref):
    mod = mod_ref[...]
    x = _pick_tokens(u_ref.shape[0], xp_ref, xs_ref)
    h = _rms(x, g_ref[...]) * (1.0 + _mod_part(mod, 1)) + _mod_part(mod, 0)
    ag = _dot(h.astype(BF16), w_ref[...])
    u_ref[...] = ag[:, :D_MODEL] * jax.nn.sigmoid(ag[:, D_MODEL:])


def _glu(xp, xs, mods, g, pw1, layer):
    tm = 512
    return pl.pallas_call(
        _glu_kernel,
        grid=(N_TOK // tm,),
        in_specs=_split_specs(tm) + [
            _mod_spec(layer, tm),
            pl.BlockSpec((1, D_MODEL), lambda i: (0, 0)),
            pl.BlockSpec((D_MODEL, 2 * D_MODEL), lambda i: (0, 0)),
        ],
        out_specs=pl.BlockSpec((tm, D_MODEL), lambda i: (i, 0)),
        out_shape=jax.ShapeDtypeStruct((N_TOK, D_MODEL), F32),
        compiler_params=_cparams("arbitrary"),
        name="glu",
    )(xp, xs, mods, g, pw1)


CONV_CHUNK = 256
CONV_HALO = 16
CONV_ROWS = 64
CONV_SHIFT_ROWS = CONV_CHUNK + (CONV_HALO - CONV_PAD + CONV_WIDTH - 1) // SUBLANE * SUBLANE


def _conv_kernel(uc_ref, up_ref, un_ref, dw_ref, dwb_ref, lng_ref, lnb_ref, w_ref, xp_ref, xs_ref, mod_ref,
                 o_ref, pad_ref, shift_ref, conv_ref):
    i = pl.program_id(0)
    start = i * CONV_CHUNK
    seq_len = jnp.where(start < NP_TOK, SEQ, DEC_SEQ)
    off = jnp.where(start < NP_TOK, start, start - NP_TOK) % seq_len
    prev_ok = off > 0
    next_ok = off + CONV_CHUNK < seq_len
    pad_ref[0:CONV_HALO, :] = jnp.where(prev_ok, up_ref[...], 0.0)
    pad_ref[CONV_HALO:CONV_HALO + CONV_CHUNK, :] = uc_ref[...]
    pad_ref[CONV_HALO + CONV_CHUNK:, :] = jnp.where(next_ok, un_ref[...], 0.0)

    base = CONV_HALO - CONV_PAD
    for b in range(1, SUBLANE):
        shift_ref[b - 1] = pad_ref[b:b + CONV_SHIFT_ROWS, :]
    for c in range(D_MODEL // LANE):
        cs = slice(c * LANE, (c + 1) * LANE)
        wcol = dw_ref[:, cs]
        bias = dwb_ref[:, cs]
        for r in range(CONV_CHUNK // CONV_ROWS):
            acc = jnp.broadcast_to(bias, (CONV_ROWS, LANE))
            for k in range(CONV_WIDTH):
                a, b = divmod(base + k, SUBLANE)
                lo = r * CONV_ROWS + SUBLANE * a
                src = pad_ref if b == 0 else shift_ref.at[b - 1]
                acc = acc + wcol[k:k + 1, :] * src[lo:lo + CONV_ROWS, cs]
            conv_ref[r * CONV_ROWS:(r + 1) * CONV_ROWS, cs] = acc

    t = conv_ref[...]
    mu = jnp.mean(t, axis=-1, keepdims=True)
    tc = t - mu
    y = tc * lax.rsqrt(jnp.mean(tc * tc, axis=-1, keepdims=True) + EPS) * lng_ref[...] + lnb_ref[...]
    res = _dot(_silu(y).astype(BF16), w_ref[...])
    o_ref[...] = _pick_tokens(CONV_CHUNK, xp_ref, xs_ref) + _mod_part(mod_ref[...], 2) * res


def _conv(u, xp, xs, mods, dw, dwb, lng, lnb, pw2, layer):
    n_chunks = N_TOK // CONV_CHUNK
    halo_per_chunk = CONV_CHUNK // CONV_HALO
    n_halo = N_TOK // CONV_HALO
    row = lambda i: (i, 0)
    const = lambda i: (0, 0)
    return pl.pallas_call(
        _conv_kernel,
        grid=(n_chunks,),
        in_specs=[
            pl.BlockSpec((CONV_CHUNK, D_MODEL), row),
            pl.BlockSpec((CONV_HALO, D_MODEL), lambda i: (jnp.maximum(i * halo_per_chunk - 1, 0), 0)),
            pl.BlockSpec((CONV_HALO, D_MODEL),
                         lambda i: (jnp.minimum((i + 1) * halo_per_chunk, n_halo - 1), 0)),
            pl.BlockSpec((CONV_WIDTH + 1, D_MODEL), const),
            pl.BlockSpec((1, D_MODEL), const),
            pl.BlockSpec((1, D_MODEL), const),
            pl.BlockSpec((1, D_MODEL), const),
            pl.BlockSpec((D_MODEL, D_MODEL), const),
        ] + _split_specs(CONV_CHUNK) + [
            _mod_spec(layer, CONV_CHUNK),
        ],
        out_specs=pl.BlockSpec((CONV_CHUNK, D_MODEL), row),
        out_shape=jax.ShapeDtypeStruct((N_TOK, D_MODEL), F32),
        scratch_shapes=[
            pltpu.VMEM((CONV_CHUNK + 2 * CONV_HALO, D_MODEL), F32),
            pltpu.VMEM((SUBLANE - 1, CONV_SHIFT_ROWS, D_MODEL), F32),
            pltpu.VMEM((CONV_CHUNK, D_MODEL), F32),
        ],
        compiler_params=_cparams("arbitrary"),
        name="conv",
    )(u, u, u, dw, dwb, lng, lnb, pw2, xp, xs, mods)


FFN_CHUNK = 256


def _ffn_kernel(x_ref, mod_ref, g_ref, w1_ref, w3_ref, w2_ref, o_ref):
    mod = mod_ref[...]
    x = x_ref[...]
    h = (_rms(x, g_ref[...]) * (1.0 + _mod_part(mod, 4)) + _mod_part(mod, 3)).astype(BF16)
    y = None
    for j in range(D_FF // FFN_CHUNK):
        cols = slice(j * FFN_CHUNK, (j + 1) * FFN_CHUNK)
        t = _silu(_dot(h, w1_ref[:, cols])) * _dot(h, w3_ref[:, cols])
        part = _dot(t.astype(BF16), w2_ref[cols, :])
        y = part if y is None else y + part
    o_ref[...] = x + _mod_part(mod, 5) * y


def _ffn(x, mods, g, w1, w3, w2, layer):
    tm = 512
    const = lambda i: (0, 0)
    resident = dict(pipeline_mode=pl.Buffered(1))
    return pl.pallas_call(
        _ffn_kernel,
        grid=(N_TOK // tm,),
        in_specs=[
            pl.BlockSpec((tm, D_MODEL), lambda i: (i, 0)),
            _mod_spec(layer, tm),
            pl.BlockSpec((1, D_MODEL), const),
            pl.BlockSpec((D_MODEL, D_FF), const, **resident),
            pl.BlockSpec((D_MODEL, D_FF), const, **resident),
            pl.BlockSpec((D_FF, D_MODEL), const, **resident),
        ],
        out_specs=pl.BlockSpec((tm, D_MODEL), lambda i: (i, 0)),
        out_shape=jax.ShapeDtypeStruct((N_TOK, D_MODEL), F32),
        compiler_params=_cparams("arbitrary"),
        name="ffn",
    )(x, mods, g, w1, w3, w2)


def _mla_down_kernel(x_ref, mod_ref, g_ref, w_ref, qg_ref, kvg_ref, cq_ref, ckvb_ref, kr_ref, new_ckv_ref, new_kr_ref):
    mod = mod_ref[...]
    h = _rms(x_ref[...], g_ref[...]) * (1.0 + _mod_part(mod, 1)) + _mod_part(mod, 0)
    d = _dot(h.astype(BF16), w_ref[...])
    cq_ref[...] = _rms(d[:, :Q_LORA_RANK], qg_ref[...]).astype(BF16)
    ckv = _rms(d[:, Q_LORA_RANK:Q_LORA_RANK + KV_LORA_RANK], kvg_ref[...])
    ckvb_ref[...] = ckv.astype(BF16)
    kr = d[:, Q_LORA_RANK + KV_LORA_RANK:]
    kr_ref[...] = kr

    @pl.when(pl.program_id(0) * x_ref.shape[0] < NP_TOK)
    def _():
        new_ckv_ref[...] = ckv
        new_kr_ref[...] = kr[:, :ROPE_DIM]


def _mla_down(x, mods, g, w_down, qg, kvg, layer):
    tm = 512
    n_p = NP_TOK // tm
    n_down = Q_LORA_RANK + KV_LORA_RANK + LANE
    return pl.pallas_call(
        _mla_down_kernel,
        grid=(N_TOK // tm,),
        in_specs=[
            pl.BlockSpec((tm, D_MODEL), lambda i: (i, 0)),
            _mod_spec(layer, tm),
            pl.BlockSpec((1, D_MODEL), lambda i: (0, 0)),
            pl.BlockSpec((D_MODEL, n_down), lambda i: (0, 0)),
            pl.BlockSpec((1, Q_LORA_RANK), lambda i: (0, 0)),
            pl.BlockSpec((1, KV_LORA_RANK), lambda i: (0, 0)),
        ],
        out_specs=[
            pl.BlockSpec((tm, Q_LORA_RANK), lambda i: (i, 0)),
            pl.BlockSpec((tm, KV_LORA_RANK), lambda i: (i, 0)),
            pl.BlockSpec((tm, LANE), lambda i: (i, 0)),
            pl.BlockSpec((tm, KV_LORA_RANK), lambda i: (jnp.minimum(i, n_p - 1), 0)),
            pl.BlockSpec((tm, ROPE_DIM), lambda i: (jnp.minimum(i, n_p - 1), 0)),
        ],
        out_shape=[
            jax.ShapeDtypeStruct((N_TOK, Q_LORA_RANK), BF16),
            jax.ShapeDtypeStruct((N_TOK, KV_LORA_RANK), BF16),
            jax.ShapeDtypeStruct((N_TOK, LANE), F32),
            jax.ShapeDtypeStruct((NP_TOK, KV_LORA_RANK), F32),
            jax.ShapeDtypeStruct((NP_TOK, ROPE_DIM), F32),
        ],
        compiler_params=_cparams("arbitrary"),
        name="mla_down",
    )(x, mods, g, w_down, qg, kvg)


ATTN_BLOCK = 1024
ATTN_TQ = 256
HEAD_GROUP = 2
N_HEAD_GROUPS = N_HEADS // HEAD_GROUP
LOG2E = 1.4426950408889634


def _attn_kernel(*refs, seq, n_cache, rope):
    it = iter(refs)
    cq_ref, ckv_ref, kr_ref = next(it), next(it), next(it)
    if n_cache:
        cckv_ref, ckr_ref = next(it), next(it)
    wuq_ref, wukv_ref, gq_ref, gk_ref = next(it), next(it), next(it), next(it)
    if rope:
        cos_ref, sin_ref = next(it), next(it)
    o_ref, k_scr, v_scr, q_scr = next(it), next(it), next(it), next(it)

    inv_dim = 1.0 / QK_HEAD_DIM
    gq, gk = gq_ref[...], gk_ref[...]
    cq, ckv, kr = cq_ref[...], ckv_ref[...], kr_ref[...]
    tables = (cos_ref[...], sin_ref[...]) if rope else None
    ones_new = jnp.ones((ATTN_BLOCK, LANE), BF16)
    if n_cache:
        cckv, ckr = cckv_ref[...].astype(BF16), ckr_ref[...]
        ones_cache = jnp.ones((n_cache, LANE), BF16)

    def normed(nope, rot2, g, tabs, out_scale):
        ssq = jnp.sum(nope * nope, axis=-1, keepdims=True) + 0.5 * jnp.sum(rot2 * rot2, axis=-1, keepdims=True)
        r = lax.rsqrt(ssq * inv_dim + EPS) * out_scale
        if tabs is None:
            rot = rot2 * g[1:2, :]
        else:
            rot = rot2 * (g[1:2, :] * tabs[0]) + pltpu.roll(rot2, ROPE_DIM, 1) * (g[2:3, :] * tabs[1])
        return jnp.concatenate([(nope * r * g[0:1, :]).astype(BF16), (rot * r).astype(BF16)], axis=1)

    def build(gi, slot):
        for j in range(HEAD_GROUP):
            h = gi * HEAD_GROUP + j
            wukv = wukv_ref[h]
            kv = _dot(ckv, wukv)
            k_scr[slot, j, 0:ATTN_BLOCK, :] = normed(kv[:, :LANE], kr, gk, tables, 1.0)
            v_scr[slot, j, 0:ATTN_BLOCK, :] = jnp.concatenate([kv[:, LANE:].astype(BF16), ones_new], axis=1)
            if n_cache:
                kvc = _dot(cckv, wukv)
                k_scr[slot, j, ATTN_BLOCK:, :] = normed(kvc[:, :LANE], ckr, gk, None, 1.0)
                v_scr[slot, j, ATTN_BLOCK:, :] = jnp.concatenate([kvc[:, LANE:].astype(BF16), ones_cache], axis=1)
            q = _dot(cq, wuq_ref[h])
            q_scr[slot, j] = normed(q[:, :LANE], q[:, LANE:], gq, tables, QK_HEAD_DIM ** -0.5 * LOG2E)

    def attend(gi, slot):
        heads = []
        for j in range(HEAD_GROUP):
            outs = []
            for i in range(ATTN_BLOCK // ATTN_TQ):
                rows = slice(i * ATTN_TQ, (i + 1) * ATTN_TQ)
                keys = slice(None) if seq == ATTN_BLOCK else rows
                s = lax.dot_general(q_scr[slot, j, rows, :], k_scr[slot, j, keys, :], (((1,), (1,)), ((), ())),
                                    preferred_element_type=F32)
                p = jnp.exp2((s - jnp.max(s, axis=-1, keepdims=True)).astype(BF16))
                oe = _dot(p, v_scr[slot, j, keys, :])
                outs.append((oe[:, :LANE] / oe[:, LANE:]).astype(BF16))
            heads.append(jnp.concatenate(outs, axis=0))
        o_ref[gi] = jnp.concatenate(heads, axis=1)

    build(0, 0)

    def two_groups(t, carry):
        g0 = 2 * t
        build(g0 + 1, 1)
        attend(g0, 0)
        build(jnp.minimum(g0 + 2, N_HEAD_GROUPS - 1), 0)
        attend(g0 + 1, 1)
        return carry

    lax.fori_loop(0, N_HEAD_GROUPS // 2, two_groups, 0)


def _attention(cq, ckv, kr2, cache, wuq, wukv, gq, gk, tables, *, tok0, n_tok, seq):
    assert seq in (ATTN_BLOCK, ATTN_TQ) and tok0 % ATTN_BLOCK == 0 and n_tok % ATTN_BLOCK == 0
    b0 = tok0 // ATTN_BLOCK
    n_cache = 0 if cache is None else cache[0].shape[1]
    assert n_cache == 0 or seq == ATTN_BLOCK
    rope = tables is not None
    row = lambda b: (b0 + b, 0)
    const2 = lambda b: (0, 0)
    const3 = lambda b: (0, 0, 0)
    in_specs = [
        pl.BlockSpec((ATTN_BLOCK, Q_LORA_RANK), row),
        pl.BlockSpec((ATTN_BLOCK, KV_LORA_RANK), row),
        pl.BlockSpec((ATTN_BLOCK, LANE), row),
    ]
    args = [cq, ckv, kr2]
    if n_cache:
        in_specs += [pl.BlockSpec((None, n_cache, KV_LORA_RANK), lambda b: (b, 0, 0)),
                     pl.BlockSpec((None, n_cache, LANE), lambda b: (b, 0, 0))]
        args += list(cache)
    in_specs += [
        pl.BlockSpec((N_HEADS, Q_LORA_RANK, HEAD_PAD), const3),
        pl.BlockSpec((N_HEADS, KV_LORA_RANK, HEAD_PAD), const3),
        pl.BlockSpec((SUBLANE, LANE), const2),
        pl.BlockSpec((SUBLANE, LANE), const2),
    ]
    args += [wuq, wukv, gq, gk]
    if rope:
        in_specs += [pl.BlockSpec((ATTN_BLOCK, LANE), const2), pl.BlockSpec((ATTN_BLOCK, LANE), const2)]
        args += list(tables)
    return pl.pallas_call(
        functools.partial(_attn_kernel, seq=seq, n_cache=n_cache, rope=rope),
        grid=(n_tok // ATTN_BLOCK,),
        in_specs=in_specs,
        out_specs=pl.BlockSpec((N_HEAD_GROUPS, ATTN_BLOCK, HEAD_GROUP * V_HEAD_DIM), lambda b: (0, b, 0)),
        out_shape=jax.ShapeDtypeStruct((N_HEAD_GROUPS, n_tok, HEAD_GROUP * V_HEAD_DIM), BF16),
        scratch_shapes=[
            pltpu.VMEM((2, HEAD_GROUP, ATTN_BLOCK + n_cache, HEAD_PAD), BF16),
            pltpu.VMEM((2, HEAD_GROUP, ATTN_BLOCK + n_cache, HEAD_PAD), BF16),
            pltpu.VMEM((2, HEAD_GROUP, ATTN_BLOCK, HEAD_PAD), BF16),
        ],
        compiler_params=_cparams("arbitrary"),
        name="attn_rope" if rope else "attn",
    )(*args)


ROUTE_TM = 4 * TOK_CHUNK
ROUTE_E1, ROUTE_E2, ROUTE_RANK1, ROUTE_RANK2 = N_EXPERTS, N_EXPERTS + 1, N_EXPERTS + 2, N_EXPERTS + 3
ROUTE_ROWS = 16


def _attn_out_kernel(op_ref, os_ref, x_ref, mod_ref, g_ref, wo_ref, wr_ref,
                     x3_ref, h_ref, route_ref, route_t_ref, cstart_ref, total_ref, carry_ref):
    i = pl.program_id(0)
    mod = mod_ref[...]
    is_prompt = i * x_ref.shape[0] < NP_TOK

    @pl.when(i == 0)
    def _():
        carry_ref[...] = jnp.zeros_like(carry_ref)

    wh, wl = _split_bf16(wr_ref[...])
    lane = lax.broadcasted_iota(I32, (TOK_CHUNK, LANE), 1)
    lanef = lane.astype(F32)
    neg = jnp.float32(-jnp.inf)
    r_id = lax.broadcasted_iota(I32, (TOK_CHUNK, TOK_CHUNK), 0)
    c_id = lax.broadcasted_iota(I32, (TOK_CHUNK, TOK_CHUNK), 1)
    tri = jnp.where(c_id < r_id, 1.0, 0.0).astype(BF16)
    seen = carry_ref[0:1, :]

    for k in range(x_ref.shape[0] // TOK_CHUNK):
        rows = slice(k * TOK_CHUNK, (k + 1) * TOK_CHUNK)
        att = _dot(jnp.where(is_prompt, op_ref[0, rows, :], os_ref[0, rows, :]), wo_ref[0])
        for gi in range(1, N_HEAD_GROUPS):
            att += _dot(jnp.where(is_prompt, op_ref[gi, rows, :], os_ref[gi, rows, :]), wo_ref[gi])
        x3 = x_ref[rows, :] + _mod_part(mod, 2) * att
        x3_ref[rows, :] = x3
        h = _rms(x3, g_ref[...]) * (1.0 + _mod_part(mod, 4)) + _mod_part(mod, 3)
        hb = h.astype(BF16)
        h_ref[rows, :] = hb
        hl = (h - hb.astype(F32)).astype(BF16)
        logits = _dot(hb, wh) + (_dot(hl, wh) + _dot(hb, wl))
        logits = jnp.where(lane < N_EXPERTS, logits, neg)
        v1 = jnp.max(logits, axis=-1, keepdims=True)
        i1 = jnp.min(jnp.where(logits == v1, lanef, float(LANE)), axis=-1, keepdims=True)
        rest = jnp.where(lanef == i1, neg, logits)
        v2 = jnp.max(rest, axis=-1, keepdims=True)
        i2 = jnp.min(jnp.where(rest == v2, lanef, float(LANE)), axis=-1, keepdims=True)
        e2 = jnp.exp(v2 - v1)
        w1 = 1.0 / (1.0 + e2)
        hot1, hot2 = lanef == i1, lanef == i2
        gates = jnp.where(hot1, w1, 0.0) + jnp.where(hot2, e2 * w1, 0.0)

        hot = jnp.where(hot1 | hot2, 1.0, 0.0)
        before = _dot(tri, hot.astype(BF16)) + seen
        rank1 = jnp.sum(jnp.where(hot1, before, 0.0), axis=-1, keepdims=True)
        rank2 = jnp.sum(jnp.where(hot2, before, 0.0), axis=-1, keepdims=True)
        route = jnp.where(lane == ROUTE_E1, i1, jnp.where(lane == ROUTE_E2, i2, jnp.where(
            lane == ROUTE_RANK1, rank1, jnp.where(lane == ROUTE_RANK2, rank2, gates))))
        route_ref[rows, :] = route
        route_t_ref[:, rows] = route.T[:ROUTE_ROWS, :]
        cstart_ref[k] = jnp.broadcast_to(seen, (SUBLANE, LANE))
        seen = before[TOK_CHUNK - 1:TOK_CHUNK, :] + hot[TOK_CHUNK - 1:TOK_CHUNK, :]

    carry_ref[...] = jnp.broadcast_to(seen, (SUBLANE, LANE))
    total_ref[...] = jnp.broadcast_to(seen, (SUBLANE, LANE))


def _attn_out(o_p, o_s, x, mods, g, wo, wr, layer):
    tm = ROUTE_TM
    per = tm // TOK_CHUNK
    n_p = NP_TOK // tm
    o_block = (N_HEAD_GROUPS, tm, HEAD_GROUP * V_HEAD_DIM)
    return pl.pallas_call(
        _attn_out_kernel,
        grid=(N_TOK // tm,),
        in_specs=[
            pl.BlockSpec(o_block, lambda i: (0, jnp.minimum(i, n_p - 1), 0)),
            pl.BlockSpec(o_block, lambda i: (0, jnp.maximum(i - n_p, 0), 0)),
            pl.BlockSpec((tm, D_MODEL), lambda i: (i, 0)),
            _mod_spec(layer, tm),
            pl.BlockSpec((1, D_MODEL), lambda i: (0, 0)),
            pl.BlockSpec((N_HEAD_GROUPS, HEAD_GROUP * V_HEAD_DIM, D_MODEL), lambda i: (0, 0, 0)),
            pl.BlockSpec((D_MODEL, LANE), lambda i: (0, 0)),
        ],
        out_specs=[
            pl.BlockSpec((tm, D_MODEL), lambda i: (i, 0)),
            pl.BlockSpec((tm, D_MODEL), lambda i: (i, 0)),
            pl.BlockSpec((tm, LANE), lambda i: (i, 0)),
            pl.BlockSpec((ROUTE_ROWS, tm), lambda i: (0, i)),
            pl.BlockSpec((per, SUBLANE, LANE), lambda i: (i, 0, 0)),
            pl.BlockSpec((SUBLANE, LANE), lambda i: (0, 0)),
        ],
        out_shape=[
            jax.ShapeDtypeStruct((N_TOK, D_MODEL), F32),
            jax.ShapeDtypeStruct((N_TOK, D_MODEL), BF16),
            jax.ShapeDtypeStruct((N_TOK, LANE), F32),
            jax.ShapeDtypeStruct((ROUTE_ROWS, N_TOK), F32),
            jax.ShapeDtypeStruct((N_CHUNKS, SUBLANE, LANE), F32),
            jax.ShapeDtypeStruct((SUBLANE, LANE), F32),
        ],
        scratch_shapes=[pltpu.VMEM((SUBLANE, LANE), F32)],
        compiler_params=_cparams("arbitrary"),
        name="attn_out",
    )(o_p, o_s, x, mods, g, wo, wr)


def _routing_tables(route_t, cstart, total):
    counts = total[0, :N_EXPERTS].astype(I32)
    padded = (counts + SLOT_TILE - 1) // SLOT_TILE * SLOT_TILE
    ends = jnp.cumsum(padded)
    offs = ends - padded
    expert_ids = jnp.arange(N_EXPERTS, dtype=I32)[:, None]

    def region_start(e_row):
        return jnp.sum(jnp.where(e_row[None, :].astype(I32) == expert_ids, offs[:, None], 0), axis=0)

    slot1 = region_start(route_t[ROUTE_E1]) + route_t[ROUTE_RANK1].astype(I32)
    slot2 = region_start(route_t[ROUTE_E2]) + route_t[ROUTE_RANK2].astype(I32)

    n_active = ends[-1] // SLOT_TILE
    tile_start = jnp.arange(N_SLOT_TILES, dtype=I32) * SLOT_TILE
    tile_expert = jnp.sum(tile_start[:, None] >= ends[None, :], axis=1).astype(I32)
    last_expert = jnp.sum((n_active - 1) * SLOT_TILE >= ends).astype(I32)
    tile_active = tile_start < ends[-1]
    tile_expert = jnp.where(tile_active, tile_expert, last_expert)
    prev_expert = jnp.concatenate([jnp.full((1,), -1, I32), tile_expert[:-1]])
    tile_first = (tile_active & (tile_expert != prev_expert)).astype(I32)
    weight_slot = ((jnp.cumsum(tile_first) - 1) % 2).astype(I32)
    later = (expert_ids.T > expert_ids) & (counts > 0)[None, :]
    next_of = jnp.min(jnp.where(later, expert_ids.T, N_EXPERTS), axis=1)
    next_of = jnp.where(next_of == N_EXPERTS, -1, next_of).astype(I32)
    next_expert = jnp.sum(jnp.where(tile_expert[:, None] == expert_ids.T, next_of[None, :], 0), axis=1).astype(I32)

    cc = jnp.concatenate([cstart[:, 0, :N_EXPERTS], total[0:1, :N_EXPERTS]]).astype(I32)

    g_start = jnp.arange(N_SLOTS // GATHER_TILE, dtype=I32) * GATHER_TILE
    g_expert = jnp.minimum(jnp.sum(g_start[:, None] >= ends[None, :], axis=1), N_EXPERTS - 1).astype(I32)
    g_hot = (g_expert[None, :] == expert_ids).astype(I32)
    rank0 = g_start - jnp.sum(g_hot * offs[:, None], axis=0)
    cc_tile = jnp.sum(cc[:, :, None] * g_hot[None, :, :], axis=1)
    c_lo = jnp.sum(cc_tile[1:] <= rank0[None, :], axis=0).astype(I32)
    rank_end = jnp.minimum(rank0 + GATHER_TILE, jnp.sum(g_hot * counts[:, None], axis=0))
    c_hi = jnp.sum(cc_tile[:-1] < rank_end[None, :], axis=0).astype(I32) - 1
    idle = (g_start >= ends[-1]) | (c_hi < c_lo)
    c_lo = jnp.where(idle, 1, c_lo)
    c_hi = jnp.where(idle, 0, c_hi)

    lo = offs[None, :] + cc[:-1]
    hi = offs[None, :] + cc[1:]
    first, last = lo // SLOT_CHUNK, (hi - 1) // SLOT_CHUNK
    ids = jnp.concatenate([first, last], axis=1)
    valid = jnp.concatenate([hi > lo, (hi > lo) & (last != first)], axis=1)
    experts = jnp.tile(jnp.arange(N_EXPERTS, dtype=I32), (N_CHUNKS, 2))
    dest = jnp.cumsum(valid, axis=1) - 1
    place = (valid[:, :, None] & (dest[:, :, None] == jnp.arange(MAX_PAIRS, dtype=I32)[None, None, :])).astype(I32)
    ids = jnp.sum(ids[:, :, None] * place, axis=1).astype(I32)
    experts = jnp.sum(experts[:, :, None] * place, axis=1).astype(I32)
    n_pairs = jnp.sum(valid, axis=1).astype(I32)
    unused = jnp.arange(MAX_PAIRS, dtype=I32)[None, :] >= n_pairs[:, None]
    ids = jnp.where(unused, ids[:, 0:1], ids)
    experts = jnp.where(unused, -1, experts)
    n_pairs = n_pairs + n_pairs % 2

    pad = jnp.zeros((N_CHUNKS, SUBLANE - TOP_K, TOK_CHUNK), I32)
    slots_lane = jnp.concatenate([slot1.reshape(N_CHUNKS, 1, TOK_CHUNK), slot2.reshape(N_CHUNKS, 1, TOK_CHUNK), pad],
                                 axis=1)
    offs_row = jnp.concatenate([offs.astype(F32), jnp.zeros((LANE - N_EXPERTS,), F32)]).reshape(1, LANE)
    return dict(tile_expert=tile_expert, n_active=n_active.reshape(1).astype(I32), tile_first=tile_first,
                weight_slot=weight_slot, next_expert=next_expert, c_lo=c_lo, c_hi=c_hi,
                ids=ids.reshape(-1), experts=experts.reshape(-1), n_pairs=n_pairs,
                slots_lane=slots_lane, offs_row=offs_row)


GATHER_TILE = 256


def _gather_kernel(clo_ref, chi_ref, slots_ref, h_ref, o_ref, acc_ref):
    g = pl.program_id(0)
    slot_id = g * GATHER_TILE + lax.broadcasted_iota(I32, (GATHER_TILE, TOK_CHUNK), 0)
    acc_ref[...] = jnp.zeros_like(acc_ref)

    c_lo, c_hi = clo_ref[g], chi_ref[g]

    def one_hot(c, value):
        sl = slots_ref[c]
        hit = (sl[0:1, :] == slot_id) | (sl[1:2, :] == slot_id)
        return jnp.where(hit, value, 0.0).astype(BF16)

    def rows(c):
        return h_ref[pl.ds(pl.multiple_of(c * TOK_CHUNK, TOK_CHUNK), TOK_CHUNK), :]

    def body(t, carry):
        c0 = c_lo + 2 * t
        c1 = jnp.minimum(c0 + 1, c_hi)
        live1 = jnp.where(c0 + 1 <= c_hi, 1.0, 0.0)
        acc_ref[...] += _dot(one_hot(c0, 1.0), rows(c0)) + _dot(one_hot(c1, live1), rows(c1))
        return carry

    lax.fori_loop(0, (c_hi - c_lo + 2) // 2, body, 0)
    o_ref[...] = acc_ref[...].astype(BF16)


def _gather(h, rt):
    return pl.pallas_call(
        _gather_kernel,
        grid_spec=pltpu.PrefetchScalarGridSpec(
            num_scalar_prefetch=2,
            grid=(N_SLOTS // GATHER_TILE,),
            in_specs=[
                pl.BlockSpec((N_CHUNKS, SUBLANE, TOK_CHUNK), lambda g, *_: (0, 0, 0)),
                pl.BlockSpec((N_TOK, D_MODEL), lambda g, *_: (0, 0), pipeline_mode=pl.Buffered(1)),
            ],
            out_specs=pl.BlockSpec((GATHER_TILE, D_MODEL), lambda g, *_: (g, 0)),
            scratch_shapes=[pltpu.VMEM((GATHER_TILE, D_MODEL), F32)],
        ),
        out_shape=jax.ShapeDtypeStruct((N_SLOTS, D_MODEL), BF16),
        compiler_params=_cparams("arbitrary"),
        name="moe_gather",
    )(rt["c_lo"], rt["c_hi"], rt["slots_lane"], h)


def _experts_kernel(te_ref, na_ref, first_ref, nxt_ref, ws_ref, h_ref, w1_hbm, w3_hbm, w2_hbm, o_ref,
                    w1_buf, w3_buf, w2_buf, sem):
    g = pl.program_id(0)
    slot = ws_ref[g]

    def weight_copies(e, s):
        return [pltpu.make_async_copy(w1_hbm.at[e], w1_buf.at[s], sem.at[s, 0]),
                pltpu.make_async_copy(w3_hbm.at[e], w3_buf.at[s], sem.at[s, 1]),
                pltpu.make_async_copy(w2_hbm.at[e], w2_buf.at[s], sem.at[s, 2])]

    @pl.when(g == 0)
    def _():
        for cp in weight_copies(te_ref[0], 0):
            cp.start()

    @pl.when(first_ref[g] == 1)
    def _():
        for cp in weight_copies(te_ref[g], slot):
            cp.wait()

        @pl.when(nxt_ref[g] >= 0)
        def _():
            for cp in weight_copies(nxt_ref[g], 1 - slot):
                cp.start()

    @pl.when(g < na_ref[0])
    def _():
        h = h_ref[...].astype(w1_buf.dtype)
        t = _silu(_dot(h, w1_buf[slot])) * _dot(h, w3_buf[slot])
        o_ref[...] = _dot(t.astype(w2_buf.dtype), w2_buf[slot]).astype(BF16)

    @pl.when(g >= na_ref[0])
    def _():
        o_ref[...] = jnp.zeros_like(o_ref)


def _experts(hs, rt, w1, w3, w2):
    tile = lambda g, te, na, *_: (jnp.minimum(g, na[0] - 1), 0)
    return pl.pallas_call(
        _experts_kernel,
        grid_spec=pltpu.PrefetchScalarGridSpec(
            num_scalar_prefetch=5,
            grid=(N_SLOT_TILES,),
            in_specs=[
                pl.BlockSpec((SLOT_TILE, D_MODEL), tile),
                pl.BlockSpec(memory_space=pl.ANY),
                pl.BlockSpec(memory_space=pl.ANY),
                pl.BlockSpec(memory_space=pl.ANY),
            ],
            out_specs=pl.BlockSpec((SLOT_TILE, D_MODEL), lambda g, *_: (g, 0)),
            scratch_shapes=[
                pltpu.VMEM((2, D_MODEL, D_FF_EXPERT), w1.dtype),
                pltpu.VMEM((2, D_MODEL, D_FF_EXPERT), w3.dtype),
                pltpu.VMEM((2, D_FF_EXPERT, D_MODEL), w2.dtype),
                pltpu.SemaphoreType.DMA((2, 3)),
            ],
        ),
        out_shape=jax.ShapeDtypeStruct((N_SLOTS, D_MODEL), BF16),
        compiler_params=_cparams("arbitrary"),
        name="moe_experts",
    )(rt["tile_expert"], rt["n_active"], rt["tile_first"], rt["next_expert"], rt["weight_slot"], hs, w1, w3, w2)


def _combine_kernel(np_ref, ids_ref, ex_ref, route_ref, offs_ref, x_ref, mod_ref, y_hbm,
                    op_ref, os_ref, buf_ref, acc_ref, sem):
    c = pl.program_id(0)
    cur = c % 2

    def chunk_copy(step, j, half):
        chunk = ids_ref[step * MAX_PAIRS + j]
        src = y_hbm.at[pl.ds(pl.multiple_of(chunk * SLOT_CHUNK, SLOT_CHUNK), SLOT_CHUNK)]
        return pltpu.make_async_copy(src, buf_ref.at[half, j], sem.at[half, j])

    def start_all(step, half):
        def go(j, carry):
            chunk_copy(step, j, half).start()
            return carry

        lax.fori_loop(0, np_ref[step], go, 0)

    @pl.when(c == 0)
    def _():
        start_all(0, 0)

    @pl.when(c + 1 < pl.num_programs(0))
    def _():
        start_all(c + 1, 1 - cur)

    route = route_ref[...]
    lanef = lax.broadcasted_iota(I32, route.shape, 1).astype(F32)
    offs = offs_ref[...]

    def slot_of(e_lane, rank_lane):
        start = jnp.sum(jnp.where(lanef == route[:, e_lane:e_lane + 1], offs, 0.0), axis=-1, keepdims=True)
        return start + route[:, rank_lane:rank_lane + 1]

    col = lax.broadcasted_iota(I32, (TOK_CHUNK, SLOT_CHUNK), 1).astype(F32)
    s1 = slot_of(ROUTE_E1, ROUTE_RANK1) - col
    s2 = slot_of(ROUTE_E2, ROUTE_RANK2) - col
    acc_ref[...] = jnp.zeros_like(acc_ref)

    def part(j):
        pair = c * MAX_PAIRS + j
        base = (ids_ref[pair] * SLOT_CHUNK).astype(F32)
        hit = (s1 == base) | (s2 == base)
        rows = _dot(jnp.where(hit, 1.0, 0.0).astype(BF16), buf_ref[cur, j])
        return _lane_pick(route, ex_ref[pair]) * rows

    def body(t, carry):
        chunk_copy(c, 2 * t, cur).wait()
        chunk_copy(c, 2 * t + 1, cur).wait()
        acc_ref[...] += part(2 * t) + part(2 * t + 1)
        return carry

    lax.fori_loop(0, np_ref[c] // 2, body, 0)
    res = x_ref[...] + _mod_part(mod_ref[...], 5) * acc_ref[...]

    @pl.when(c * TOK_CHUNK < NP_TOK)
    def _():
        op_ref[...] = res

    @pl.when(c * TOK_CHUNK >= NP_TOK)
    def _():
        os_ref[...] = res


def _combine(ys, route, x, mods, rt, layer):
    tok = lambda c, *_: (c, 0)
    return pl.pallas_call(
        _combine_kernel,
        grid_spec=pltpu.PrefetchScalarGridSpec(
            num_scalar_prefetch=3,
            grid=(N_CHUNKS,),
            in_specs=[
                pl.BlockSpec((TOK_CHUNK, LANE), tok),
                pl.BlockSpec((1, LANE), lambda c, *_: (0, 0)),
                pl.BlockSpec((TOK_CHUNK, D_MODEL), tok),
                _mod_spec(layer, TOK_CHUNK),
                pl.BlockSpec(memory_space=pl.ANY),
            ],
            out_specs=_split_specs(TOK_CHUNK),
            scratch_shapes=[
                pltpu.VMEM((2, MAX_PAIRS, SLOT_CHUNK, D_MODEL), BF16),
                pltpu.VMEM((TOK_CHUNK, D_MODEL), F32),
                pltpu.SemaphoreType.DMA((2, MAX_PAIRS)),
            ],
        ),
        out_shape=[jax.ShapeDtypeStruct((NP_TOK, D_MODEL), F32), jax.ShapeDtypeStruct((NS_TOK, D_MODEL), F32)],
        compiler_params=_cparams("arbitrary"),
        name="moe_combine",
    )(rt["n_pairs"], rt["ids"], rt["experts"], route, rt["offs_row"], x, mods, ys)


def _rope_partner(t):
    half = AXIS_ROPE_DIM // 2
    s = t.shape[:-1]
    return t.reshape(s + (2, 2, half))[..., ::-1, :].reshape(s + (ROPE_DIM,))


def _rope_tables(n_tokens):
    rows = n_tokens // GRID_W
    row = np.repeat(np.arange(rows), GRID_W).astype(np.float32)
    col = np.tile(np.arange(GRID_W), rows).astype(np.float32)
    inv = (ROPE_BASE ** (-np.arange(0, AXIS_ROPE_DIM, 2, dtype=np.float32) / AXIS_ROPE_DIM)).astype(np.float32)
    ar, ac = row[:, None] * inv, col[:, None] * inv
    cos = np.concatenate([np.cos(ar), np.cos(ar), np.cos(ac), np.cos(ac)], axis=-1)
    sin = np.concatenate([-np.sin(ar), np.sin(ar), -np.sin(ac), np.sin(ac)], axis=-1)
    zeros = np.zeros_like(cos)
    return (jnp.asarray(np.concatenate([cos, zeros], axis=-1), F32),
            jnp.asarray(np.concatenate([sin, zeros], axis=-1), F32))


def _qk_gain_rows(g):
    z = jnp.zeros((ROPE_DIM,), F32)
    rows = jnp.stack([g[:QK_NOPE_DIM],
                      jnp.concatenate([g[QK_NOPE_DIM:], z]),
                      jnp.concatenate([_rope_partner(g[QK_NOPE_DIM:]), z])])
    return jnp.concatenate([rows, jnp.zeros((SUBLANE - 3, LANE), F32)])


def kernel(x_prompt, x_sample, c, cache_ckv, cache_krope, c_ctx, ada_w, ada_b, norm1_g, norm2_g, conv_pw1, conv_dw, conv_dw_b, conv_ln_g, conv_ln_b, conv_pw2, ffn_w1, ffn_w3, ffn_w2, mla_wdq, mla_q_norm_g, mla_wuq, mla_wdkv, mla_kv_norm_g, mla_wukv, mla_q_qk_g, mla_k_qk_g, mla_wo, moe_router, moe_w1, moe_w3, moe_w2):
    xp, xs = x_prompt.reshape(NP_TOK, D_MODEL), x_sample.reshape(NS_TOK, D_MODEL)
    cond16 = jnp.concatenate([c_ctx[None, :], c, jnp.zeros((MOD_ROWS - 1 - DEC_BATCH, D_MODEL), F32)])
    mods = _adaln(cond16, ada_w, ada_b).reshape(2, MOD_ROWS, 1, N_MOD * D_MODEL)
    vec = lambda a: a.reshape(1, -1)

    u = _glu(xp, xs, mods, vec(norm1_g[0]), conv_pw1[0].astype(BF16), 0)
    dw = jnp.concatenate([conv_dw[0], jnp.zeros((1, D_MODEL), F32)])
    x = _conv(u, xp, xs, mods, dw, vec(conv_dw_b[0]), vec(conv_ln_g[0]), vec(conv_ln_b[0]),
              conv_pw2[0].astype(BF16), 0)
    x = _ffn(x, mods, vec(norm2_g[0]), ffn_w1[0].astype(BF16), ffn_w3[0].astype(BF16),
             ffn_w2[0].astype(BF16), 0)

    wdkv = mla_wdkv[0]
    w_down = jnp.concatenate([mla_wdq[0], wdkv, _rope_partner(wdkv[:, KV_LORA_RANK:])], axis=1).astype(BF16)
    cq, ckv_b, kr2, new_ckv, new_krope = _mla_down(x, mods, vec(norm1_g[1]), w_down, vec(mla_q_norm_g[0]),
                                                   vec(mla_kv_norm_g[0]), 1)

    wuq = mla_wuq[0].reshape(Q_LORA_RANK, N_HEADS, QK_HEAD_DIM)
    wuq = jnp.concatenate([wuq, _rope_partner(wuq[..., QK_NOPE_DIM:])], axis=-1)
    wuq = wuq.transpose(1, 0, 2).astype(BF16)
    wukv = mla_wukv[0].reshape(KV_LORA_RANK, N_HEADS, HEAD_PAD).transpose(1, 0, 2).astype(BF16)
    gq, gk = _qk_gain_rows(mla_q_qk_g[0]), _qk_gain_rows(mla_k_qk_g[0])
    ckr = cache_krope[:, 0]
    cache = (cache_ckv[:, 0], jnp.concatenate([ckr, _rope_partner(ckr)], axis=-1))
    o_p = _attention(cq, ckv_b, kr2, None, wuq, wukv, gq, gk, None, tok0=0, n_tok=NP_TOK, seq=SEQ)
    o_s = _attention(cq, ckv_b, kr2, cache, wuq, wukv, gq, gk, _rope_tables(DEC_SEQ),
                     tok0=NP_TOK, n_tok=NS_TOK, seq=DEC_SEQ)

    wr = jnp.concatenate([moe_router[0], jnp.zeros((D_MODEL, LANE - N_EXPERTS), F32)], axis=1)
    wo = mla_wo[0].astype(BF16).reshape(N_HEAD_GROUPS, HEAD_GROUP * V_HEAD_DIM, D_MODEL)
    x, h, route, route_t, cstart, total = _attn_out(o_p, o_s, x, mods, vec(norm2_g[1]), wo, wr, 1)
    rt = _routing_tables(route_t, cstart, total)
    hs = _gather(h, rt)
    ys = _experts(hs, rt, moe_w1[0], moe_w3[0], moe_w2[0])
    yp, ysamp = _combine(ys, route, x, mods, rt, 1)

    return (yp.reshape(BATCH, SEQ, D_MODEL), ysamp.reshape(DEC_BATCH, DEC_SEQ, D_MODEL),
            new_ckv.reshape(BATCH, 1, SEQ, KV_LORA_RANK), new_krope.reshape(BATCH, 1, SEQ, ROPE_DIM))
```

```python
import functools

import jax
import jax.numpy as jnp
import numpy as np
from jax import lax
from jax.experimental import pallas as pl
from jax.experimental.pallas import tpu as pltpu

D_MODEL = 1024
BATCH = 32
SEQ = 256
DEC_BATCH = 8
DEC_SEQ = 1024
PAST_LEN = 512
GRID_W = 64
N_MOD = 6
CONV_WIDTH = 31
CONV_PAD = CONV_WIDTH // 2
N_HEADS = 16
QK_NOPE_DIM = 128
ROPE_DIM = 64
QK_HEAD_DIM = QK_NOPE_DIM + ROPE_DIM
V_HEAD_DIM = 128
Q_LORA_RANK = 512
KV_LORA_RANK = 256
AXIS_ROPE_DIM = ROPE_DIM // 2
ROPE_BASE = 10000.0
D_FF = 2816
N_EXPERTS = 8
TOP_K = 2
D_FF_EXPERT = 1536
EPS = 1e-6
F32 = jnp.float32
BF16 = jnp.bfloat16
I32 = jnp.int32

NP_TOK = BATCH * SEQ
NS_TOK = DEC_BATCH * DEC_SEQ
N_TOK = NP_TOK + NS_TOK
MOD_ROWS = 16
LANE = 128
SUBLANE = 8
HEAD_PAD = 2 * LANE
VMEM_LIMIT = 56 * 1024 * 1024

TOK_CHUNK = 256
N_CHUNKS = N_TOK // TOK_CHUNK
SLOT_TILE = 512
N_SLOT_TILES = (TOP_K * N_TOK + N_EXPERTS * (SLOT_TILE - 1)) // SLOT_TILE
N_SLOTS = N_SLOT_TILES * SLOT_TILE
SLOT_CHUNK = 256
MAX_PAIRS = 2 * N_EXPERTS


def _cparams(*sem):
    return pltpu.CompilerParams(dimension_semantics=sem, vmem_limit_bytes=VMEM_LIMIT)


def _mod_row(tile, tm):
    start = tile * tm
    return jnp.where(start < NP_TOK, 0, 1 + (start - NP_TOK) // DEC_SEQ)


def _mod_spec(layer, tm):
    return pl.BlockSpec((None, None, 1, N_MOD * D_MODEL),
                        lambda i, *_: (layer, _mod_row(i, tm), 0, 0))


def _split_specs(tm):
    n_p = NP_TOK // tm
    return [pl.BlockSpec((tm, D_MODEL), lambda i, *_: (jnp.minimum(i, n_p - 1), 0)),
            pl.BlockSpec((tm, D_MODEL), lambda i, *_: (jnp.maximum(i - n_p, 0), 0))]


def _pick_tokens(tm, xp_ref, xs_ref):
    return jnp.where(pl.program_id(0) * tm < NP_TOK, xp_ref[...], xs_ref[...])


def _mod_part(mod, k):
    return mod[:, k * D_MODEL:(k + 1) * D_MODEL]


def _rms(x, g):
    return x * lax.rsqrt(jnp.mean(x * x, axis=-1, keepdims=True) + EPS) * g


def _silu(x):
    return x * jax.nn.sigmoid(x)


def _split_bf16(x):
    hi = x.astype(BF16)
    lo = (x - hi.astype(F32)).astype(BF16)
    return hi, lo


def _dot(a, b):
    return jnp.dot(a, b, preferred_element_type=F32)


def _dot3(a, b):
    ah, al = _split_bf16(a)
    bh, bl = _split_bf16(b)
    return _dot(ah, bh) + (_dot(al, bh) + _dot(ah, bl))


def _lane_pick(x, idx):
    lane = lax.broadcasted_iota(I32, x.shape, 1)
    return jnp.sum(jnp.where(lane == idx, x, 0.0), axis=-1, keepdims=True)


def _adaln_kernel(cond_ref, w_ref, b_ref, o_ref):
    o_ref[...] = _dot3(_silu(cond_ref[...]), w_ref[...]) + b_ref[...]


def _adaln(cond16, ada_w, ada_b):
    depth = ada_w.shape[0]
    tn = 1536
    return pl.pallas_call(
        _adaln_kernel,
        grid=(depth, N_MOD * D_MODEL // tn),
        in_specs=[
            pl.BlockSpec((MOD_ROWS, D_MODEL), lambda l, j: (0, 0)),
            pl.BlockSpec((None, D_MODEL, tn), lambda l, j: (l, 0, j)),
            pl.BlockSpec((None, 1, tn), lambda l, j: (l, 0, j)),
        ],
        out_specs=pl.BlockSpec((None, MOD_ROWS, tn), lambda l, j: (l, 0, j)),
        out_shape=jax.ShapeDtypeStruct((depth, MOD_ROWS, N_MOD * D_MODEL), F32),
        compiler_params=_cparams("arbitrary", "arbitrary"),
        name="adaln",
    )(cond16, ada_w, ada_b.reshape(depth, 1, N_MOD * D_MODEL))


def _glu_kernel(xp_ref, xs_ref, mod_ref, g_ref, w_ref, u_ref):
    mod = mod_ref[...]
    x = _pick_tokens(u_ref.shape[0], xp_ref, xs_ref)
    h = _rms(x, g_ref[...]) * (1.0 + _mod_part(mod, 1)) + _mod_part(mod, 0)
    ag = _dot(h.astype(BF16), w_ref[...])
    u_ref[...] = ag[:, :D_MODEL] * jax.nn.sigmoid(ag[:, D_MODEL:])


def _glu(xp, xs, mods, g, pw1, layer):
    tm = 1024
    return pl.pallas_call(
        _glu_kernel,
        grid=(N_TOK // tm,),
        in_specs=_split_specs(tm) + [
            _mod_spec(layer, tm),
            pl.BlockSpec((1, D_MODEL), lambda i: (0, 0)),
            pl.BlockSpec((D_MODEL, 2 * D_MODEL), lambda i: (0, 0)),
        ],
        out_specs=pl.BlockSpec((tm, D_MODEL), lambda i: (i, 0)),
        out_shape=jax.ShapeDtypeStruct((N_TOK, D_MODEL), F32),
        compiler_params=_cparams("arbitrary"),
        name="glu",
    )(xp, xs, mods, g, pw1)


CONV_CHUNK = 256
CONV_HALO = 16
CONV_ROWS = 64
CONV_SHIFT_ROWS = CONV_CHUNK + (CONV_HALO - CONV_PAD + CONV_WIDTH - 1) // SUBLANE * SUBLANE


def _conv_kernel(uc_ref, up_ref, un_ref, dw_ref, dwb_ref, lng_ref, lnb_ref, w_ref, xp_ref, xs_ref, mod_ref,
                 o_ref, pad_ref, shift_ref, conv_ref):
    i = pl.program_id(0)
    start = i * CONV_CHUNK
    seq_len = jnp.where(start < NP_TOK, SEQ, DEC_SEQ)
    off = jnp.where(start < NP_TOK, start, start - NP_TOK) % seq_len
    prev_ok = off > 0
    next_ok = off + CONV_CHUNK < seq_len
    pad_ref[0:CONV_HALO, :] = jnp.where(prev_ok, up_ref[...], 0.0)
    pad_ref[CONV_HALO:CONV_HALO + CONV_CHUNK, :] = uc_ref[...]
    pad_ref[CONV_HALO + CONV_CHUNK:, :] = jnp.where(next_ok, un_ref[...], 0.0)

    base = CONV_HALO - CONV_PAD
    for b in range(1, SUBLANE):
        shift_ref[b - 1] = pad_ref[b:b + CONV_SHIFT_ROWS, :]
    for c in range(D_MODEL // LANE):
        cs = slice(c * LANE, (c + 1) * LANE)
        wcol = dw_ref[:, cs]
        bias = dwb_ref[:, cs]
        for r in range(CONV_CHUNK // CONV_ROWS):
            acc = jnp.broadcast_to(bias, (CONV_ROWS, LANE))
            for k in range(CONV_WIDTH):
                a, b = divmod(base + k, SUBLANE)
                lo = r * CONV_ROWS + SUBLANE * a
                src = pad_ref if b == 0 else shift_ref.at[b - 1]
                acc = acc + wcol[k:k + 1, :] * src[lo:lo + CONV_ROWS, cs]
            conv_ref[r * CONV_ROWS:(r + 1) * CONV_ROWS, cs] = acc

    t = conv_ref[...]
    mu = jnp.mean(t, axis=-1, keepdims=True)
    tc = t - mu
    y = tc * lax.rsqrt(jnp.mean(tc * tc, axis=-1, keepdims=True) + EPS) * lng_ref[...] + lnb_ref[...]
    res = _dot(_silu(y).astype(BF16), w_ref[...])
    o_ref[...] = _pick_tokens(CONV_CHUNK, xp_ref, xs_ref) + _mod_part(mod_ref[...], 2) * res


def _conv(u, xp, xs, mods, dw, dwb, lng, lnb, pw2, layer):
    n_chunks = N_TOK // CONV_CHUNK
    halo_per_chunk = CONV_CHUNK // CONV_HALO
    n_halo = N_TOK // CONV_HALO
    row = lambda i: (i, 0)
    const = lambda i: (0, 0)
    return pl.pallas_call(
        _conv_kernel,
        grid=(n_chunks,),
        in_specs=[
            pl.BlockSpec((CONV_CHUNK, D_MODEL), row),
            pl.BlockSpec((CONV_HALO, D_MODEL), lambda i: (jnp.maximum(i * halo_per_chunk - 1, 0), 0)),
            pl.BlockSpec((CONV_HALO, D_MODEL),
                         lambda i: (jnp.minimum((i + 1) * halo_per_chunk, n_halo - 1), 0)),
            pl.BlockSpec((CONV_WIDTH + 1, D_MODEL), const),
            pl.BlockSpec((1, D_MODEL), const),
            pl.BlockSpec((1, D_MODEL), const),
            pl.BlockSpec((1, D_MODEL), const),
            pl.BlockSpec((D_MODEL, D_MODEL), const),
        ] + _split_specs(CONV_CHUNK) + [
            _mod_spec(layer, CONV_CHUNK),
        ],
        out_specs=pl.BlockSpec((CONV_CHUNK, D_MODEL), row),
        out_shape=jax.ShapeDtypeStruct((N_TOK, D_MODEL), F32),
        scratch_shapes=[
            pltpu.VMEM((CONV_CHUNK + 2 * CONV_HALO, D_MODEL), F32),
            pltpu.VMEM((SUBLANE - 1, CONV_SHIFT_ROWS, D_MODEL), F32),
            pltpu.VMEM((CONV_CHUNK, D_MODEL), F32),
        ],
        compiler_params=_cparams("arbitrary"),
        name="conv",
    )(u, u, u, dw, dwb, lng, lnb, pw2, xp, xs, mods)


FFN_CHUNK = 256


def _ffn_kernel(x_ref, mod_ref, g_ref, w1_ref, w3_ref, w2_ref, o_ref):
    mod = mod_ref[...]
    x = x_ref[...]
    h = (_rms(x, g_ref[...]) * (1.0 + _mod_part(mod, 4)) + _mod_part(mod, 3)).astype(BF16)
    y = None
    for j in range(D_FF // FFN_CHUNK):
        cols = slice(j * FFN_CHUNK, (j + 1) * FFN_CHUNK)
        t = _silu(_dot(h, w1_ref[:, cols])) * _dot(h, w3_ref[:, cols])
        part = _dot(t.astype(BF16), w2_ref[cols, :])
        y = part if y is None else y + part
    o_ref[...] = x + _mod_part(mod, 5) * y


def _ffn(x, mods, g, w1, w3, w2, layer):
    tm = 512
    const = lambda i: (0, 0)
    resident = dict(pipeline_mode=pl.Buffered(1))
    return pl.pallas_call(
        _ffn_kernel,
        grid=(N_TOK // tm,),
        in_specs=[
            pl.BlockSpec((tm, D_MODEL), lambda i: (i, 0)),
            _mod_spec(layer, tm),
            pl.BlockSpec((1, D_MODEL), const),
            pl.BlockSpec((D_MODEL, D_FF), const, **resident),
            pl.BlockSpec((D_MODEL, D_FF), const, **resident),
            pl.BlockSpec((D_FF, D_MODEL), const, **resident),
        ],
        out_specs=pl.BlockSpec((tm, D_MODEL), lambda i: (i, 0)),
        out_shape=jax.ShapeDtypeStruct((N_TOK, D_MODEL), F32),
        compiler_params=_cparams("arbitrary"),
        name="ffn",
    )(x, mods, g, w1, w3, w2)


def _mla_down_kernel(x_ref, mod_ref, g_ref, w_ref, qg_ref, kvg_ref, cq_ref, ckvb_ref, kr_ref, new_ckv_ref, new_kr_ref):
    mod = mod_ref[...]
    h = _rms(x_ref[...], g_ref[...]) * (1.0 + _mod_part(mod, 1)) + _mod_part(mod, 0)
    d = _dot(h.astype(BF16), w_ref[...])
    cq_ref[...] = _rms(d[:, :Q_LORA_RANK], qg_ref[...]).astype(BF16)
    ckv = _rms(d[:, Q_LORA_RANK:Q_LORA_RANK + KV_LORA_RANK], kvg_ref[...])
    ckvb_ref[...] = ckv.astype(BF16)
    kr = d[:, Q_LORA_RANK + KV_LORA_RANK:]
    kr_ref[...] = kr

    @pl.when(pl.program_id(0) * x_ref.shape[0] < NP_TOK)
    def _():
        new_ckv_ref[...] = ckv
        new_kr_ref[...] = kr[:, :ROPE_DIM]


def _mla_down(x, mods, g, w_down, qg, kvg, layer):
    tm = 512
    n_p = NP_TOK // tm
    n_down = Q_LORA_RANK + KV_LORA_RANK + LANE
    return pl.pallas_call(
        _mla_down_kernel,
        grid=(N_TOK // tm,),
        in_specs=[
            pl.BlockSpec((tm, D_MODEL), lambda i: (i, 0)),
            _mod_spec(layer, tm),
            pl.BlockSpec((1, D_MODEL), lambda i: (0, 0)),
            pl.BlockSpec((D_MODEL, n_down), lambda i: (0, 0)),
            pl.BlockSpec((1, Q_LORA_RANK), lambda i: (0, 0)),
            pl.BlockSpec((1, KV_LORA_RANK), lambda i: (0, 0)),
        ],
        out_specs=[
            pl.BlockSpec((tm, Q_LORA_RANK), lambda i: (i, 0)),
            pl.BlockSpec((tm, KV_LORA_RANK), lambda i: (i, 0)),
            pl.BlockSpec((tm, LANE), lambda i: (i, 0)),
            pl.BlockSpec((tm, KV_LORA_RANK), lambda i: (jnp.minimum(i, n_p - 1), 0)),
            pl.BlockSpec((tm, ROPE_DIM), lambda i: (jnp.minimum(i, n_p - 1), 0)),
        ],
        out_shape=[
            jax.ShapeDtypeStruct((N_TOK, Q_LORA_RANK), BF16),
            jax.ShapeDtypeStruct((N_TOK, KV_LORA_RANK), BF16),
            jax.ShapeDtypeStruct((N_TOK, LANE), F32),
            jax.ShapeDtypeStruct((NP_TOK, KV_LORA_RANK), F32),
            jax.ShapeDtypeStruct((NP_TOK, ROPE_DIM), F32),
        ],
        compiler_params=_cparams("arbitrary"),
        name="mla_down",
    )(x, mods, g, w_down, qg, kvg)


ATTN_BLOCK = 1024
ATTN_TQ = 256
HEAD_GROUP = 2
N_HEAD_GROUPS = N_HEADS // HEAD_GROUP
LOG2E = 1.4426950408889634


def _attn_kernel(*refs, seq, n_cache, rope):
    it = iter(refs)
    n_src = 5 if n_cache else 3
    cur_refs = [next(it) for _ in range(n_src)]
    nxt_refs = [next(it) for _ in range(n_src)]
    wuq_ref, wukv_ref, gq_ref, gk_ref = next(it), next(it), next(it), next(it)
    if rope:
        cos_ref, sin_ref = next(it), next(it)
    o_ref, k_scr, v_scr, q_scr = next(it), next(it), next(it), next(it)

    inv_dim = 1.0 / QK_HEAD_DIM
    gq, gk = gq_ref[...], gk_ref[...]
    tables = (cos_ref[...], sin_ref[...]) if rope else None
    ones_new = jnp.ones((ATTN_BLOCK, LANE), BF16)
    ones_cache = jnp.ones((n_cache, LANE), BF16) if n_cache else None

    def latents(src_refs):
        vals = [r[...] for r in src_refs]
        if n_cache:
            vals[3] = vals[3].astype(BF16)
        return vals

    cur = latents(cur_refs)

    def normed(nope, rot2, g, tabs, out_scale):
        ssq = jnp.sum(nope * nope, axis=-1, keepdims=True) + 0.5 * jnp.sum(rot2 * rot2, axis=-1, keepdims=True)
        r = lax.rsqrt(ssq * inv_dim + EPS) * out_scale
        if tabs is None:
            rot = rot2 * g[1:2, :]
        else:
            rot = rot2 * (g[1:2, :] * tabs[0]) + pltpu.roll(rot2, ROPE_DIM, 1) * (g[2:3, :] * tabs[1])
        return jnp.concatenate([(nope * r * g[0:1, :]).astype(BF16), (rot * r).astype(BF16)], axis=1)

    def build(gi, slot, src):
        cq, ckv, kr = src[:3]
        for j in range(HEAD_GROUP):
            h = gi * HEAD_GROUP + j
            wukv = wukv_ref[h]
            kv = _dot(ckv, wukv)
            k_scr[slot, j, 0:ATTN_BLOCK, :] = normed(kv[:, :LANE], kr, gk, tables, 1.0)
            v_scr[slot, j, 0:ATTN_BLOCK, :] = jnp.concatenate([kv[:, LANE:].astype(BF16), ones_new], axis=1)
            if n_cache:
                kvc = _dot(src[3], wukv)
                k_scr[slot, j, ATTN_BLOCK:, :] = normed(kvc[:, :LANE], src[4], gk, None, 1.0)
                v_scr[slot, j, ATTN_BLOCK:, :] = jnp.concatenate([kvc[:, LANE:].astype(BF16), ones_cache], axis=1)
            q = _dot(cq, wuq_ref[h])
            q_scr[slot, j] = normed(q[:, :LANE], q[:, LANE:], gq, tables, QK_HEAD_DIM ** -0.5 * LOG2E)

    def attend(gi, slot):
        heads = []
        for j in range(HEAD_GROUP):
            outs = []
            for i in range(ATTN_BLOCK // ATTN_TQ):
                rows = slice(i * ATTN_TQ, (i + 1) * ATTN_TQ)
                keys = slice(None) if seq == ATTN_BLOCK else rows
                s = lax.dot_general(q_scr[slot, j, rows, :], k_scr[slot, j, keys, :], (((1,), (1,)), ((), ())),
                                    preferred_element_type=F32)
                p = jnp.exp2((s - jnp.max(s, axis=-1, keepdims=True)).astype(BF16))
                oe = _dot(p, v_scr[slot, j, keys, :])
                outs.append((oe[:, :LANE] / oe[:, LANE:]).astype(BF16))
            heads.append(jnp.concatenate(outs, axis=0))
        o_ref[gi] = jnp.concatenate(heads, axis=1)

    @pl.when(pl.program_id(0) == 0)
    def _():
        build(0, 0, cur)

    nxt = latents(nxt_refs)
    n_trips = N_HEAD_GROUPS // 2

    def two_groups(t, carry):
        g0 = 2 * t
        build(g0 + 1, 1, cur)
        attend(g0, 0)
        wraps = t == n_trips - 1
        build((g0 + 2) % N_HEAD_GROUPS, 0, [jnp.where(wraps, n, c) for n, c in zip(nxt, cur)])
        attend(g0 + 1, 1)
        return carry

    lax.fori_loop(0, n_trips, two_groups, 0)


def _attention(cq, ckv, kr2, cache, wuq, wukv, gq, gk, tables, *, tok0, n_tok, seq):
    assert seq in (ATTN_BLOCK, ATTN_TQ) and tok0 % ATTN_BLOCK == 0 and n_tok % ATTN_BLOCK == 0
    b0 = tok0 // ATTN_BLOCK
    n_cache = 0 if cache is None else cache[0].shape[1]
    assert n_cache == 0 or seq == ATTN_BLOCK
    rope = tables is not None
    n_blocks = n_tok // ATTN_BLOCK
    const2 = lambda b: (0, 0)
    const3 = lambda b: (0, 0, 0)
    in_specs, args = [], []
    resident = dict(pipeline_mode=pl.Buffered(1))
    for ahead in (0, 1):
        blk = lambda b, ahead=ahead: jnp.minimum(b + ahead, n_blocks - 1)
        mode = resident if ahead else {}
        in_specs += [
            pl.BlockSpec((ATTN_BLOCK, Q_LORA_RANK), lambda b, blk=blk: (b0 + blk(b), 0), **mode),
            pl.BlockSpec((ATTN_BLOCK, KV_LORA_RANK), lambda b, blk=blk: (b0 + blk(b), 0), **mode),
            pl.BlockSpec((ATTN_BLOCK, LANE), lambda b, blk=blk: (b0 + blk(b), 0), **mode),
        ]
        args += [cq, ckv, kr2]
        if n_cache:
            in_specs += [pl.BlockSpec((None, n_cache, KV_LORA_RANK), lambda b, blk=blk: (blk(b), 0, 0), **mode),
                         pl.BlockSpec((None, n_cache, LANE), lambda b, blk=blk: (blk(b), 0, 0), **mode)]
            args += list(cache)
    in_specs += [
        pl.BlockSpec((N_HEADS, Q_LORA_RANK, HEAD_PAD), const3, **resident),
        pl.BlockSpec((N_HEADS, KV_LORA_RANK, HEAD_PAD), const3, **resident),
        pl.BlockSpec((SUBLANE, LANE), const2),
        pl.BlockSpec((SUBLANE, LANE), const2),
    ]
    args += [wuq, wukv, gq, gk]
    if rope:
        in_specs += [pl.BlockSpec((ATTN_BLOCK, LANE), const2, **resident),
                     pl.BlockSpec((ATTN_BLOCK, LANE), const2, **resident)]
        args += list(tables)
    return pl.pallas_call(
        functools.partial(_attn_kernel, seq=seq, n_cache=n_cache, rope=rope),
        grid=(n_blocks,),
        in_specs=in_specs,
        out_specs=pl.BlockSpec((N_HEAD_GROUPS, ATTN_BLOCK, HEAD_GROUP * V_HEAD_DIM), lambda b: (0, b, 0)),
        out_shape=jax.ShapeDtypeStruct((N_HEAD_GROUPS, n_tok, HEAD_GROUP * V_HEAD_DIM), BF16),
        scratch_shapes=[
            pltpu.VMEM((2, HEAD_GROUP, ATTN_BLOCK + n_cache, HEAD_PAD), BF16),
            pltpu.VMEM((2, HEAD_GROUP, ATTN_BLOCK + n_cache, HEAD_PAD), BF16),
            pltpu.VMEM((2, HEAD_GROUP, ATTN_BLOCK, HEAD_PAD), BF16),
        ],
        compiler_params=_cparams("arbitrary"),
        name="attn_rope" if rope else "attn",
    )(*args)


ROUTE_TM = 4 * TOK_CHUNK
ROUTE_E1, ROUTE_E2, ROUTE_RANK1, ROUTE_RANK2 = N_EXPERTS, N_EXPERTS + 1, N_EXPERTS + 2, N_EXPERTS + 3
ROUTE_ROWS = 16


def _attn_out_kernel(op_ref, os_ref, x_ref, mod_ref, g_ref, wo_ref, wr_ref,
                     x3_ref, h_ref, route_ref, route_t_ref, cstart_ref, total_ref, carry_ref):
    i = pl.program_id(0)
    mod = mod_ref[...]
    is_prompt = i * x_ref.shape[0] < NP_TOK

    @pl.when(i == 0)
    def _():
        carry_ref[...] = jnp.zeros_like(carry_ref)

    wh, wl = _split_bf16(wr_ref[...])
    lane = lax.broadcasted_iota(I32, (TOK_CHUNK, LANE), 1)
    lanef = lane.astype(F32)
    neg = jnp.float32(-jnp.inf)
    r_id = lax.broadcasted_iota(I32, (TOK_CHUNK, TOK_CHUNK), 0)
    c_id = lax.broadcasted_iota(I32, (TOK_CHUNK, TOK_CHUNK), 1)
    tri = jnp.where(c_id < r_id, 1.0, 0.0).astype(BF16)
    seen = carry_ref[0:1, :]

    for k in range(x_ref.shape[0] // TOK_CHUNK):
        rows = slice(k * TOK_CHUNK, (k + 1) * TOK_CHUNK)
        att = _dot(jnp.where(is_prompt, op_ref[0, rows, :], os_ref[0, rows, :]), wo_ref[0])
        for gi in range(1, N_HEAD_GROUPS):
            att += _dot(jnp.where(is_prompt, op_ref[gi, rows, :], os_ref[gi, rows, :]), wo_ref[gi])
        x3 = x_ref[rows, :] + _mod_part(mod, 2) * att
        x3_ref[rows, :] = x3
        h = _rms(x3, g_ref[...]) * (1.0 + _mod_part(mod, 4)) + _mod_part(mod, 3)
        hb = h.astype(BF16)
        h_ref[rows, :] = hb
        hl = (h - hb.astype(F32)).astype(BF16)
        logits = _dot(hb, wh) + (_dot(hl, wh) + _dot(hb, wl))
        logits = jnp.where(lane < N_EXPERTS, logits, neg)
        v1 = jnp.max(logits, axis=-1, keepdims=True)
        i1 = jnp.min(jnp.where(logits == v1, lanef, float(LANE)), axis=-1, keepdims=True)
        rest = jnp.where(lanef == i1, neg, logits)
        v2 = jnp.max(rest, axis=-1, keepdims=True)
        i2 = jnp.min(jnp.where(rest == v2, lanef, float(LANE)), axis=-1, keepdims=True)
        e2 = jnp.exp(v2 - v1)
        w1 = 1.0 / (1.0 + e2)
        hot1, hot2 = lanef == i1, lanef == i2
        gates = jnp.where(hot1, w1, 0.0) + jnp.where(hot2, e2 * w1, 0.0)

        hot = jnp.where(hot1 | hot2, 1.0, 0.0)
        before = _dot(tri, hot.astype(BF16)) + seen
        rank1 = jnp.sum(jnp.where(hot1, before, 0.0), axis=-1, keepdims=True)
        rank2 = jnp.sum(jnp.where(hot2, before, 0.0), axis=-1, keepdims=True)
        route = jnp.where(lane == ROUTE_E1, i1, jnp.where(lane == ROUTE_E2, i2, jnp.where(
            lane == ROUTE_RANK1, rank1, jnp.where(lane == ROUTE_RANK2, rank2, gates))))
        route_ref[rows, :] = route
        route_t_ref[:, rows] = route.T[:ROUTE_ROWS, :]
        cstart_ref[k] = jnp.broadcast_to(seen, (SUBLANE, LANE))
        seen = before[TOK_CHUNK - 1:TOK_CHUNK, :] + hot[TOK_CHUNK - 1:TOK_CHUNK, :]

    carry_ref[...] = jnp.broadcast_to(seen, (SUBLANE, LANE))
    total_ref[...] = jnp.broadcast_to(seen, (SUBLANE, LANE))


def _attn_out(o_p, o_s, x, mods, g, wo, wr, layer):
    tm = ROUTE_TM
    per = tm // TOK_CHUNK
    n_p = NP_TOK // tm
    o_block = (N_HEAD_GROUPS, tm, HEAD_GROUP * V_HEAD_DIM)
    return pl.pallas_call(
        _attn_out_kernel,
        grid=(N_TOK // tm,),
        in_specs=[
            pl.BlockSpec(o_block, lambda i: (0, jnp.minimum(i, n_p - 1), 0)),
            pl.BlockSpec(o_block, lambda i: (0, jnp.maximum(i - n_p, 0), 0)),
            pl.BlockSpec((tm, D_MODEL), lambda i: (i, 0)),
            _mod_spec(layer, tm),
            pl.BlockSpec((1, D_MODEL), lambda i: (0, 0)),
            pl.BlockSpec((N_HEAD_GROUPS, HEAD_GROUP * V_HEAD_DIM, D_MODEL), lambda i: (0, 0, 0)),
            pl.BlockSpec((D_MODEL, LANE), lambda i: (0, 0)),
        ],
        out_specs=[
            pl.BlockSpec((tm, D_MODEL), lambda i: (i, 0)),
            pl.BlockSpec((tm, D_MODEL), lambda i: (i, 0)),
            pl.BlockSpec((tm, LANE), lambda i: (i, 0)),
            pl.BlockSpec((ROUTE_ROWS, tm), lambda i: (0, i)),
            pl.BlockSpec((per, SUBLANE, LANE), lambda i: (i, 0, 0)),
            pl.BlockSpec((SUBLANE, LANE), lambda i: (0, 0)),
        ],
        out_shape=[
            jax.ShapeDtypeStruct((N_TOK, D_MODEL), F32),
            jax.ShapeDtypeStruct((N_TOK, D_MODEL), BF16),
            jax.ShapeDtypeStruct((N_TOK, LANE), F32),
            jax.ShapeDtypeStruct((ROUTE_ROWS, N_TOK), F32),
            jax.ShapeDtypeStruct((N_CHUNKS, SUBLANE, LANE), F32),
            jax.ShapeDtypeStruct((SUBLANE, LANE), F32),
        ],
        scratch_shapes=[pltpu.VMEM((SUBLANE, LANE), F32)],
        compiler_params=_cparams("arbitrary"),
        name="attn_out",
    )(o_p, o_s, x, mods, g, wo, wr)


def _routing_tables(route_t, cstart, total):
    counts = total[0, :N_EXPERTS].astype(I32)
    padded = (counts + SLOT_TILE - 1) // SLOT_TILE * SLOT_TILE
    ends = jnp.cumsum(padded)
    offs = ends - padded
    expert_ids = jnp.arange(N_EXPERTS, dtype=I32)[:, None]

    def region_start(e_row):
        return jnp.sum(jnp.where(e_row[None, :].astype(I32) == expert_ids, offs[:, None], 0), axis=0)

    slot1 = region_start(route_t[ROUTE_E1]) + route_t[ROUTE_RANK1].astype(I32)
    slot2 = region_start(route_t[ROUTE_E2]) + route_t[ROUTE_RANK2].astype(I32)

    n_active = ends[-1] // SLOT_TILE
    tile_start = jnp.arange(N_SLOT_TILES, dtype=I32) * SLOT_TILE
    tile_expert = jnp.sum(tile_start[:, None] >= ends[None, :], axis=1).astype(I32)
    last_expert = jnp.sum((n_active - 1) * SLOT_TILE >= ends).astype(I32)
    tile_active = tile_start < ends[-1]
    tile_expert = jnp.where(tile_active, tile_expert, last_expert)
    prev_expert = jnp.concatenate([jnp.full((1,), -1, I32), tile_expert[:-1]])
    tile_first = (tile_active & (tile_expert != prev_expert)).astype(I32)
    weight_slot = ((jnp.cumsum(tile_first) - 1) % 2).astype(I32)
    later = (expert_ids.T > expert_ids) & (counts > 0)[None, :]
    next_of = jnp.min(jnp.where(later, expert_ids.T, N_EXPERTS), axis=1)
    next_of = jnp.where(next_of == N_EXPERTS, -1, next_of).astype(I32)
    next_expert = jnp.sum(jnp.where(tile_expert[:, None] == expert_ids.T, next_of[None, :], 0), axis=1).astype(I32)

    cc = jnp.concatenate([cstart[:, 0, :N_EXPERTS], total[0:1, :N_EXPERTS]]).astype(I32)

    g_start = jnp.arange(N_SLOTS // GATHER_TILE, dtype=I32) * GATHER_TILE
    g_expert = jnp.minimum(jnp.sum(g_start[:, None] >= ends[None, :], axis=1), N_EXPERTS - 1).astype(I32)
    g_hot = (g_expert[None, :] == expert_ids).astype(I32)
    rank0 = g_start - jnp.sum(g_hot * offs[:, None], axis=0)
    cc_tile = jnp.sum(cc[:, :, None] * g_hot[None, :, :], axis=1)
    c_lo = jnp.sum(cc_tile[1:] <= rank0[None, :], axis=0).astype(I32)
    rank_end = jnp.minimum(rank0 + GATHER_TILE, jnp.sum(g_hot * counts[:, None], axis=0))
    c_hi = jnp.sum(cc_tile[:-1] < rank_end[None, :], axis=0).astype(I32) - 1
    idle = (g_start >= ends[-1]) | (c_hi < c_lo)
    c_lo = jnp.where(idle, 1, c_lo)
    c_hi = jnp.where(idle, 0, c_hi)

    lo = offs[None, :] + cc[:-1]
    hi = offs[None, :] + cc[1:]
    first, last = lo // SLOT_CHUNK, (hi - 1) // SLOT_CHUNK
    ids = jnp.concatenate([first, last], axis=1)
    valid = jnp.concatenate([hi > lo, (hi > lo) & (last != first)], axis=1)
    experts = jnp.tile(jnp.arange(N_EXPERTS, dtype=I32), (N_CHUNKS, 2))
    dest = jnp.cumsum(valid, axis=1) - 1
    place = (valid[:, :, None] & (dest[:, :, None] == jnp.arange(MAX_PAIRS, dtype=I32)[None, None, :])).astype(I32)
    ids = jnp.sum(ids[:, :, None] * place, axis=1).astype(I32)
    experts = jnp.sum(experts[:, :, None] * place, axis=1).astype(I32)
    n_pairs = jnp.sum(valid, axis=1).astype(I32)
    unused = jnp.arange(MAX_PAIRS, dtype=I32)[None, :] >= n_pairs[:, None]
    ids = jnp.where(unused, ids[:, 0:1], ids)
    experts = jnp.where(unused, -1, experts)
    n_pairs = n_pairs + n_pairs % 2

    pad = jnp.zeros((N_CHUNKS, SUBLANE - TOP_K, TOK_CHUNK), I32)
    slots_lane = jnp.concatenate([slot1.reshape(N_CHUNKS, 1, TOK_CHUNK), slot2.reshape(N_CHUNKS, 1, TOK_CHUNK), pad],
                                 axis=1)
    offs_row = jnp.concatenate([offs.astype(F32), jnp.zeros((LANE - N_EXPERTS,), F32)]).reshape(1, LANE)
    return dict(tile_expert=tile_expert, n_active=n_active.reshape(1).astype(I32), tile_first=tile_first,
                weight_slot=weight_slot, next_expert=next_expert, c_lo=c_lo, c_hi=c_hi,
                ids=ids.reshape(-1), experts=experts.reshape(-1), n_pairs=n_pairs,
                slots_lane=slots_lane, offs_row=offs_row)


GATHER_TILE = 256
GATHER_UNROLL = 3


def _gather_kernel(clo_ref, chi_ref, slots_ref, h_ref, o_ref, acc_ref):
    g = pl.program_id(0)
    slot_id = g * GATHER_TILE + lax.broadcasted_iota(I32, (GATHER_TILE, TOK_CHUNK), 0)
    acc_ref[...] = jnp.zeros_like(acc_ref)

    c_lo, c_hi = clo_ref[g], chi_ref[g]

    def one_hot(c, value):
        sl = slots_ref[c]
        hit = (sl[0:1, :] == slot_id) | (sl[1:2, :] == slot_id)
        return jnp.where(hit, value, 0.0).astype(BF16)

    def rows(c):
        return h_ref[pl.ds(pl.multiple_of(c * TOK_CHUNK, TOK_CHUNK), TOK_CHUNK), :]

    def body(t, carry):
        c0 = c_lo + GATHER_UNROLL * t
        total = _dot(one_hot(c0, 1.0), rows(c0))
        for k in range(1, GATHER_UNROLL):
            live = jnp.where(c0 + k <= c_hi, 1.0, 0.0)
            ck = jnp.minimum(c0 + k, c_hi)
            total += _dot(one_hot(ck, live), rows(ck))
        acc_ref[...] += total
        return carry

    lax.fori_loop(0, (c_hi - c_lo + GATHER_UNROLL) // GATHER_UNROLL, body, 0)
    o_ref[...] = acc_ref[...].astype(BF16)


def _gather(h, rt):
    return pl.pallas_call(
        _gather_kernel,
        grid_spec=pltpu.PrefetchScalarGridSpec(
            num_scalar_prefetch=2,
            grid=(N_SLOTS // GATHER_TILE,),
            in_specs=[
                pl.BlockSpec((N_CHUNKS, SUBLANE, TOK_CHUNK), lambda g, *_: (0, 0, 0)),
                pl.BlockSpec((N_TOK, D_MODEL), lambda g, *_: (0, 0), pipeline_mode=pl.Buffered(1)),
            ],
            out_specs=pl.BlockSpec((GATHER_TILE, D_MODEL), lambda g, *_: (g, 0)),
            scratch_shapes=[pltpu.VMEM((GATHER_TILE, D_MODEL), F32)],
        ),
        out_shape=jax.ShapeDtypeStruct((N_SLOTS, D_MODEL), BF16),
        compiler_params=_cparams("arbitrary"),
        name="moe_gather",
    )(rt["c_lo"], rt["c_hi"], rt["slots_lane"], h)


def _experts_kernel(te_ref, na_ref, first_ref, nxt_ref, ws_ref, h_ref, w1_hbm, w3_hbm, w2_hbm, o_ref,
                    w1_buf, w3_buf, w2_buf, sem):
    g = pl.program_id(0)
    slot = ws_ref[g]

    def weight_copies(e, s):
        return [pltpu.make_async_copy(w1_hbm.at[e], w1_buf.at[s], sem.at[s, 0]),
                pltpu.make_async_copy(w3_hbm.at[e], w3_buf.at[s], sem.at[s, 1]),
                pltpu.make_async_copy(w2_hbm.at[e], w2_buf.at[s], sem.at[s, 2])]

    @pl.when(g == 0)
    def _():
        for cp in weight_copies(te_ref[0], 0):
            cp.start()

    @pl.when(first_ref[g] == 1)
    def _():
        for cp in weight_copies(te_ref[g], slot):
            cp.wait()

        @pl.when(nxt_ref[g] >= 0)
        def _():
            for cp in weight_copies(nxt_ref[g], 1 - slot):
                cp.start()

    @pl.when(g < na_ref[0])
    def _():
        h = h_ref[...].astype(w1_buf.dtype)
        t = _silu(_dot(h, w1_buf[slot])) * _dot(h, w3_buf[slot])
        o_ref[...] = _dot(t.astype(w2_buf.dtype), w2_buf[slot]).astype(BF16)

    @pl.when(g >= na_ref[0])
    def _():
        o_ref[...] = jnp.zeros_like(o_ref)


def _experts(hs, rt, w1, w3, w2):
    tile = lambda g, te, na, *_: (jnp.minimum(g, na[0] - 1), 0)
    return pl.pallas_call(
        _experts_kernel,
        grid_spec=pltpu.PrefetchScalarGridSpec(
            num_scalar_prefetch=5,
            grid=(N_SLOT_TILES,),
            in_specs=[
                pl.BlockSpec((SLOT_TILE, D_MODEL), tile),
                pl.BlockSpec(memory_space=pl.ANY),
                pl.BlockSpec(memory_space=pl.ANY),
                pl.BlockSpec(memory_space=pl.ANY),
            ],
            out_specs=pl.BlockSpec((SLOT_TILE, D_MODEL), lambda g, *_: (g, 0)),
            scratch_shapes=[
                pltpu.VMEM((2, D_MODEL, D_FF_EXPERT), w1.dtype),
                pltpu.VMEM((2, D_MODEL, D_FF_EXPERT), w3.dtype),
                pltpu.VMEM((2, D_FF_EXPERT, D_MODEL), w2.dtype),
                pltpu.SemaphoreType.DMA((2, 3)),
            ],
        ),
        out_shape=jax.ShapeDtypeStruct((N_SLOTS, D_MODEL), BF16),
        compiler_params=_cparams("arbitrary"),
        name="moe_experts",
    )(rt["tile_expert"], rt["n_active"], rt["tile_first"], rt["next_expert"], rt["weight_slot"], hs, w1, w3, w2)


def _combine_kernel(np_ref, ids_ref, ex_ref, route_ref, offs_ref, x_ref, mod_ref, y_hbm,
                    op_ref, os_ref, buf_ref, acc_ref, sem):
    c = pl.program_id(0)
    cur = c % 2

    def chunk_copy(step, j, half):
        chunk = ids_ref[step * MAX_PAIRS + j]
        src = y_hbm.at[pl.ds(pl.multiple_of(chunk * SLOT_CHUNK, SLOT_CHUNK), SLOT_CHUNK)]
        return pltpu.make_async_copy(src, buf_ref.at[half, j], sem.at[half, j])

    def start_all(step, half):
        def go(j, carry):
            chunk_copy(step, j, half).start()
            return carry

        lax.fori_loop(0, np_ref[step], go, 0)

    @pl.when(c == 0)
    def _():
        start_all(0, 0)

    @pl.when(c + 1 < pl.num_programs(0))
    def _():
        start_all(c + 1, 1 - cur)

    route = route_ref[...]
    lanef = lax.broadcasted_iota(I32, route.shape, 1).astype(F32)
    offs = offs_ref[...]

    def slot_of(e_lane, rank_lane):
        start = jnp.sum(jnp.where(lanef == route[:, e_lane:e_lane + 1], offs, 0.0), axis=-1, keepdims=True)
        return start + route[:, rank_lane:rank_lane + 1]

    col = lax.broadcasted_iota(I32, (TOK_CHUNK, SLOT_CHUNK), 1).astype(F32)
    s1 = slot_of(ROUTE_E1, ROUTE_RANK1) - col
    s2 = slot_of(ROUTE_E2, ROUTE_RANK2) - col
    acc_ref[...] = jnp.zeros_like(acc_ref)

    def part(j):
        pair = c * MAX_PAIRS + j
        base = (ids_ref[pair] * SLOT_CHUNK).astype(F32)
        hit = (s1 == base) | (s2 == base)
        rows = _dot(jnp.where(hit, 1.0, 0.0).astype(BF16), buf_ref[cur, j])
        return _lane_pick(route, ex_ref[pair]) * rows

    def body(t, carry):
        chunk_copy(c, 2 * t, cur).wait()
        chunk_copy(c, 2 * t + 1, cur).wait()
        acc_ref[...] += part(2 * t) + part(2 * t + 1)
        return carry

    lax.fori_loop(0, np_ref[c] // 2, body, 0)
    res = x_ref[...] + _mod_part(mod_ref[...], 5) * acc_ref[...]

    @pl.when(c * TOK_CHUNK < NP_TOK)
    def _():
        op_ref[...] = res

    @pl.when(c * TOK_CHUNK >= NP_TOK)
    def _():
        os_ref[...] = res


def _combine(ys, route, x, mods, rt, layer):
    tok = lambda c, *_: (c, 0)
    return pl.pallas_call(
        _combine_kernel,
        grid_spec=pltpu.PrefetchScalarGridSpec(
            num_scalar_prefetch=3,
            grid=(N_CHUNKS,),
            in_specs=[
                pl.BlockSpec((TOK_CHUNK, LANE), tok),
                pl.BlockSpec((1, LANE), lambda c, *_: (0, 0)),
                pl.BlockSpec((TOK_CHUNK, D_MODEL), tok),
                _mod_spec(layer, TOK_CHUNK),
                pl.BlockSpec(memory_space=pl.ANY),
            ],
            out_specs=_split_specs(TOK_CHUNK),
            scratch_shapes=[
                pltpu.VMEM((2, MAX_PAIRS, SLOT_CHUNK, D_MODEL), BF16),
                pltpu.VMEM((TOK_CHUNK, D_MODEL), F32),
                pltpu.SemaphoreType.DMA((2, MAX_PAIRS)),
            ],
        ),
        out_shape=[jax.ShapeDtypeStruct((NP_TOK, D_MODEL), F32), jax.ShapeDtypeStruct((NS_TOK, D_MODEL), F32)],
        compiler_params=_cparams("arbitrary"),
        name="moe_combine",
    )(rt["n_pairs"], rt["ids"], rt["experts"], route, rt["offs_row"], x, mods, ys)


def _rope_partner(t):
    half = AXIS_ROPE_DIM // 2
    s = t.shape[:-1]
    return t.reshape(s + (2, 2, half))[..., ::-1, :].reshape(s + (ROPE_DIM,))


def _rope_tables(n_tokens):
    rows = n_tokens // GRID_W
    row = np.repeat(np.arange(rows), GRID_W).astype(np.float32)
    col = np.tile(np.arange(GRID_W), rows).astype(np.float32)
    inv = (ROPE_BASE ** (-np.arange(0, AXIS_ROPE_DIM, 2, dtype=np.float32) / AXIS_ROPE_DIM)).astype(np.float32)
    ar, ac = row[:, None] * inv, col[:, None] * inv
    cos = np.concatenate([np.cos(ar), np.cos(ar), np.cos(ac), np.cos(ac)], axis=-1)
    sin = np.concatenate([-np.sin(ar), np.sin(ar), -np.sin(ac), np.sin(ac)], axis=-1)
    zeros = np.zeros_like(cos)
    return (jnp.asarray(np.concatenate([cos, zeros], axis=-1), F32),
            jnp.asarray(np.concatenate([sin, zeros], axis=-1), F32))


def _qk_gain_rows(g):
    z = jnp.zeros((ROPE_DIM,), F32)
    rows = jnp.stack([g[:QK_NOPE_DIM],
                      jnp.concatenate([g[QK_NOPE_DIM:], z]),
                      jnp.concatenate([_rope_partner(g[QK_NOPE_DIM:]), z])])
    return jnp.concatenate([rows, jnp.zeros((SUBLANE - 3, LANE), F32)])


def kernel(x_prompt, x_sample, c, cache_ckv, cache_krope, c_ctx, ada_w, ada_b, norm1_g, norm2_g, conv_pw1, conv_dw, conv_dw_b, conv_ln_g, conv_ln_b, conv_pw2, ffn_w1, ffn_w3, ffn_w2, mla_wdq, mla_q_norm_g, mla_wuq, mla_wdkv, mla_kv_norm_g, mla_wukv, mla_q_qk_g, mla_k_qk_g, mla_wo, moe_router, moe_w1, moe_w3, moe_w2):
    xp, xs = x_prompt.reshape(NP_TOK, D_MODEL), x_sample.reshape(NS_TOK, D_MODEL)
    cond16 = jnp.concatenate([c_ctx[None, :], c, jnp.zeros((MOD_ROWS - 1 - DEC_BATCH, D_MODEL), F32)])
    mods = _adaln(cond16, ada_w, ada_b).reshape(2, MOD_ROWS, 1, N_MOD * D_MODEL)
    vec = lambda a: a.reshape(1, -1)

    u = _glu(xp, xs, mods, vec(norm1_g[0]), conv_pw1[0].astype(BF16), 0)
    dw = jnp.concatenate([conv_dw[0], jnp.zeros((1, D_MODEL), F32)])
    x = _conv(u, xp, xs, mods, dw, vec(conv_dw_b[0]), vec(conv_ln_g[0]), vec(conv_ln_b[0]),
              conv_pw2[0].astype(BF16), 0)
    x = _ffn(x, mods, vec(norm2_g[0]), ffn_w1[0].astype(BF16), ffn_w3[0].astype(BF16),
             ffn_w2[0].astype(BF16), 0)

    wdkv = mla_wdkv[0]
    w_down = jnp.concatenate([mla_wdq[0], wdkv, _rope_partner(wdkv[:, KV_LORA_RANK:])], axis=1).astype(BF16)
    cq, ckv_b, kr2, new_ckv, new_krope = _mla_down(x, mods, vec(norm1_g[1]), w_down, vec(mla_q_norm_g[0]),
                                                   vec(mla_kv_norm_g[0]), 1)

    wuq = mla_wuq[0].reshape(Q_LORA_RANK, N_HEADS, QK_HEAD_DIM)
    wuq = jnp.concatenate([wuq, _rope_partner(wuq[..., QK_NOPE_DIM:])], axis=-1)
    wuq = wuq.transpose(1, 0, 2).astype(BF16)
    wukv = mla_wukv[0].reshape(KV_LORA_RANK, N_HEADS, HEAD_PAD).transpose(1, 0, 2).astype(BF16)
    gq, gk = _qk_gain_rows(mla_q_qk_g[0]), _qk_gain_rows(mla_k_qk_g[0])
    ckr = cache_krope[:, 0]
    cache = (cache_ckv[:, 0], jnp.concatenate([ckr, _rope_partner(ckr)], axis=-1))
    o_p = _attention(cq, ckv_b, kr2, None, wuq, wukv, gq, gk, None, tok0=0, n_tok=NP_TOK, seq=SEQ)
    o_s = _attention(cq, ckv_b, kr2, cache, wuq, wukv, gq, gk, _rope_tables(DEC_SEQ),
                     tok0=NP_TOK, n_tok=NS_TOK, seq=DEC_SEQ)

    wr = jnp.concatenate([moe_router[0], jnp.zeros((D_MODEL, LANE - N_EXPERTS), F32)], axis=1)
    wo = mla_wo[0].astype(BF16).reshape(N_HEAD_GROUPS, HEAD_GROUP * V_HEAD_DIM, D_MODEL)
    x, h, route, route_t, cstart, total = _attn_out(o_p, o_s, x, mods, vec(norm2_g[1]), wo, wr, 1)
    rt = _routing_tables(route_t, cstart, total)
    hs = _gather(h, rt)
    ys = _experts(hs, rt, moe_w1[0], moe_w3[0], moe_w2[0])
    yp, ysamp = _combine(ys, route, x, mods, rt, 1)

    return (yp.reshape(BATCH, SEQ, D_MODEL), ysamp.reshape(DEC_BATCH, DEC_SEQ, D_MODEL),
            new_ckv.reshape(BATCH, 1, SEQ, KV_LORA_RANK), new_krope.reshape(BATCH, 1, SEQ, ROPE_DIM))
```

```python
import functools

import jax
import jax.numpy as jnp
import numpy as np
from jax import lax
from jax.experimental import pallas as pl
from jax.experimental.pallas import tpu as pltpu

D_MODEL = 1024
BATCH = 32
SEQ = 256
DEC_BATCH = 8
DEC_SEQ = 1024
PAST_LEN = 512
GRID_W = 64
N_MOD = 6
CONV_WIDTH = 31
CONV_PAD = CONV_WIDTH // 2
N_HEADS = 16
QK_NOPE_DIM = 128
ROPE_DIM = 64
QK_HEAD_DIM = QK_NOPE_DIM + ROPE_DIM
V_HEAD_DIM = 128
Q_LORA_RANK = 512
KV_LORA_RANK = 256
AXIS_ROPE_DIM = ROPE_DIM // 2
ROPE_BASE = 10000.0
D_FF = 2816
N_EXPERTS = 8
TOP_K = 2
D_FF_EXPERT = 1536
EPS = 1e-6
F32 = jnp.float32
BF16 = jnp.bfloat16
I32 = jnp.int32

NP_TOK = BATCH * SEQ
NS_TOK = DEC_BATCH * DEC_SEQ
N_TOK = NP_TOK + NS_TOK
MOD_ROWS = 16
LANE = 128
SUBLANE = 8
HEAD_PAD = 2 * LANE
VMEM_LIMIT = 56 * 1024 * 1024

TOK_CHUNK = 256
N_CHUNKS = N_TOK // TOK_CHUNK
SLOT_TILE = 512
N_SLOT_TILES = (TOP_K * N_TOK + N_EXPERTS * (SLOT_TILE - 1)) // SLOT_TILE
N_SLOTS = N_SLOT_TILES * SLOT_TILE
SLOT_CHUNK = 256
MAX_PAIRS = 2 * N_EXPERTS


def _cparams(*sem):
    return pltpu.CompilerParams(dimension_semantics=sem, vmem_limit_bytes=VMEM_LIMIT)


def _mod_row(tile, tm):
    start = tile * tm
    return jnp.where(start < NP_TOK, 0, 1 + (start - NP_TOK) // DEC_SEQ)


def _mod_spec(layer, tm):
    return pl.BlockSpec((None, None, 1, N_MOD * D_MODEL),
                        lambda i, *_: (layer, _mod_row(i, tm), 0, 0))


def _split_specs(tm):
    n_p = NP_TOK // tm
    return [pl.BlockSpec((tm, D_MODEL), lambda i, *_: (jnp.minimum(i, n_p - 1), 0)),
            pl.BlockSpec((tm, D_MODEL), lambda i, *_: (jnp.maximum(i - n_p, 0), 0))]


def _pick_tokens(tm, xp_ref, xs_ref):
    return jnp.where(pl.program_id(0) * tm < NP_TOK, xp_ref[...], xs_ref[...])


def _mod_part(mod, k):
    return mod[:, k * D_MODEL:(k + 1) * D_MODEL]


def _rms(x, g):
    return x * lax.rsqrt(jnp.mean(x * x, axis=-1, keepdims=True) + EPS) * g


def _silu(x):
    return x * jax.nn.sigmoid(x)


def _split_bf16(x):
    hi = x.astype(BF16)
    lo = (x - hi.astype(F32)).astype(BF16)
    return hi, lo


def _dot(a, b):
    return jnp.dot(a, b, preferred_element_type=F32)


def _dot3(a, b):
    ah, al = _split_bf16(a)
    bh, bl = _split_bf16(b)
    return _dot(ah, bh) + (_dot(al, bh) + _dot(ah, bl))


def _lane_pick(x, idx):
    lane = lax.broadcasted_iota(I32, x.shape, 1)
    return jnp.sum(jnp.where(lane == idx, x, 0.0), axis=-1, keepdims=True)


def _adaln_kernel(cond_ref, w_ref, b_ref, o_ref):
    o_ref[...] = _dot3(_silu(cond_ref[...]), w_ref[...]) + b_ref[...]


def _adaln(cond16, ada_w, ada_b):
    depth = ada_w.shape[0]
    tn = 1536
    return pl.pallas_call(
        _adaln_kernel,
        grid=(depth, N_MOD * D_MODEL // tn),
        in_specs=[
            pl.BlockSpec((MOD_ROWS, D_MODEL), lambda l, j: (0, 0)),
            pl.BlockSpec((None, D_MODEL, tn), lambda l, j: (l, 0, j)),
            pl.BlockSpec((None, 1, tn), lambda l, j: (l, 0, j)),
        ],
        out_specs=pl.BlockSpec((None, MOD_ROWS, tn), lambda l, j: (l, 0, j)),
        out_shape=jax.ShapeDtypeStruct((depth, MOD_ROWS, N_MOD * D_MODEL), F32),
        compiler_params=_cparams("arbitrary", "arbitrary"),
        name="adaln",
    )(cond16, ada_w, ada_b.reshape(depth, 1, N_MOD * D_MODEL))


def _glu_kernel(xp_ref, xs_ref, mod_ref, g_ref, w_ref, u_ref):
    mod = mod_ref[...]
    x = _pick_tokens(u_ref.shape[0], xp_ref, xs_ref)
    h = _rms(x, g_ref[...]) * (1.0 + _mod_part(mod, 1)) + _mod_part(mod, 0)
    ag = _dot(h.astype(BF16), w_ref[...])
    u_ref[...] = ag[:, :D_MODEL] * jax.nn.sigmoid(ag[:, D_MODEL:])


def _glu(xp, xs, mods, g, pw1, layer):
    tm = 1024
    return pl.pallas_call(
        _glu_kernel,
        grid=(N_TOK // tm,),
        in_specs=_split_specs(tm) + [
            _mod_spec(layer, tm),
            pl.BlockSpec((1, D_MODEL), lambda i: (0, 0)),
            pl.BlockSpec((D_MODEL, 2 * D_MODEL), lambda i: (0, 0)),
        ],
        out_specs=pl.BlockSpec((tm, D_MODEL), lambda i: (i, 0)),
        out_shape=jax.ShapeDtypeStruct((N_TOK, D_MODEL), F32),
        compiler_params=_cparams("arbitrary"),
        name="glu",
    )(xp, xs, mods, g, pw1)


CONV_CHUNK = 256
CONV_HALO = 16
CONV_ROWS = 64
CONV_SHIFT_ROWS = CONV_CHUNK + (CONV_HALO - CONV_PAD + CONV_WIDTH - 1) // SUBLANE * SUBLANE


def _conv_kernel(uc_ref, up_ref, un_ref, dw_ref, dwb_ref, lng_ref, lnb_ref, w_ref, xp_ref, xs_ref, mod_ref,
                 o_ref, pad_ref, shift_ref, conv_ref):
    i = pl.program_id(0)
    start = i * CONV_CHUNK
    seq_len = jnp.where(start < NP_TOK, SEQ, DEC_SEQ)
    off = jnp.where(start < NP_TOK, start, start - NP_TOK) % seq_len
    prev_ok = off > 0
    next_ok = off + CONV_CHUNK < seq_len
    pad_ref[0:CONV_HALO, :] = jnp.where(prev_ok, up_ref[...], 0.0)
    pad_ref[CONV_HALO:CONV_HALO + CONV_CHUNK, :] = uc_ref[...]
    pad_ref[CONV_HALO + CONV_CHUNK:, :] = jnp.where(next_ok, un_ref[...], 0.0)

    base = CONV_HALO - CONV_PAD
    for b in range(1, SUBLANE):
        shift_ref[b - 1] = pad_ref[b:b + CONV_SHIFT_ROWS, :]
    for c in range(D_MODEL // LANE):
        cs = slice(c * LANE, (c + 1) * LANE)
        wcol = dw_ref[:, cs]
        bias = dwb_ref[:, cs]
        for r in range(CONV_CHUNK // CONV_ROWS):
            acc = jnp.broadcast_to(bias, (CONV_ROWS, LANE))
            for k in range(CONV_WIDTH):
                a, b = divmod(base + k, SUBLANE)
                lo = r * CONV_ROWS + SUBLANE * a
                src = pad_ref if b == 0 else shift_ref.at[b - 1]
                acc = acc + wcol[k:k + 1, :] * src[lo:lo + CONV_ROWS, cs]
            conv_ref[r * CONV_ROWS:(r + 1) * CONV_ROWS, cs] = acc

    t = conv_ref[...]
    mu = jnp.mean(t, axis=-1, keepdims=True)
    tc = t - mu
    y = tc * lax.rsqrt(jnp.mean(tc * tc, axis=-1, keepdims=True) + EPS) * lng_ref[...] + lnb_ref[...]
    res = _dot(_silu(y).astype(BF16), w_ref[...])
    o_ref[...] = _pick_tokens(CONV_CHUNK, xp_ref, xs_ref) + _mod_part(mod_ref[...], 2) * res


def _conv(u, xp, xs, mods, dw, dwb, lng, lnb, pw2, layer):
    n_chunks = N_TOK // CONV_CHUNK
    halo_per_chunk = CONV_CHUNK // CONV_HALO
    n_halo = N_TOK // CONV_HALO
    row = lambda i: (i, 0)
    const = lambda i: (0, 0)
    return pl.pallas_call(
        _conv_kernel,
        grid=(n_chunks,),
        in_specs=[
            pl.BlockSpec((CONV_CHUNK, D_MODEL), row),
            pl.BlockSpec((CONV_HALO, D_MODEL), lambda i: (jnp.maximum(i * halo_per_chunk - 1, 0), 0)),
            pl.BlockSpec((CONV_HALO, D_MODEL),
                         lambda i: (jnp.minimum((i + 1) * halo_per_chunk, n_halo - 1), 0)),
            pl.BlockSpec((CONV_WIDTH + 1, D_MODEL), const),
            pl.BlockSpec((1, D_MODEL), const),
            pl.BlockSpec((1, D_MODEL), const),
            pl.BlockSpec((1, D_MODEL), const),
            pl.BlockSpec((D_MODEL, D_MODEL), const),
        ] + _split_specs(CONV_CHUNK) + [
            _mod_spec(layer, CONV_CHUNK),
        ],
        out_specs=pl.BlockSpec((CONV_CHUNK, D_MODEL), row),
        out_shape=jax.ShapeDtypeStruct((N_TOK, D_MODEL), F32),
        scratch_shapes=[
            pltpu.VMEM((CONV_CHUNK + 2 * CONV_HALO, D_MODEL), F32),
            pltpu.VMEM((SUBLANE - 1, CONV_SHIFT_ROWS, D_MODEL), F32),
            pltpu.VMEM((CONV_CHUNK, D_MODEL), F32),
        ],
        compiler_params=_cparams("arbitrary"),
        name="conv",
    )(u, u, u, dw, dwb, lng, lnb, pw2, xp, xs, mods)


FFN_CHUNK = 256


def _ffn_kernel(x_ref, mod_ref, g_ref, w1_ref, w3_ref, w2_ref, o_ref):
    mod = mod_ref[...]
    x = x_ref[...]
    h = (_rms(x, g_ref[...]) * (1.0 + _mod_part(mod, 4)) + _mod_part(mod, 3)).astype(BF16)
    y = None
    for j in range(D_FF // FFN_CHUNK):
        cols = slice(j * FFN_CHUNK, (j + 1) * FFN_CHUNK)
        t = _silu(_dot(h, w1_ref[:, cols])) * _dot(h, w3_ref[:, cols])
        part = _dot(t.astype(BF16), w2_ref[cols, :])
        y = part if y is None else y + part
    o_ref[...] = x + _mod_part(mod, 5) * y


def _ffn(x, mods, g, w1, w3, w2, layer):
    tm = 512
    const = lambda i: (0, 0)
    resident = dict(pipeline_mode=pl.Buffered(1))
    return pl.pallas_call(
        _ffn_kernel,
        grid=(N_TOK // tm,),
        in_specs=[
            pl.BlockSpec((tm, D_MODEL), lambda i: (i, 0)),
            _mod_spec(layer, tm),
            pl.BlockSpec((1, D_MODEL), const),
            pl.BlockSpec((D_MODEL, D_FF), const, **resident),
            pl.BlockSpec((D_MODEL, D_FF), const, **resident),
            pl.BlockSpec((D_FF, D_MODEL), const, **resident),
        ],
        out_specs=pl.BlockSpec((tm, D_MODEL), lambda i: (i, 0)),
        out_shape=jax.ShapeDtypeStruct((N_TOK, D_MODEL), F32),
        compiler_params=_cparams("arbitrary"),
        name="ffn",
    )(x, mods, g, w1, w3, w2)


def _mla_down_kernel(x_ref, mod_ref, g_ref, w_ref, qg_ref, kvg_ref, cq_ref, ckvb_ref, kr_ref, new_ckv_ref, new_kr_ref):
    mod = mod_ref[...]
    h = _rms(x_ref[...], g_ref[...]) * (1.0 + _mod_part(mod, 1)) + _mod_part(mod, 0)
    d = _dot(h.astype(BF16), w_ref[...])
    cq_ref[...] = _rms(d[:, :Q_LORA_RANK], qg_ref[...]).astype(BF16)
    ckv = _rms(d[:, Q_LORA_RANK:Q_LORA_RANK + KV_LORA_RANK], kvg_ref[...])
    ckvb_ref[...] = ckv.astype(BF16)
    kr = d[:, Q_LORA_RANK + KV_LORA_RANK:]
    kr_ref[...] = kr

    @pl.when(pl.program_id(0) * x_ref.shape[0] < NP_TOK)
    def _():
        new_ckv_ref[...] = ckv
        new_kr_ref[...] = kr[:, :ROPE_DIM]


def _mla_down(x, mods, g, w_down, qg, kvg, layer):
    tm = 512
    n_p = NP_TOK // tm
    n_down = Q_LORA_RANK + KV_LORA_RANK + LANE
    return pl.pallas_call(
        _mla_down_kernel,
        grid=(N_TOK // tm,),
        in_specs=[
            pl.BlockSpec((tm, D_MODEL), lambda i: (i, 0)),
            _mod_spec(layer, tm),
            pl.BlockSpec((1, D_MODEL), lambda i: (0, 0)),
            pl.BlockSpec((D_MODEL, n_down), lambda i: (0, 0)),
            pl.BlockSpec((1, Q_LORA_RANK), lambda i: (0, 0)),
            pl.BlockSpec((1, KV_LORA_RANK), lambda i: (0, 0)),
        ],
        out_specs=[
            pl.BlockSpec((tm, Q_LORA_RANK), lambda i: (i, 0)),
            pl.BlockSpec((tm, KV_LORA_RANK), lambda i: (i, 0)),
            pl.BlockSpec((tm, LANE), lambda i: (i, 0)),
            pl.BlockSpec((tm, KV_LORA_RANK), lambda i: (jnp.minimum(i, n_p - 1), 0)),
            pl.BlockSpec((tm, ROPE_DIM), lambda i: (jnp.minimum(i, n_p - 1), 0)),
        ],
        out_shape=[
            jax.ShapeDtypeStruct((N_TOK, Q_LORA_RANK), BF16),
            jax.ShapeDtypeStruct((N_TOK, KV_LORA_RANK), BF16),
            jax.ShapeDtypeStruct((N_TOK, LANE), F32),
            jax.ShapeDtypeStruct((NP_TOK, KV_LORA_RANK), F32),
            jax.ShapeDtypeStruct((NP_TOK, ROPE_DIM), F32),
        ],
        compiler_params=_cparams("arbitrary"),
        name="mla_down",
    )(x, mods, g, w_down, qg, kvg)


ATTN_BLOCK = 1024
ATTN_TQ = 256
HEAD_GROUP = 2
N_HEAD_GROUPS = N_HEADS // HEAD_GROUP
LOG2E = 1.4426950408889634
ATTN_VMEM_LIMIT = 60 * 1024 * 1024


def _attn_kernel(*refs, seq, n_cache, rope):
    it = iter(refs)
    n_src = 5 if n_cache else 3
    cur_refs = [next(it) for _ in range(n_src)]
    nxt_refs = [next(it) for _ in range(n_src)]
    wuq_ref, wukv_ref, gq_ref, gk_ref = next(it), next(it), next(it), next(it)
    if rope:
        cos_ref, sin_ref = next(it), next(it)
    o_ref, k_scr, v_scr, q_scr = next(it), next(it), next(it), next(it)

    inv_dim = 1.0 / QK_HEAD_DIM
    gq, gk = gq_ref[...], gk_ref[...]
    tables = (cos_ref[...], sin_ref[...]) if rope else None
    ones_new = jnp.ones((ATTN_BLOCK, LANE), BF16)
    ones_cache = jnp.ones((n_cache, LANE), BF16) if n_cache else None

    def latents(src_refs):
        vals = [r[...] for r in src_refs]
        if n_cache:
            vals[3] = vals[3].astype(BF16)
        return vals

    cur = latents(cur_refs)

    def normed(nope, rot2, g, tabs, out_scale):
        ssq = jnp.sum(nope * nope, axis=-1, keepdims=True) + 0.5 * jnp.sum(rot2 * rot2, axis=-1, keepdims=True)
        r = lax.rsqrt(ssq * inv_dim + EPS) * out_scale
        if tabs is None:
            rot = rot2 * g[1:2, :]
        else:
            rot = rot2 * (g[1:2, :] * tabs[0]) + pltpu.roll(rot2, ROPE_DIM, 1) * (g[2:3, :] * tabs[1])
        return jnp.concatenate([(nope * r * g[0:1, :]).astype(BF16), (rot * r).astype(BF16)], axis=1)

    def build(gi, slot, src):
        cq, ckv, kr = src[:3]
        for j in range(HEAD_GROUP):
            h = gi * HEAD_GROUP + j
            wukv = wukv_ref[h]
            kv = _dot(ckv, wukv)
            k_scr[slot, j, 0:ATTN_BLOCK, :] = normed(kv[:, :LANE], kr, gk, tables, 1.0)
            v_scr[slot, j, 0:ATTN_BLOCK, :] = jnp.concatenate([kv[:, LANE:].astype(BF16), ones_new], axis=1)
            if n_cache:
                kvc = _dot(src[3], wukv)
                k_scr[slot, j, ATTN_BLOCK:, :] = normed(kvc[:, :LANE], src[4], gk, None, 1.0)
                v_scr[slot, j, ATTN_BLOCK:, :] = jnp.concatenate([kvc[:, LANE:].astype(BF16), ones_cache], axis=1)
            q = _dot(cq, wuq_ref[h])
            q_scr[slot, j] = normed(q[:, :LANE], q[:, LANE:], gq, tables, QK_HEAD_DIM ** -0.5 * LOG2E)

    def attend(gi, slot):
        heads = []
        for j in range(HEAD_GROUP):
            outs = []
            for i in range(ATTN_BLOCK // ATTN_TQ):
                rows = slice(i * ATTN_TQ, (i + 1) * ATTN_TQ)
                keys = slice(None) if seq == ATTN_BLOCK else rows
                s = lax.dot_general(q_scr[slot, j, rows, :], k_scr[slot, j, keys, :], (((1,), (1,)), ((), ())),
                                    preferred_element_type=F32)
                p = jnp.exp2((s - jnp.max(s, axis=-1, keepdims=True)).astype(BF16))
                oe = _dot(p, v_scr[slot, j, keys, :])
                outs.append((oe[:, :LANE] / oe[:, LANE:]).astype(BF16))
            heads.append(jnp.concatenate(outs, axis=0))
        o_ref[gi] = jnp.concatenate(heads, axis=1)

    @pl.when(pl.program_id(0) == 0)
    def _():
        build(0, 0, cur)

    nxt = latents(nxt_refs)
    n_trips = N_HEAD_GROUPS // 2

    def two_groups(t, carry):
        g0 = 2 * t
        build(g0 + 1, 1, cur)
        attend(g0, 0)
        wraps = t == n_trips - 1
        build((g0 + 2) % N_HEAD_GROUPS, 0, [jnp.where(wraps, n, c) for n, c in zip(nxt, cur)])
        attend(g0 + 1, 1)
        return carry

    lax.fori_loop(0, n_trips, two_groups, 0)


def _attention(cq, ckv, kr2, cache, wuq, wukv, gq, gk, tables, *, tok0, n_tok, seq):
    assert seq in (ATTN_BLOCK, ATTN_TQ) and tok0 % ATTN_BLOCK == 0 and n_tok % ATTN_BLOCK == 0
    b0 = tok0 // ATTN_BLOCK
    n_cache = 0 if cache is None else cache[0].shape[1]
    assert n_cache == 0 or seq == ATTN_BLOCK
    rope = tables is not None
    n_blocks = n_tok // ATTN_BLOCK
    const2 = lambda b: (0, 0)
    const3 = lambda b: (0, 0, 0)
    in_specs, args = [], []
    resident = dict(pipeline_mode=pl.Buffered(1))
    for ahead in (0, 1):
        blk = lambda b, ahead=ahead: jnp.minimum(b + ahead, n_blocks - 1)
        mode = {}
        in_specs += [
            pl.BlockSpec((ATTN_BLOCK, Q_LORA_RANK), lambda b, blk=blk: (b0 + blk(b), 0), **mode),
            pl.BlockSpec((ATTN_BLOCK, KV_LORA_RANK), lambda b, blk=blk: (b0 + blk(b), 0), **mode),
            pl.BlockSpec((ATTN_BLOCK, LANE), lambda b, blk=blk: (b0 + blk(b), 0), **mode),
        ]
        args += [cq, ckv, kr2]
        if n_cache:
            in_specs += [pl.BlockSpec((None, n_cache, KV_LORA_RANK), lambda b, blk=blk: (blk(b), 0, 0), **mode),
                         pl.BlockSpec((None, n_cache, LANE), lambda b, blk=blk: (blk(b), 0, 0), **mode)]
            args += list(cache)
    in_specs += [
        pl.BlockSpec((N_HEADS, Q_LORA_RANK, HEAD_PAD), const3, **resident),
        pl.BlockSpec((N_HEADS, KV_LORA_RANK, HEAD_PAD), const3, **resident),
        pl.BlockSpec((SUBLANE, LANE), const2),
        pl.BlockSpec((SUBLANE, LANE), const2),
    ]
    args += [wuq, wukv, gq, gk]
    if rope:
        in_specs += [pl.BlockSpec((ATTN_BLOCK, LANE), const2, **resident),
                     pl.BlockSpec((ATTN_BLOCK, LANE), const2, **resident)]
        args += list(tables)
    return pl.pallas_call(
        functools.partial(_attn_kernel, seq=seq, n_cache=n_cache, rope=rope),
        grid=(n_blocks,),
        in_specs=in_specs,
        out_specs=pl.BlockSpec((N_HEAD_GROUPS, ATTN_BLOCK, HEAD_GROUP * V_HEAD_DIM), lambda b: (0, b, 0)),
        out_shape=jax.ShapeDtypeStruct((N_HEAD_GROUPS, n_tok, HEAD_GROUP * V_HEAD_DIM), BF16),
        scratch_shapes=[
            pltpu.VMEM((2, HEAD_GROUP, ATTN_BLOCK + n_cache, HEAD_PAD), BF16),
            pltpu.VMEM((2, HEAD_GROUP, ATTN_BLOCK + n_cache, HEAD_PAD), BF16),
            pltpu.VMEM((2, HEAD_GROUP, ATTN_BLOCK, HEAD_PAD), BF16),
        ],
        compiler_params=pltpu.CompilerParams(dimension_semantics=("arbitrary",), vmem_limit_bytes=ATTN_VMEM_LIMIT),
        name="attn_rope" if rope else "attn",
    )(*args)


ROUTE_TM = 4 * TOK_CHUNK
ROUTE_E1, ROUTE_E2, ROUTE_RANK1, ROUTE_RANK2 = N_EXPERTS, N_EXPERTS + 1, N_EXPERTS + 2, N_EXPERTS + 3
ROUTE_ROWS = 16


def _attn_out_kernel(op_ref, os_ref, x_ref, mod_ref, g_ref, wo_ref, wr_ref,
                     x3_ref, h_ref, route_ref, route_t_ref, cstart_ref, total_ref, carry_ref):
    i = pl.program_id(0)
    mod = mod_ref[...]
    is_prompt = i * x_ref.shape[0] < NP_TOK

    @pl.when(i == 0)
    def _():
        carry_ref[...] = jnp.zeros_like(carry_ref)

    wh, wl = _split_bf16(wr_ref[...])
    lane = lax.broadcasted_iota(I32, (TOK_CHUNK, LANE), 1)
    lanef = lane.astype(F32)
    neg = jnp.float32(-jnp.inf)
    r_id = lax.broadcasted_iota(I32, (TOK_CHUNK, TOK_CHUNK), 0)
    c_id = lax.broadcasted_iota(I32, (TOK_CHUNK, TOK_CHUNK), 1)
    tri = jnp.where(c_id < r_id, 1.0, 0.0).astype(BF16)
    seen = carry_ref[0:1, :]

    for k in range(x_ref.shape[0] // TOK_CHUNK):
        rows = slice(k * TOK_CHUNK, (k + 1) * TOK_CHUNK)
        att = _dot(jnp.where(is_prompt, op_ref[0, rows, :], os_ref[0, rows, :]), wo_ref[0])
        for gi in range(1, N_HEAD_GROUPS):
            att += _dot(jnp.where(is_prompt, op_ref[gi, rows, :], os_ref[gi, rows, :]), wo_ref[gi])
        x3 = x_ref[rows, :] + _mod_part(mod, 2) * att
        x3_ref[rows, :] = x3
        h = _rms(x3, g_ref[...]) * (1.0 + _mod_part(mod, 4)) + _mod_part(mod, 3)
        hb = h.astype(BF16)
        h_ref[rows, :] = hb
        hl = (h - hb.astype(F32)).astype(BF16)
        logits = _dot(hb, wh) + (_dot(hl, wh) + _dot(hb, wl))
        logits = jnp.where(lane < N_EXPERTS, logits, neg)
        v1 = jnp.max(logits, axis=-1, keepdims=True)
        i1 = jnp.min(jnp.where(logits == v1, lanef, float(LANE)), axis=-1, keepdims=True)
        rest = jnp.where(lanef == i1, neg, logits)
        v2 = jnp.max(rest, axis=-1, keepdims=True)
        i2 = jnp.min(jnp.where(rest == v2, lanef, float(LANE)), axis=-1, keepdims=True)
        e2 = jnp.exp(v2 - v1)
        w1 = 1.0 / (1.0 + e2)
        hot1, hot2 = lanef == i1, lanef == i2
        gates = jnp.where(hot1, w1, 0.0) + jnp.where(hot2, e2 * w1, 0.0)

        hot = jnp.where(hot1 | hot2, 1.0, 0.0)
        before = _dot(tri, hot.astype(BF16)) + seen
        rank1 = jnp.sum(jnp.where(hot1, before, 0.0), axis=-1, keepdims=True)
        rank2 = jnp.sum(jnp.where(hot2, before, 0.0), axis=-1, keepdims=True)
        route = jnp.where(lane == ROUTE_E1, i1, jnp.where(lane == ROUTE_E2, i2, jnp.where(
            lane == ROUTE_RANK1, rank1, jnp.where(lane == ROUTE_RANK2, rank2, gates))))
        route_ref[rows, :] = route
        route_t_ref[:, rows] = route.T[:ROUTE_ROWS, :]
        cstart_ref[k] = jnp.broadcast_to(seen, (SUBLANE, LANE))
        seen = before[TOK_CHUNK - 1:TOK_CHUNK, :] + hot[TOK_CHUNK - 1:TOK_CHUNK, :]

    carry_ref[...] = jnp.broadcast_to(seen, (SUBLANE, LANE))
    total_ref[...] = jnp.broadcast_to(seen, (SUBLANE, LANE))


def _attn_out(o_p, o_s, x, mods, g, wo, wr, layer):
    tm = ROUTE_TM
    per = tm // TOK_CHUNK
    n_p = NP_TOK // tm
    o_block = (N_HEAD_GROUPS, tm, HEAD_GROUP * V_HEAD_DIM)
    return pl.pallas_call(
        _attn_out_kernel,
        grid=(N_TOK // tm,),
        in_specs=[
            pl.BlockSpec(o_block, lambda i: (0, jnp.minimum(i, n_p - 1), 0)),
            pl.BlockSpec(o_block, lambda i: (0, jnp.maximum(i - n_p, 0), 0)),
            pl.BlockSpec((tm, D_MODEL), lambda i: (i, 0)),
            _mod_spec(layer, tm),
            pl.BlockSpec((1, D_MODEL), lambda i: (0, 0)),
            pl.BlockSpec((N_HEAD_GROUPS, HEAD_GROUP * V_HEAD_DIM, D_MODEL), lambda i: (0, 0, 0)),
            pl.BlockSpec((D_MODEL, LANE), lambda i: (0, 0)),
        ],
        out_specs=[
            pl.BlockSpec((tm, D_MODEL), lambda i: (i, 0)),
            pl.BlockSpec((tm, D_MODEL), lambda i: (i, 0)),
            pl.BlockSpec((tm, LANE), lambda i: (i, 0)),
            pl.BlockSpec((ROUTE_ROWS, tm), lambda i: (0, i)),
            pl.BlockSpec((per, SUBLANE, LANE), lambda i: (i, 0, 0)),
            pl.BlockSpec((SUBLANE, LANE), lambda i: (0, 0)),
        ],
        out_shape=[
            jax.ShapeDtypeStruct((N_TOK, D_MODEL), F32),
            jax.ShapeDtypeStruct((N_TOK, D_MODEL), BF16),
            jax.ShapeDtypeStruct((N_TOK, LANE), F32),
            jax.ShapeDtypeStruct((ROUTE_ROWS, N_TOK), F32),
            jax.ShapeDtypeStruct((N_CHUNKS, SUBLANE, LANE), F32),
            jax.ShapeDtypeStruct((SUBLANE, LANE), F32),
        ],
        scratch_shapes=[pltpu.VMEM((SUBLANE, LANE), F32)],
        compiler_params=_cparams("arbitrary"),
        name="attn_out",
    )(o_p, o_s, x, mods, g, wo, wr)


def _routing_tables(route_t, cstart, total):
    counts = total[0, :N_EXPERTS].astype(I32)
    padded = (counts + SLOT_TILE - 1) // SLOT_TILE * SLOT_TILE
    ends = jnp.cumsum(padded)
    offs = ends - padded
    expert_ids = jnp.arange(N_EXPERTS, dtype=I32)[:, None]

    def region_start(e_row):
        return jnp.sum(jnp.where(e_row[None, :].astype(I32) == expert_ids, offs[:, None], 0), axis=0)

    slot1 = region_start(route_t[ROUTE_E1]) + route_t[ROUTE_RANK1].astype(I32)
    slot2 = region_start(route_t[ROUTE_E2]) + route_t[ROUTE_RANK2].astype(I32)

    n_active = ends[-1] // SLOT_TILE
    tile_start = jnp.arange(N_SLOT_TILES, dtype=I32) * SLOT_TILE
    tile_expert = jnp.sum(tile_start[:, None] >= ends[None, :], axis=1).astype(I32)
    last_expert = jnp.sum((n_active - 1) * SLOT_TILE >= ends).astype(I32)
    tile_active = tile_start < ends[-1]
    tile_expert = jnp.where(tile_active, tile_expert, last_expert)
    prev_expert = jnp.concatenate([jnp.full((1,), -1, I32), tile_expert[:-1]])
    tile_first = (tile_active & (tile_expert != prev_expert)).astype(I32)
    weight_slot = ((jnp.cumsum(tile_first) - 1) % 2).astype(I32)
    later = (expert_ids.T > expert_ids) & (counts > 0)[None, :]
    next_of = jnp.min(jnp.where(later, expert_ids.T, N_EXPERTS), axis=1)
    next_of = jnp.where(next_of == N_EXPERTS, -1, next_of).astype(I32)
    next_expert = jnp.sum(jnp.where(tile_expert[:, None] == expert_ids.T, next_of[None, :], 0), axis=1).astype(I32)

    cc = jnp.concatenate([cstart[:, 0, :N_EXPERTS], total[0:1, :N_EXPERTS]]).astype(I32)

    g_start = jnp.arange(N_SLOTS // GATHER_TILE, dtype=I32) * GATHER_TILE
    g_expert = jnp.minimum(jnp.sum(g_start[:, None] >= ends[None, :], axis=1), N_EXPERTS - 1).astype(I32)
    g_hot = (g_expert[None, :] == expert_ids).astype(I32)
    rank0 = g_start - jnp.sum(g_hot * offs[:, None], axis=0)
    cc_tile = jnp.sum(cc[:, :, None] * g_hot[None, :, :], axis=1)
    c_lo = jnp.sum(cc_tile[1:] <= rank0[None, :], axis=0).astype(I32)
    rank_end = jnp.minimum(rank0 + GATHER_TILE, jnp.sum(g_hot * counts[:, None], axis=0))
    c_hi = jnp.sum(cc_tile[:-1] < rank_end[None, :], axis=0).astype(I32) - 1
    idle = (g_start >= ends[-1]) | (c_hi < c_lo)
    c_lo = jnp.where(idle, 1, c_lo)
    c_hi = jnp.where(idle, 0, c_hi)

    lo = offs[None, :] + cc[:-1]
    hi = offs[None, :] + cc[1:]
    first, last = lo // SLOT_CHUNK, (hi - 1) // SLOT_CHUNK
    ids = jnp.concatenate([first, last], axis=1)
    valid = jnp.concatenate([hi > lo, (hi > lo) & (last != first)], axis=1)
    experts = jnp.tile(jnp.arange(N_EXPERTS, dtype=I32), (N_CHUNKS, 2))
    dest = jnp.cumsum(valid, axis=1) - 1
    place = (valid[:, :, None] & (dest[:, :, None] == jnp.arange(MAX_PAIRS, dtype=I32)[None, None, :])).astype(I32)
    ids = jnp.sum(ids[:, :, None] * place, axis=1).astype(I32)
    experts = jnp.sum(experts[:, :, None] * place, axis=1).astype(I32)
    n_pairs = jnp.sum(valid, axis=1).astype(I32)
    unused = jnp.arange(MAX_PAIRS, dtype=I32)[None, :] >= n_pairs[:, None]
    ids = jnp.where(unused, ids[:, 0:1], ids)
    experts = jnp.where(unused, -1, experts)
    n_pairs = n_pairs + n_pairs % 2

    pad = jnp.zeros((N_CHUNKS, SUBLANE - TOP_K, TOK_CHUNK), I32)
    slots_lane = jnp.concatenate([slot1.reshape(N_CHUNKS, 1, TOK_CHUNK), slot2.reshape(N_CHUNKS, 1, TOK_CHUNK), pad],
                                 axis=1)
    offs_row = jnp.concatenate([offs.astype(F32), jnp.zeros((LANE - N_EXPERTS,), F32)]).reshape(1, LANE)
    return dict(tile_expert=tile_expert, n_active=n_active.reshape(1).astype(I32), tile_first=tile_first,
                weight_slot=weight_slot, next_expert=next_expert, c_lo=c_lo, c_hi=c_hi,
                ids=ids.reshape(-1), experts=experts.reshape(-1), n_pairs=n_pairs,
                slots_lane=slots_lane, offs_row=offs_row)


GATHER_TILE = 256
GATHER_UNROLL = 3


def _gather_kernel(clo_ref, chi_ref, slots_ref, h_ref, o_ref, acc_ref):
    g = pl.program_id(0)
    slot_id = g * GATHER_TILE + lax.broadcasted_iota(I32, (GATHER_TILE, TOK_CHUNK), 0)
    acc_ref[...] = jnp.zeros_like(acc_ref)

    c_lo, c_hi = clo_ref[g], chi_ref[g]

    def one_hot(c, value):
        sl = slots_ref[c]
        hit = (sl[0:1, :] == slot_id) | (sl[1:2, :] == slot_id)
        return jnp.where(hit, value, 0.0).astype(BF16)

    def rows(c):
        return h_ref[pl.ds(pl.multiple_of(c * TOK_CHUNK, TOK_CHUNK), TOK_CHUNK), :]

    def body(t, carry):
        c0 = c_lo + GATHER_UNROLL * t
        total = _dot(one_hot(c0, 1.0), rows(c0))
        for k in range(1, GATHER_UNROLL):
            live = jnp.where(c0 + k <= c_hi, 1.0, 0.0)
            ck = jnp.minimum(c0 + k, c_hi)
            total += _dot(one_hot(ck, live), rows(ck))
        acc_ref[...] += total
        return carry

    lax.fori_loop(0, (c_hi - c_lo + GATHER_UNROLL) // GATHER_UNROLL, body, 0)
    o_ref[...] = acc_ref[...].astype(BF16)


def _gather(h, rt):
    return pl.pallas_call(
        _gather_kernel,
        grid_spec=pltpu.PrefetchScalarGridSpec(
            num_scalar_prefetch=2,
            grid=(N_SLOTS // GATHER_TILE,),
            in_specs=[
                pl.BlockSpec((N_CHUNKS, SUBLANE, TOK_CHUNK), lambda g, *_: (0, 0, 0)),
                pl.BlockSpec((N_TOK, D_MODEL), lambda g, *_: (0, 0), pipeline_mode=pl.Buffered(1)),
            ],
            out_specs=pl.BlockSpec((GATHER_TILE, D_MODEL), lambda g, *_: (g, 0)),
            scratch_shapes=[pltpu.VMEM((GATHER_TILE, D_MODEL), F32)],
        ),
        out_shape=jax.ShapeDtypeStruct((N_SLOTS, D_MODEL), BF16),
        compiler_params=_cparams("arbitrary"),
        name="moe_gather",
    )(rt["c_lo"], rt["c_hi"], rt["slots_lane"], h)


def _experts_kernel(te_ref, na_ref, first_ref, nxt_ref, ws_ref, h_ref, w1_hbm, w3_hbm, w2_hbm, o_ref,
                    w1_buf, w3_buf, w2_buf, sem):
    g = pl.program_id(0)
    slot = ws_ref[g]

    def weight_copies(e, s):
        return [pltpu.make_async_copy(w1_hbm.at[e], w1_buf.at[s], sem.at[s, 0]),
                pltpu.make_async_copy(w3_hbm.at[e], w3_buf.at[s], sem.at[s, 1]),
                pltpu.make_async_copy(w2_hbm.at[e], w2_buf.at[s], sem.at[s, 2])]

    @pl.when(g == 0)
    def _():
        for cp in weight_copies(te_ref[0], 0):
            cp.start()

    @pl.when(first_ref[g] == 1)
    def _():
        for cp in weight_copies(te_ref[g], slot):
            cp.wait()

        @pl.when(nxt_ref[g] >= 0)
        def _():
            for cp in weight_copies(nxt_ref[g], 1 - slot):
                cp.start()

    @pl.when(g < na_ref[0])
    def _():
        h = h_ref[...].astype(w1_buf.dtype)
        t = _silu(_dot(h, w1_buf[slot])) * _dot(h, w3_buf[slot])
        o_ref[...] = _dot(t.astype(w2_buf.dtype), w2_buf[slot]).astype(BF16)

    @pl.when(g >= na_ref[0])
    def _():
        o_ref[...] = jnp.zeros_like(o_ref)


def _experts(hs, rt, w1, w3, w2):
    tile = lambda g, te, na, *_: (jnp.minimum(g, na[0] - 1), 0)
    return pl.pallas_call(
        _experts_kernel,
        grid_spec=pltpu.PrefetchScalarGridSpec(
            num_scalar_prefetch=5,
            grid=(N_SLOT_TILES,),
            in_specs=[
                pl.BlockSpec((SLOT_TILE, D_MODEL), tile),
                pl.BlockSpec(memory_space=pl.ANY),
                pl.BlockSpec(memory_space=pl.ANY),
                pl.BlockSpec(memory_space=pl.ANY),
            ],
            out_specs=pl.BlockSpec((SLOT_TILE, D_MODEL), lambda g, *_: (g, 0)),
            scratch_shapes=[
                pltpu.VMEM((2, D_MODEL, D_FF_EXPERT), w1.dtype),
                pltpu.VMEM((2, D_MODEL, D_FF_EXPERT), w3.dtype),
                pltpu.VMEM((2, D_FF_EXPERT, D_MODEL), w2.dtype),
                pltpu.SemaphoreType.DMA((2, 3)),
            ],
        ),
        out_shape=jax.ShapeDtypeStruct((N_SLOTS, D_MODEL), BF16),
        compiler_params=_cparams("arbitrary"),
        name="moe_experts",
    )(rt["tile_expert"], rt["n_active"], rt["tile_first"], rt["next_expert"], rt["weight_slot"], hs, w1, w3, w2)


def _combine_kernel(np_ref, ids_ref, ex_ref, route_ref, offs_ref, x_ref, mod_ref, y_hbm,
                    op_ref, os_ref, buf_ref, acc_ref, sem):
    c = pl.program_id(0)
    cur = c % 2

    def chunk_copy(step, j, half):
        chunk = ids_ref[step * MAX_PAIRS + j]
        src = y_hbm.at[pl.ds(pl.multiple_of(chunk * SLOT_CHUNK, SLOT_CHUNK), SLOT_CHUNK)]
        return pltpu.make_async_copy(src, buf_ref.at[half, j], sem.at[half, j])

    def start_all(step, half):
        def go(j, carry):
            chunk_copy(step, j, half).start()
            return carry

        lax.fori_loop(0, np_ref[step], go, 0)

    @pl.when(c == 0)
    def _():
        start_all(0, 0)

    @pl.when(c + 1 < pl.num_programs(0))
    def _():
        start_all(c + 1, 1 - cur)

    route = route_ref[...]
    lanef = lax.broadcasted_iota(I32, route.shape, 1).astype(F32)
    offs = offs_ref[...]

    def slot_of(e_lane, rank_lane):
        start = jnp.sum(jnp.where(lanef == route[:, e_lane:e_lane + 1], offs, 0.0), axis=-1, keepdims=True)
        return start + route[:, rank_lane:rank_lane + 1]

    col = lax.broadcasted_iota(I32, (TOK_CHUNK, SLOT_CHUNK), 1).astype(F32)
    s1 = slot_of(ROUTE_E1, ROUTE_RANK1) - col
    s2 = slot_of(ROUTE_E2, ROUTE_RANK2) - col
    acc_ref[...] = jnp.zeros_like(acc_ref)

    def part(j):
        pair = c * MAX_PAIRS + j
        base = (ids_ref[pair] * SLOT_CHUNK).astype(F32)
        hit = (s1 == base) | (s2 == base)
        rows = _dot(jnp.where(hit, 1.0, 0.0).astype(BF16), buf_ref[cur, j])
        return _lane_pick(route, ex_ref[pair]) * rows

    def body(t, carry):
        chunk_copy(c, 2 * t, cur).wait()
        chunk_copy(c, 2 * t + 1, cur).wait()
        acc_ref[...] += part(2 * t) + part(2 * t + 1)
        return carry

    lax.fori_loop(0, np_ref[c] // 2, body, 0)
    res = x_ref[...] + _mod_part(mod_ref[...], 5) * acc_ref[...]

    @pl.when(c * TOK_CHUNK < NP_TOK)
    def _():
        op_ref[...] = res

    @pl.when(c * TOK_CHUNK >= NP_TOK)
    def _():
        os_ref[...] = res


def _combine(ys, route, x, mods, rt, layer):
    tok = lambda c, *_: (c, 0)
    return pl.pallas_call(
        _combine_kernel,
        grid_spec=pltpu.PrefetchScalarGridSpec(
            num_scalar_prefetch=3,
            grid=(N_CHUNKS,),
            in_specs=[
                pl.BlockSpec((TOK_CHUNK, LANE), tok),
                pl.BlockSpec((1, LANE), lambda c, *_: (0, 0)),
                pl.BlockSpec((TOK_CHUNK, D_MODEL), tok),
                _mod_spec(layer, TOK_CHUNK),
                pl.BlockSpec(memory_space=pl.ANY),
            ],
            out_specs=_split_specs(TOK_CHUNK),
            scratch_shapes=[
                pltpu.VMEM((2, MAX_PAIRS, SLOT_CHUNK, D_MODEL), BF16),
                pltpu.VMEM((TOK_CHUNK, D_MODEL), F32),
                pltpu.SemaphoreType.DMA((2, MAX_PAIRS)),
            ],
        ),
        out_shape=[jax.ShapeDtypeStruct((NP_TOK, D_MODEL), F32), jax.ShapeDtypeStruct((NS_TOK, D_MODEL), F32)],
        compiler_params=_cparams("arbitrary"),
        name="moe_combine",
    )(rt["n_pairs"], rt["ids"], rt["experts"], route, rt["offs_row"], x, mods, ys)


def _rope_partner(t):
    half = AXIS_ROPE_DIM // 2
    s = t.shape[:-1]
    return t.reshape(s + (2, 2, half))[..., ::-1, :].reshape(s + (ROPE_DIM,))


def _rope_tables(n_tokens):
    rows = n_tokens // GRID_W
    row = np.repeat(np.arange(rows), GRID_W).astype(np.float32)
    col = np.tile(np.arange(GRID_W), rows).astype(np.float32)
    inv = (ROPE_BASE ** (-np.arange(0, AXIS_ROPE_DIM, 2, dtype=np.float32) / AXIS_ROPE_DIM)).astype(np.float32)
    ar, ac = row[:, None] * inv, col[:, None] * inv
    cos = np.concatenate([np.cos(ar), np.cos(ar), np.cos(ac), np.cos(ac)], axis=-1)
    sin = np.concatenate([-np.sin(ar), np.sin(ar), -np.sin(ac), np.sin(ac)], axis=-1)
    zeros = np.zeros_like(cos)
    return (jnp.asarray(np.concatenate([cos, zeros], axis=-1), F32),
            jnp.asarray(np.concatenate([sin, zeros], axis=-1), F32))


def _qk_gain_rows(g):
    z = jnp.zeros((ROPE_DIM,), F32)
    rows = jnp.stack([g[:QK_NOPE_DIM],
                      jnp.concatenate([g[QK_NOPE_DIM:], z]),
                      jnp.concatenate([_rope_partner(g[QK_NOPE_DIM:]), z])])
    return jnp.concatenate([rows, jnp.zeros((SUBLANE - 3, LANE), F32)])


def kernel(x_prompt, x_sample, c, cache_ckv, cache_krope, c_ctx, ada_w, ada_b, norm1_g, norm2_g, conv_pw1, conv_dw, conv_dw_b, conv_ln_g, conv_ln_b, conv_pw2, ffn_w1, ffn_w3, ffn_w2, mla_wdq, mla_q_norm_g, mla_wuq, mla_wdkv, mla_kv_norm_g, mla_wukv, mla_q_qk_g, mla_k_qk_g, mla_wo, moe_router, moe_w1, moe_w3, moe_w2):
    xp, xs = x_prompt.reshape(NP_TOK, D_MODEL), x_sample.reshape(NS_TOK, D_MODEL)
    cond16 = jnp.concatenate([c_ctx[None, :], c, jnp.zeros((MOD_ROWS - 1 - DEC_BATCH, D_MODEL), F32)])
    mods = _adaln(cond16, ada_w, ada_b).reshape(2, MOD_ROWS, 1, N_MOD * D_MODEL)
    vec = lambda a: a.reshape(1, -1)

    u = _glu(xp, xs, mods, vec(norm1_g[0]), conv_pw1[0].astype(BF16), 0)
    dw = jnp.concatenate([conv_dw[0], jnp.zeros((1, D_MODEL), F32)])
    x = _conv(u, xp, xs, mods, dw, vec(conv_dw_b[0]), vec(conv_ln_g[0]), vec(conv_ln_b[0]),
              conv_pw2[0].astype(BF16), 0)
    x = _ffn(x, mods, vec(norm2_g[0]), ffn_w1[0].astype(BF16), ffn_w3[0].astype(BF16),
             ffn_w2[0].astype(BF16), 0)

    wdkv = mla_wdkv[0]
    w_down = jnp.concatenate([mla_wdq[0], wdkv, _rope_partner(wdkv[:, KV_LORA_RANK:])], axis=1).astype(BF16)
    cq, ckv_b, kr2, new_ckv, new_krope = _mla_down(x, mods, vec(norm1_g[1]), w_down, vec(mla_q_norm_g[0]),
                                                   vec(mla_kv_norm_g[0]), 1)

    wuq = mla_wuq[0].reshape(Q_LORA_RANK, N_HEADS, QK_HEAD_DIM)
    wuq = jnp.concatenate([wuq, _rope_partner(wuq[..., QK_NOPE_DIM:])], axis=-1)
    wuq = wuq.transpose(1, 0, 2).astype(BF16)
    wukv = mla_wukv[0].reshape(KV_LORA_RANK, N_HEADS, HEAD_PAD).transpose(1, 0, 2).astype(BF16)
    gq, gk = _qk_gain_rows(mla_q_qk_g[0]), _qk_gain_rows(mla_k_qk_g[0])
    ckr = cache_krope[:, 0]
    cache = (cache_ckv[:, 0], jnp.concatenate([ckr, _rope_partner(ckr)], axis=-1))
    o_p = _attention(cq, ckv_b, kr2, None, wuq, wukv, gq, gk, None, tok0=0, n_tok=NP_TOK, seq=SEQ)
    o_s = _attention(cq, ckv_b, kr2, cache, wuq, wukv, gq, gk, _rope_tables(DEC_SEQ),
                     tok0=NP_TOK, n_tok=NS_TOK, seq=DEC_SEQ)

    wr = jnp.concatenate([moe_router[0], jnp.zeros((D_MODEL, LANE - N_EXPERTS), F32)], axis=1)
    wo = mla_wo[0].astype(BF16).reshape(N_HEAD_GROUPS, HEAD_GROUP * V_HEAD_DIM, D_MODEL)
    x, h, route, route_t, cstart, total = _attn_out(o_p, o_s, x, mods, vec(norm2_g[1]), wo, wr, 1)
    rt = _routing_tables(route_t, cstart, total)
    hs = _gather(h, rt)
    ys = _experts(hs, rt, moe_w1[0], moe_w3[0], moe_w2[0])
    yp, ysamp = _combine(ys, route, x, mods, rt, 1)

    return (yp.reshape(BATCH, SEQ, D_MODEL), ysamp.reshape(DEC_BATCH, DEC_SEQ, D_MODEL),
            new_ckv.reshape(BATCH, 1, SEQ, KV_LORA_RANK), new_krope.reshape(BATCH, 1, SEQ, ROPE_DIM))
```

```python
import functools

import jax
import jax.numpy as jnp
import numpy as np
from jax import lax
from jax.experimental import pallas as pl
from jax.experimental.pallas import tpu as pltpu

D_MODEL = 1024
BATCH = 32
SEQ = 256
DEC_BATCH = 8
DEC_SEQ = 1024
PAST_LEN = 512
GRID_W = 64
N_MOD = 6
CONV_WIDTH = 31
CONV_PAD = CONV_WIDTH // 2
N_HEADS = 16
QK_NOPE_DIM = 128
ROPE_DIM = 64
QK_HEAD_DIM = QK_NOPE_DIM + ROPE_DIM
V_HEAD_DIM = 128
Q_LORA_RANK = 512
KV_LORA_RANK = 256
AXIS_ROPE_DIM = ROPE_DIM // 2
ROPE_BASE = 10000.0
D_FF = 2816
N_EXPERTS = 8
TOP_K = 2
D_FF_EXPERT = 1536
EPS = 1e-6
F32 = jnp.float32
BF16 = jnp.bfloat16
I32 = jnp.int32

NP_TOK = BATCH * SEQ
NS_TOK = DEC_BATCH * DEC_SEQ
N_TOK = NP_TOK + NS_TOK
MOD_ROWS = 16
LANE = 128
SUBLANE = 8
HEAD_PAD = 2 * LANE
VMEM_LIMIT = 56 * 1024 * 1024

TOK_CHUNK = 256
N_CHUNKS = N_TOK // TOK_CHUNK
SLOT_TILE = 512
N_SLOT_TILES = (TOP_K * N_TOK + N_EXPERTS * (SLOT_TILE - 1)) // SLOT_TILE
N_SLOTS = N_SLOT_TILES * SLOT_TILE
SLOT_CHUNK = 256
MAX_PAIRS = 2 * N_EXPERTS


def _cparams(*sem):
    return pltpu.CompilerParams(dimension_semantics=sem, vmem_limit_bytes=VMEM_LIMIT)


def _mod_row(tile, tm):
    start = tile * tm
    return jnp.where(start < NP_TOK, 0, 1 + (start - NP_TOK) // DEC_SEQ)


def _mod_spec(layer, tm):
    return pl.BlockSpec((None, None, 1, N_MOD * D_MODEL),
                        lambda i, *_: (layer, _mod_row(i, tm), 0, 0))


def _split_specs(tm):
    n_p = NP_TOK // tm
    return [pl.BlockSpec((tm, D_MODEL), lambda i, *_: (jnp.minimum(i, n_p - 1), 0)),
            pl.BlockSpec((tm, D_MODEL), lambda i, *_: (jnp.maximum(i - n_p, 0), 0))]


def _pick_tokens(tm, xp_ref, xs_ref):
    return jnp.where(pl.program_id(0) * tm < NP_TOK, xp_ref[...], xs_ref[...])


def _mod_part(mod, k):
    return mod[:, k * D_MODEL:(k + 1) * D_MODEL]


def _rms(x, g):
    return x * lax.rsqrt(jnp.mean(x * x, axis=-1, keepdims=True) + EPS) * g


def _silu(x):
    return x * jax.nn.sigmoid(x)


def _split_bf16(x):
    hi = x.astype(BF16)
    lo = (x - hi.astype(F32)).astype(BF16)
    return hi, lo


def _dot(a, b):
    return jnp.dot(a, b, preferred_element_type=F32)


def _dot3(a, b):
    ah, al = _split_bf16(a)
    bh, bl = _split_bf16(b)
    return _dot(ah, bh) + (_dot(al, bh) + _dot(ah, bl))


def _lane_pick(x, idx):
    lane = lax.broadcasted_iota(I32, x.shape, 1)
    return jnp.sum(jnp.where(lane == idx, x, 0.0), axis=-1, keepdims=True)


def _adaln_kernel(cond_ref, w_ref, b_ref, o_ref):
    o_ref[...] = _dot3(_silu(cond_ref[...]), w_ref[...]) + b_ref[...]


def _adaln(cond16, ada_w, ada_b):
    depth = ada_w.shape[0]
    tn = 1536
    return pl.pallas_call(
        _adaln_kernel,
        grid=(depth, N_MOD * D_MODEL // tn),
        in_specs=[
            pl.BlockSpec((MOD_ROWS, D_MODEL), lambda l, j: (0, 0)),
            pl.BlockSpec((None, D_MODEL, tn), lambda l, j: (l, 0, j)),
            pl.BlockSpec((None, 1, tn), lambda l, j: (l, 0, j)),
        ],
        out_specs=pl.BlockSpec((None, MOD_ROWS, tn), lambda l, j: (l, 0, j)),
        out_shape=jax.ShapeDtypeStruct((depth, MOD_ROWS, N_MOD * D_MODEL), F32),
        compiler_params=_cparams("arbitrary", "arbitrary"),
        name="adaln",
    )(cond16, ada_w, ada_b.reshape(depth, 1, N_MOD * D_MODEL))


def _glu_kernel(xp_ref, xs_ref, mod_ref, g_ref, w_ref, u_ref):
    mod = mod_ref[...]
    x = _pick_tokens(u_ref.shape[0], xp_ref, xs_ref)
    h = _rms(x, g_ref[...]) * (1.0 + _mod_part(mod, 1)) + _mod_part(mod, 0)
    ag = _dot(h.astype(BF16), w_ref[...])
    u_ref[...] = ag[:, :D_MODEL] * jax.nn.sigmoid(ag[:, D_MODEL:])


def _glu(xp, xs, mods, g, pw1, layer):
    tm = 1024
    return pl.pallas_call(
        _glu_kernel,
        grid=(N_TOK // tm,),
        in_specs=_split_specs(tm) + [
            _mod_spec(layer, tm),
            pl.BlockSpec((1, D_MODEL), lambda i: (0, 0)),
            pl.BlockSpec((D_MODEL, 2 * D_MODEL), lambda i: (0, 0)),
        ],
        out_specs=pl.BlockSpec((tm, D_MODEL), lambda i: (i, 0)),
        out_shape=jax.ShapeDtypeStruct((N_TOK, D_MODEL), F32),
        compiler_params=_cparams("arbitrary"),
        name="glu",
    )(xp, xs, mods, g, pw1)


CONV_CHUNK = 256
CONV_HALO = 16
CONV_ROWS = 64
CONV_SHIFT_ROWS = CONV_CHUNK + (CONV_HALO - CONV_PAD + CONV_WIDTH - 1) // SUBLANE * SUBLANE


def _conv_kernel(uc_ref, up_ref, un_ref, dw_ref, dwb_ref, lng_ref, lnb_ref, w_ref, xp_ref, xs_ref, mod_ref,
                 o_ref, pad_ref, shift_ref, conv_ref):
    i = pl.program_id(0)
    start = i * CONV_CHUNK
    seq_len = jnp.where(start < NP_TOK, SEQ, DEC_SEQ)
    off = jnp.where(start < NP_TOK, start, start - NP_TOK) % seq_len
    prev_ok = off > 0
    next_ok = off + CONV_CHUNK < seq_len
    pad_ref[0:CONV_HALO, :] = jnp.where(prev_ok, up_ref[...], 0.0)
    pad_ref[CONV_HALO:CONV_HALO + CONV_CHUNK, :] = uc_ref[...]
    pad_ref[CONV_HALO + CONV_CHUNK:, :] = jnp.where(next_ok, un_ref[...], 0.0)

    base = CONV_HALO - CONV_PAD
    for b in range(1, SUBLANE):
        shift_ref[b - 1] = pad_ref[b:b + CONV_SHIFT_ROWS, :]
    for c in range(D_MODEL // LANE):
        cs = slice(c * LANE, (c + 1) * LANE)
        wcol = dw_ref[:, cs]
        bias = dwb_ref[:, cs]
        for r in range(CONV_CHUNK // CONV_ROWS):
            acc = jnp.broadcast_to(bias, (CONV_ROWS, LANE))
            for k in range(CONV_WIDTH):
                a, b = divmod(base + k, SUBLANE)
                lo = r * CONV_ROWS + SUBLANE * a
                src = pad_ref if b == 0 else shift_ref.at[b - 1]
                acc = acc + wcol[k:k + 1, :] * src[lo:lo + CONV_ROWS, cs]
            conv_ref[r * CONV_ROWS:(r + 1) * CONV_ROWS, cs] = acc

    t = conv_ref[...]
    mu = jnp.mean(t, axis=-1, keepdims=True)
    tc = t - mu
    y = tc * lax.rsqrt(jnp.mean(tc * tc, axis=-1, keepdims=True) + EPS) * lng_ref[...] + lnb_ref[...]
    res = _dot(_silu(y).astype(BF16), w_ref[...])
    o_ref[...] = _pick_tokens(CONV_CHUNK, xp_ref, xs_ref) + _mod_part(mod_ref[...], 2) * res


def _conv(u, xp, xs, mods, dw, dwb, lng, lnb, pw2, layer):
    n_chunks = N_TOK // CONV_CHUNK
    halo_per_chunk = CONV_CHUNK // CONV_HALO
    n_halo = N_TOK // CONV_HALO
    row = lambda i: (i, 0)
    const = lambda i: (0, 0)
    return pl.pallas_call(
        _conv_kernel,
        grid=(n_chunks,),
        in_specs=[
            pl.BlockSpec((CONV_CHUNK, D_MODEL), row),
            pl.BlockSpec((CONV_HALO, D_MODEL), lambda i: (jnp.maximum(i * halo_per_chunk - 1, 0), 0)),
            pl.BlockSpec((CONV_HALO, D_MODEL),
                         lambda i: (jnp.minimum((i + 1) * halo_per_chunk, n_halo - 1), 0)),
            pl.BlockSpec((CONV_WIDTH + 1, D_MODEL), const),
            pl.BlockSpec((1, D_MODEL), const),
            pl.BlockSpec((1, D_MODEL), const),
            pl.BlockSpec((1, D_MODEL), const),
            pl.BlockSpec((D_MODEL, D_MODEL), const),
        ] + _split_specs(CONV_CHUNK) + [
            _mod_spec(layer, CONV_CHUNK),
        ],
        out_specs=pl.BlockSpec((CONV_CHUNK, D_MODEL), row),
        out_shape=jax.ShapeDtypeStruct((N_TOK, D_MODEL), F32),
        scratch_shapes=[
            pltpu.VMEM((CONV_CHUNK + 2 * CONV_HALO, D_MODEL), F32),
            pltpu.VMEM((SUBLANE - 1, CONV_SHIFT_ROWS, D_MODEL), F32),
            pltpu.VMEM((CONV_CHUNK, D_MODEL), F32),
        ],
        compiler_params=_cparams("arbitrary"),
        name="conv",
    )(u, u, u, dw, dwb, lng, lnb, pw2, xp, xs, mods)


FFN_CHUNK = 256


def _ffn_kernel(x_ref, mod_ref, g_ref, w1_ref, w3_ref, w2_ref, o_ref):
    mod = mod_ref[...]
    x = x_ref[...]
    h = (_rms(x, g_ref[...]) * (1.0 + _mod_part(mod, 4)) + _mod_part(mod, 3)).astype(BF16)
    y = None
    for j in range(D_FF // FFN_CHUNK):
        cols = slice(j * FFN_CHUNK, (j + 1) * FFN_CHUNK)
        t = _silu(_dot(h, w1_ref[:, cols])) * _dot(h, w3_ref[:, cols])
        part = _dot(t.astype(BF16), w2_ref[cols, :])
        y = part if y is None else y + part
    o_ref[...] = x + _mod_part(mod, 5) * y


def _ffn(x, mods, g, w1, w3, w2, layer):
    tm = 512
    const = lambda i: (0, 0)
    resident = dict(pipeline_mode=pl.Buffered(1))
    return pl.pallas_call(
        _ffn_kernel,
        grid=(N_TOK // tm,),
        in_specs=[
            pl.BlockSpec((tm, D_MODEL), lambda i: (i, 0)),
            _mod_spec(layer, tm),
            pl.BlockSpec((1, D_MODEL), const),
            pl.BlockSpec((D_MODEL, D_FF), const, **resident),
            pl.BlockSpec((D_MODEL, D_FF), const, **resident),
            pl.BlockSpec((D_FF, D_MODEL), const, **resident),
        ],
        out_specs=pl.BlockSpec((tm, D_MODEL), lambda i: (i, 0)),
        out_shape=jax.ShapeDtypeStruct((N_TOK, D_MODEL), F32),
        compiler_params=_cparams("arbitrary"),
        name="ffn",
    )(x, mods, g, w1, w3, w2)


def _mla_down_kernel(x_ref, mod_ref, g_ref, w_ref, qg_ref, kvg_ref, cq_ref, ckvb_ref, kr_ref, new_ckv_ref, new_kr_ref):
    mod = mod_ref[...]
    h = _rms(x_ref[...], g_ref[...]) * (1.0 + _mod_part(mod, 1)) + _mod_part(mod, 0)
    d = _dot(h.astype(BF16), w_ref[...])
    cq_ref[...] = _rms(d[:, :Q_LORA_RANK], qg_ref[...]).astype(BF16)
    ckv = _rms(d[:, Q_LORA_RANK:Q_LORA_RANK + KV_LORA_RANK], kvg_ref[...])
    ckvb_ref[...] = ckv.astype(BF16)
    kr = d[:, Q_LORA_RANK + KV_LORA_RANK:]
    kr_ref[...] = kr

    @pl.when(pl.program_id(0) * x_ref.shape[0] < NP_TOK)
    def _():
        new_ckv_ref[...] = ckv
        new_kr_ref[...] = kr[:, :ROPE_DIM]


def _mla_down(x, mods, g, w_down, qg, kvg, layer):
    tm = 1024
    n_p = NP_TOK // tm
    n_down = Q_LORA_RANK + KV_LORA_RANK + LANE
    return pl.pallas_call(
        _mla_down_kernel,
        grid=(N_TOK // tm,),
        in_specs=[
            pl.BlockSpec((tm, D_MODEL), lambda i: (i, 0)),
            _mod_spec(layer, tm),
            pl.BlockSpec((1, D_MODEL), lambda i: (0, 0)),
            pl.BlockSpec((D_MODEL, n_down), lambda i: (0, 0)),
            pl.BlockSpec((1, Q_LORA_RANK), lambda i: (0, 0)),
            pl.BlockSpec((1, KV_LORA_RANK), lambda i: (0, 0)),
        ],
        out_specs=[
            pl.BlockSpec((tm, Q_LORA_RANK), lambda i: (i, 0)),
            pl.BlockSpec((tm, KV_LORA_RANK), lambda i: (i, 0)),
            pl.BlockSpec((tm, LANE), lambda i: (i, 0)),
            pl.BlockSpec((tm, KV_LORA_RANK), lambda i: (jnp.minimum(i, n_p - 1), 0)),
            pl.BlockSpec((tm, ROPE_DIM), lambda i: (jnp.minimum(i, n_p - 1), 0)),
        ],
        out_shape=[
            jax.ShapeDtypeStruct((N_TOK, Q_LORA_RANK), BF16),
            jax.ShapeDtypeStruct((N_TOK, KV_LORA_RANK), BF16),
            jax.ShapeDtypeStruct((N_TOK, LANE), F32),
            jax.ShapeDtypeStruct((NP_TOK, KV_LORA_RANK), F32),
            jax.ShapeDtypeStruct((NP_TOK, ROPE_DIM), F32),
        ],
        compiler_params=_cparams("arbitrary"),
        name="mla_down",
    )(x, mods, g, w_down, qg, kvg)


ATTN_BLOCK = 1024
ATTN_TQ = 256
HEAD_GROUP = 2
N_HEAD_GROUPS = N_HEADS // HEAD_GROUP
LOG2E = 1.4426950408889634
ATTN_VMEM_LIMIT = 60 * 1024 * 1024


def _attn_kernel(*refs, seq, n_cache, rope):
    it = iter(refs)
    n_src = 5 if n_cache else 3
    cur_refs = [next(it) for _ in range(n_src)]
    nxt_refs = [next(it) for _ in range(n_src)]
    wuq_ref, wukv_ref, gq_ref, gk_ref = next(it), next(it), next(it), next(it)
    if rope:
        cos_ref, sin_ref = next(it), next(it)
    o_ref, k_scr, v_scr, q_scr = next(it), next(it), next(it), next(it)

    inv_dim = 1.0 / QK_HEAD_DIM
    gq, gk = gq_ref[...], gk_ref[...]
    tables = (cos_ref[...], sin_ref[...]) if rope else None
    ones_new = jnp.ones((ATTN_BLOCK, LANE), BF16)
    ones_cache = jnp.ones((n_cache, LANE), BF16) if n_cache else None

    def latents(src_refs):
        vals = [r[...] for r in src_refs]
        if n_cache:
            vals[3] = vals[3].astype(BF16)
        return vals

    cur = latents(cur_refs)

    def normed(nope, rot2, g, tabs, out_scale):
        ssq = jnp.sum(nope * nope + 0.5 * (rot2 * rot2), axis=-1, keepdims=True)
        r = lax.rsqrt(ssq * inv_dim + EPS) * out_scale
        if tabs is None:
            rot = rot2 * g[1:2, :]
        else:
            rot = rot2 * (g[1:2, :] * tabs[0]) + pltpu.roll(rot2, ROPE_DIM, 1) * (g[2:3, :] * tabs[1])
        return jnp.concatenate([(nope * r * g[0:1, :]).astype(BF16), (rot * r).astype(BF16)], axis=1)

    def build(gi, slot, src):
        cq, ckv, kr = src[:3]
        for j in range(HEAD_GROUP):
            h = gi * HEAD_GROUP + j
            wukv = wukv_ref[h]
            kv = _dot(ckv, wukv)
            k_scr[slot, j, 0:ATTN_BLOCK, :] = normed(kv[:, :LANE], kr, gk, tables, 1.0)
            v_scr[slot, j, 0:ATTN_BLOCK, :] = jnp.concatenate([kv[:, LANE:].astype(BF16), ones_new], axis=1)
            if n_cache:
                kvc = _dot(src[3], wukv)
                k_scr[slot, j, ATTN_BLOCK:, :] = normed(kvc[:, :LANE], src[4], gk, None, 1.0)
                v_scr[slot, j, ATTN_BLOCK:, :] = jnp.concatenate([kvc[:, LANE:].astype(BF16), ones_cache], axis=1)
            q = _dot(cq, wuq_ref[h])
            q_scr[slot, j] = normed(q[:, :LANE], q[:, LANE:], gq, tables, QK_HEAD_DIM ** -0.5 * LOG2E)

    def attend(gi, slot):
        heads = []
        for j in range(HEAD_GROUP):
            outs = []
            for i in range(ATTN_BLOCK // ATTN_TQ):
                rows = slice(i * ATTN_TQ, (i + 1) * ATTN_TQ)
                keys = slice(None) if seq == ATTN_BLOCK else rows
                s = lax.dot_general(q_scr[slot, j, rows, :], k_scr[slot, j, keys, :], (((1,), (1,)), ((), ())),
                                    preferred_element_type=F32)
                p = jnp.exp2((s - jnp.max(s, axis=-1, keepdims=True)).astype(BF16))
                oe = _dot(p, v_scr[slot, j, keys, :])
                outs.append((oe[:, :LANE] / oe[:, LANE:]).astype(BF16))
            heads.append(jnp.concatenate(outs, axis=0))
        o_ref[gi] = jnp.concatenate(heads, axis=1)

    @pl.when(pl.program_id(0) == 0)
    def _():
        build(0, 0, cur)

    nxt = latents(nxt_refs)
    n_trips = N_HEAD_GROUPS // 2

    def two_groups(t, carry):
        g0 = 2 * t
        build(g0 + 1, 1, cur)
        attend(g0, 0)
        wraps = t == n_trips - 1
        build((g0 + 2) % N_HEAD_GROUPS, 0, [jnp.where(wraps, n, c) for n, c in zip(nxt, cur)])
        attend(g0 + 1, 1)
        return carry

    lax.fori_loop(0, n_trips, two_groups, 0)


def _attention(cq, ckv, kr2, cache, wuq, wukv, gq, gk, tables, *, tok0, n_tok, seq):
    assert seq in (ATTN_BLOCK, ATTN_TQ) and tok0 % ATTN_BLOCK == 0 and n_tok % ATTN_BLOCK == 0
    b0 = tok0 // ATTN_BLOCK
    n_cache = 0 if cache is None else cache[0].shape[1]
    assert n_cache == 0 or seq == ATTN_BLOCK
    rope = tables is not None
    n_blocks = n_tok // ATTN_BLOCK
    const2 = lambda b: (0, 0)
    const3 = lambda b: (0, 0, 0)
    in_specs, args = [], []
    resident = dict(pipeline_mode=pl.Buffered(1))
    for ahead in (0, 1):
        blk = lambda b, ahead=ahead: jnp.minimum(b + ahead, n_blocks - 1)
        mode = {}
        in_specs += [
            pl.BlockSpec((ATTN_BLOCK, Q_LORA_RANK), lambda b, blk=blk: (b0 + blk(b), 0), **mode),
            pl.BlockSpec((ATTN_BLOCK, KV_LORA_RANK), lambda b, blk=blk: (b0 + blk(b), 0), **mode),
            pl.BlockSpec((ATTN_BLOCK, LANE), lambda b, blk=blk: (b0 + blk(b), 0), **mode),
        ]
        args += [cq, ckv, kr2]
        if n_cache:
            in_specs += [pl.BlockSpec((None, n_cache, KV_LORA_RANK), lambda b, blk=blk: (blk(b), 0, 0), **mode),
                         pl.BlockSpec((None, n_cache, LANE), lambda b, blk=blk: (blk(b), 0, 0), **mode)]
            args += list(cache)
    in_specs += [
        pl.BlockSpec((N_HEADS, Q_LORA_RANK, HEAD_PAD), const3, **resident),
        pl.BlockSpec((N_HEADS, KV_LORA_RANK, HEAD_PAD), const3, **resident),
        pl.BlockSpec((SUBLANE, LANE), const2),
        pl.BlockSpec((SUBLANE, LANE), const2),
    ]
    args += [wuq, wukv, gq, gk]
    if rope:
        in_specs += [pl.BlockSpec((ATTN_BLOCK, LANE), const2, **resident),
                     pl.BlockSpec((ATTN_BLOCK, LANE), const2, **resident)]
        args += list(tables)
    return pl.pallas_call(
        functools.partial(_attn_kernel, seq=seq, n_cache=n_cache, rope=rope),
        grid=(n_blocks,),
        in_specs=in_specs,
        out_specs=pl.BlockSpec((N_HEAD_GROUPS, ATTN_BLOCK, HEAD_GROUP * V_HEAD_DIM), lambda b: (0, b, 0)),
        out_shape=jax.ShapeDtypeStruct((N_HEAD_GROUPS, n_tok, HEAD_GROUP * V_HEAD_DIM), BF16),
        scratch_shapes=[
            pltpu.VMEM((2, HEAD_GROUP, ATTN_BLOCK + n_cache, HEAD_PAD), BF16),
            pltpu.VMEM((2, HEAD_GROUP, ATTN_BLOCK + n_cache, HEAD_PAD), BF16),
            pltpu.VMEM((2, HEAD_GROUP, ATTN_BLOCK, HEAD_PAD), BF16),
        ],
        compiler_params=pltpu.CompilerParams(dimension_semantics=("arbitrary",), vmem_limit_bytes=ATTN_VMEM_LIMIT),
        name="attn_rope" if rope else "attn",
    )(*args)


ROUTE_TM = 4 * TOK_CHUNK
ROUTE_E1, ROUTE_E2, ROUTE_RANK1, ROUTE_RANK2 = N_EXPERTS, N_EXPERTS + 1, N_EXPERTS + 2, N_EXPERTS + 3
ROUTE_ROWS = 16


def _attn_out_kernel(op_ref, os_ref, x_ref, mod_ref, g_ref, wo_ref, wr_ref,
                     x3_ref, h_ref, route_ref, route_t_ref, cstart_ref, total_ref, carry_ref):
    i = pl.program_id(0)
    mod = mod_ref[...]
    is_prompt = i * x_ref.shape[0] < NP_TOK

    @pl.when(i == 0)
    def _():
        carry_ref[...] = jnp.zeros_like(carry_ref)

    wh, wl = _split_bf16(wr_ref[...])
    lane = lax.broadcasted_iota(I32, (TOK_CHUNK, LANE), 1)
    lanef = lane.astype(F32)
    neg = jnp.float32(-jnp.inf)
    r_id = lax.broadcasted_iota(I32, (TOK_CHUNK, TOK_CHUNK), 0)
    c_id = lax.broadcasted_iota(I32, (TOK_CHUNK, TOK_CHUNK), 1)
    tri = jnp.where(c_id < r_id, 1.0, 0.0).astype(BF16)
    seen = carry_ref[0:1, :]

    for k in range(x_ref.shape[0] // TOK_CHUNK):
        rows = slice(k * TOK_CHUNK, (k + 1) * TOK_CHUNK)
        att = _dot(jnp.where(is_prompt, op_ref[0, rows, :], os_ref[0, rows, :]), wo_ref[0])
        for gi in range(1, N_HEAD_GROUPS):
            att += _dot(jnp.where(is_prompt, op_ref[gi, rows, :], os_ref[gi, rows, :]), wo_ref[gi])
        x3 = x_ref[rows, :] + _mod_part(mod, 2) * att
        x3_ref[rows, :] = x3
        h = _rms(x3, g_ref[...]) * (1.0 + _mod_part(mod, 4)) + _mod_part(mod, 3)
        hb = h.astype(BF16)
        h_ref[rows, :] = hb
        hl = (h - hb.astype(F32)).astype(BF16)
        logits = _dot(hb, wh) + (_dot(hl, wh) + _dot(hb, wl))
        logits = jnp.where(lane < N_EXPERTS, logits, neg)
        v1 = jnp.max(logits, axis=-1, keepdims=True)
        i1 = jnp.min(jnp.where(logits == v1, lanef, float(LANE)), axis=-1, keepdims=True)
        rest = jnp.where(lanef == i1, neg, logits)
        v2 = jnp.max(rest, axis=-1, keepdims=True)
        i2 = jnp.min(jnp.where(rest == v2, lanef, float(LANE)), axis=-1, keepdims=True)
        e2 = jnp.exp(v2 - v1)
        w1 = 1.0 / (1.0 + e2)
        hot1, hot2 = lanef == i1, lanef == i2
        gates = jnp.where(hot1, w1, 0.0) + jnp.where(hot2, e2 * w1, 0.0)

        hot = jnp.where(hot1 | hot2, 1.0, 0.0)
        before = _dot(tri, hot.astype(BF16)) + seen
        rank1 = jnp.sum(jnp.where(hot1, before, 0.0), axis=-1, keepdims=True)
        rank2 = jnp.sum(jnp.where(hot2, before, 0.0), axis=-1, keepdims=True)
        route = jnp.where(lane == ROUTE_E1, i1, jnp.where(lane == ROUTE_E2, i2, jnp.where(
            lane == ROUTE_RANK1, rank1, jnp.where(lane == ROUTE_RANK2, rank2, gates))))
        route_ref[rows, :] = route
        route_t_ref[:, rows] = route.T[:ROUTE_ROWS, :]
        cstart_ref[k] = jnp.broadcast_to(seen, (SUBLANE, LANE))
        seen = before[TOK_CHUNK - 1:TOK_CHUNK, :] + hot[TOK_CHUNK - 1:TOK_CHUNK, :]

    carry_ref[...] = jnp.broadcast_to(seen, (SUBLANE, LANE))
    total_ref[...] = jnp.broadcast_to(seen, (SUBLANE, LANE))


def _attn_out(o_p, o_s, x, mods, g, wo, wr, layer):
    tm = ROUTE_TM
    per = tm // TOK_CHUNK
    n_p = NP_TOK // tm
    o_block = (N_HEAD_GROUPS, tm, HEAD_GROUP * V_HEAD_DIM)
    return pl.pallas_call(
        _attn_out_kernel,
        grid=(N_TOK // tm,),
        in_specs=[
            pl.BlockSpec(o_block, lambda i: (0, jnp.minimum(i, n_p - 1), 0)),
            pl.BlockSpec(o_block, lambda i: (0, jnp.maximum(i - n_p, 0), 0)),
            pl.BlockSpec((tm, D_MODEL), lambda i: (i, 0)),
            _mod_spec(layer, tm),
            pl.BlockSpec((1, D_MODEL), lambda i: (0, 0)),
            pl.BlockSpec((N_HEAD_GROUPS, HEAD_GROUP * V_HEAD_DIM, D_MODEL), lambda i: (0, 0, 0)),
            pl.BlockSpec((D_MODEL, LANE), lambda i: (0, 0)),
        ],
        out_specs=[
            pl.BlockSpec((tm, D_MODEL), lambda i: (i, 0)),
            pl.BlockSpec((tm, D_MODEL), lambda i: (i, 0)),
            pl.BlockSpec((tm, LANE), lambda i: (i, 0)),
            pl.BlockSpec((ROUTE_ROWS, tm), lambda i: (0, i)),
            pl.BlockSpec((per, SUBLANE, LANE), lambda i: (i, 0, 0)),
            pl.BlockSpec((SUBLANE, LANE), lambda i: (0, 0)),
        ],
        out_shape=[
            jax.ShapeDtypeStruct((N_TOK, D_MODEL), F32),
            jax.ShapeDtypeStruct((N_TOK, D_MODEL), BF16),
            jax.ShapeDtypeStruct((N_TOK, LANE), F32),
            jax.ShapeDtypeStruct((ROUTE_ROWS, N_TOK), F32),
            jax.ShapeDtypeStruct((N_CHUNKS, SUBLANE, LANE), F32),
            jax.ShapeDtypeStruct((SUBLANE, LANE), F32),
        ],
        scratch_shapes=[pltpu.VMEM((SUBLANE, LANE), F32)],
        compiler_params=_cparams("arbitrary"),
        name="attn_out",
    )(o_p, o_s, x, mods, g, wo, wr)


def _routing_tables(route_t, cstart, total):
    counts = total[0, :N_EXPERTS].astype(I32)
    padded = (counts + SLOT_TILE - 1) // SLOT_TILE * SLOT_TILE
    ends = jnp.cumsum(padded)
    offs = ends - padded
    expert_ids = jnp.arange(N_EXPERTS, dtype=I32)[:, None]

    def region_start(e_row):
        return jnp.sum(jnp.where(e_row[None, :].astype(I32) == expert_ids, offs[:, None], 0), axis=0)

    slot1 = region_start(route_t[ROUTE_E1]) + route_t[ROUTE_RANK1].astype(I32)
    slot2 = region_start(route_t[ROUTE_E2]) + route_t[ROUTE_RANK2].astype(I32)

    n_active = ends[-1] // SLOT_TILE
    tile_start = jnp.arange(N_SLOT_TILES, dtype=I32) * SLOT_TILE
    tile_expert = jnp.sum(tile_start[:, None] >= ends[None, :], axis=1).astype(I32)
    last_expert = jnp.sum((n_active - 1) * SLOT_TILE >= ends).astype(I32)
    tile_active = tile_start < ends[-1]
    tile_expert = jnp.where(tile_active, tile_expert, last_expert)
    prev_expert = jnp.concatenate([jnp.full((1,), -1, I32), tile_expert[:-1]])
    tile_first = (tile_active & (tile_expert != prev_expert)).astype(I32)
    weight_slot = ((jnp.cumsum(tile_first) - 1) % 2).astype(I32)
    later = (expert_ids.T > expert_ids) & (counts > 0)[None, :]
    next_of = jnp.min(jnp.where(later, expert_ids.T, N_EXPERTS), axis=1)
    next_of = jnp.where(next_of == N_EXPERTS, -1, next_of).astype(I32)
    next_expert = jnp.sum(jnp.where(tile_expert[:, None] == expert_ids.T, next_of[None, :], 0), axis=1).astype(I32)

    cc = jnp.concatenate([cstart[:, 0, :N_EXPERTS], total[0:1, :N_EXPERTS]]).astype(I32)

    g_start = jnp.arange(N_SLOTS // GATHER_TILE, dtype=I32) * GATHER_TILE
    g_expert = jnp.minimum(jnp.sum(g_start[:, None] >= ends[None, :], axis=1), N_EXPERTS - 1).astype(I32)
    g_hot = (g_expert[None, :] == expert_ids).astype(I32)
    rank0 = g_start - jnp.sum(g_hot * offs[:, None], axis=0)
    cc_tile = jnp.sum(cc[:, :, None] * g_hot[None, :, :], axis=1)
    c_lo = jnp.sum(cc_tile[1:] <= rank0[None, :], axis=0).astype(I32)
    rank_end = jnp.minimum(rank0 + GATHER_TILE, jnp.sum(g_hot * counts[:, None], axis=0))
    c_hi = jnp.sum(cc_tile[:-1] < rank_end[None, :], axis=0).astype(I32) - 1
    idle = (g_start >= ends[-1]) | (c_hi < c_lo)
    c_lo = jnp.where(idle, 1, c_lo)
    c_hi = jnp.where(idle, 0, c_hi)

    lo = offs[None, :] + cc[:-1]
    hi = offs[None, :] + cc[1:]
    first, last = lo // SLOT_CHUNK, (hi - 1) // SLOT_CHUNK
    ids = jnp.concatenate([first, last], axis=1)
    valid = jnp.concatenate([hi > lo, (hi > lo) & (last != first)], axis=1)
    experts = jnp.tile(jnp.arange(N_EXPERTS, dtype=I32), (N_CHUNKS, 2))
    dest = jnp.cumsum(valid, axis=1) - 1
    place = (valid[:, :, None] & (dest[:, :, None] == jnp.arange(MAX_PAIRS, dtype=I32)[None, None, :])).astype(I32)
    ids = jnp.sum(ids[:, :, None] * place, axis=1).astype(I32)
    experts = jnp.sum(experts[:, :, None] * place, axis=1).astype(I32)
    n_pairs = jnp.sum(valid, axis=1).astype(I32)
    unused = jnp.arange(MAX_PAIRS, dtype=I32)[None, :] >= n_pairs[:, None]
    ids = jnp.where(unused, ids[:, 0:1], ids)
    experts = jnp.where(unused, -1, experts)
    n_pairs = n_pairs + n_pairs % 2

    pad = jnp.zeros((N_CHUNKS, SUBLANE - TOP_K, TOK_CHUNK), I32)
    slots_lane = jnp.concatenate([slot1.reshape(N_CHUNKS, 1, TOK_CHUNK), slot2.reshape(N_CHUNKS, 1, TOK_CHUNK), pad],
                                 axis=1)
    offs_row = jnp.concatenate([offs.astype(F32), jnp.zeros((LANE - N_EXPERTS,), F32)]).reshape(1, LANE)
    return dict(tile_expert=tile_expert, n_active=n_active.reshape(1).astype(I32), tile_first=tile_first,
                weight_slot=weight_slot, next_expert=next_expert, c_lo=c_lo, c_hi=c_hi,
                ids=ids.reshape(-1), experts=experts.reshape(-1), n_pairs=n_pairs,
                slots_lane=slots_lane, offs_row=offs_row)


GATHER_TILE = 256
GATHER_UNROLL = 3


def _gather_kernel(clo_ref, chi_ref, slots_ref, h_ref, o_ref, acc_ref):
    g = pl.program_id(0)
    slot_id = g * GATHER_TILE + lax.broadcasted_iota(I32, (GATHER_TILE, TOK_CHUNK), 0)
    acc_ref[...] = jnp.zeros_like(acc_ref)

    c_lo, c_hi = clo_ref[g], chi_ref[g]

    def one_hot(c, value):
        sl = slots_ref[c]
        hit = (sl[0:1, :] == slot_id) | (sl[1:2, :] == slot_id)
        return jnp.where(hit, value, 0.0).astype(BF16)

    def rows(c):
        return h_ref[pl.ds(pl.multiple_of(c * TOK_CHUNK, TOK_CHUNK), TOK_CHUNK), :]

    def body(t, carry):
        c0 = c_lo + GATHER_UNROLL * t
        total = _dot(one_hot(c0, 1.0), rows(c0))
        for k in range(1, GATHER_UNROLL):
            live = jnp.where(c0 + k <= c_hi, 1.0, 0.0)
            ck = jnp.minimum(c0 + k, c_hi)
            total += _dot(one_hot(ck, live), rows(ck))
        acc_ref[...] += total
        return carry

    lax.fori_loop(0, (c_hi - c_lo + GATHER_UNROLL) // GATHER_UNROLL, body, 0)
    o_ref[...] = acc_ref[...].astype(BF16)


def _gather(h, rt):
    return pl.pallas_call(
        _gather_kernel,
        grid_spec=pltpu.PrefetchScalarGridSpec(
            num_scalar_prefetch=2,
            grid=(N_SLOTS // GATHER_TILE,),
            in_specs=[
                pl.BlockSpec((N_CHUNKS, SUBLANE, TOK_CHUNK), lambda g, *_: (0, 0, 0)),
                pl.BlockSpec((N_TOK, D_MODEL), lambda g, *_: (0, 0), pipeline_mode=pl.Buffered(1)),
            ],
            out_specs=pl.BlockSpec((GATHER_TILE, D_MODEL), lambda g, *_: (g, 0)),
            scratch_shapes=[pltpu.VMEM((GATHER_TILE, D_MODEL), F32)],
        ),
        out_shape=jax.ShapeDtypeStruct((N_SLOTS, D_MODEL), BF16),
        compiler_params=_cparams("arbitrary"),
        name="moe_gather",
    )(rt["c_lo"], rt["c_hi"], rt["slots_lane"], h)


def _experts_kernel(te_ref, na_ref, first_ref, nxt_ref, ws_ref, h_ref, w1_hbm, w3_hbm, w2_hbm, o_ref,
                    w1_buf, w3_buf, w2_buf, sem):
    g = pl.program_id(0)
    slot = ws_ref[g]

    def weight_copies(e, s):
        return [pltpu.make_async_copy(w1_hbm.at[e], w1_buf.at[s], sem.at[s, 0]),
                pltpu.make_async_copy(w3_hbm.at[e], w3_buf.at[s], sem.at[s, 1]),
                pltpu.make_async_copy(w2_hbm.at[e], w2_buf.at[s], sem.at[s, 2])]

    @pl.when(g == 0)
    def _():
        for cp in weight_copies(te_ref[0], 0):
            cp.start()

    @pl.when(first_ref[g] == 1)
    def _():
        for cp in weight_copies(te_ref[g], slot):
            cp.wait()

        @pl.when(nxt_ref[g] >= 0)
        def _():
            for cp in weight_copies(nxt_ref[g], 1 - slot):
                cp.start()

    @pl.when(g < na_ref[0])
    def _():
        h = h_ref[...].astype(w1_buf.dtype)
        t = _silu(_dot(h, w1_buf[slot])) * _dot(h, w3_buf[slot])
        o_ref[...] = _dot(t.astype(w2_buf.dtype), w2_buf[slot]).astype(BF16)

    @pl.when(g >= na_ref[0])
    def _():
        o_ref[...] = jnp.zeros_like(o_ref)


def _experts(hs, rt, w1, w3, w2):
    tile = lambda g, te, na, *_: (jnp.minimum(g, na[0] - 1), 0)
    return pl.pallas_call(
        _experts_kernel,
        grid_spec=pltpu.PrefetchScalarGridSpec(
            num_scalar_prefetch=5,
            grid=(N_SLOT_TILES,),
            in_specs=[
                pl.BlockSpec((SLOT_TILE, D_MODEL), tile),
                pl.BlockSpec(memory_space=pl.ANY),
                pl.BlockSpec(memory_space=pl.ANY),
                pl.BlockSpec(memory_space=pl.ANY),
            ],
            out_specs=pl.BlockSpec((SLOT_TILE, D_MODEL), lambda g, *_: (g, 0)),
            scratch_shapes=[
                pltpu.VMEM((2, D_MODEL, D_FF_EXPERT), w1.dtype),
                pltpu.VMEM((2, D_MODEL, D_FF_EXPERT), w3.dtype),
                pltpu.VMEM((2, D_FF_EXPERT, D_MODEL), w2.dtype),
                pltpu.SemaphoreType.DMA((2, 3)),
            ],
        ),
        out_shape=jax.ShapeDtypeStruct((N_SLOTS, D_MODEL), BF16),
        compiler_params=_cparams("arbitrary"),
        name="moe_experts",
    )(rt["tile_expert"], rt["n_active"], rt["tile_first"], rt["next_expert"], rt["weight_slot"], hs, w1, w3, w2)


def _combine_kernel(np_ref, ids_ref, ex_ref, route_ref, offs_ref, x_ref, mod_ref, y_hbm,
                    op_ref, os_ref, buf_ref, acc_ref, sem):
    c = pl.program_id(0)
    cur = c % 2

    def chunk_copy(step, j, half):
        chunk = ids_ref[step * MAX_PAIRS + j]
        src = y_hbm.at[pl.ds(pl.multiple_of(chunk * SLOT_CHUNK, SLOT_CHUNK), SLOT_CHUNK)]
        return pltpu.make_async_copy(src, buf_ref.at[half, j], sem.at[half, j])

    def start_all(step, half):
        def go(j, carry):
            chunk_copy(step, j, half).start()
            return carry

        lax.fori_loop(0, np_ref[step], go, 0)

    @pl.when(c == 0)
    def _():
        start_all(0, 0)

    @pl.when(c + 1 < pl.num_programs(0))
    def _():
        start_all(c + 1, 1 - cur)

    route = route_ref[...]
    lanef = lax.broadcasted_iota(I32, route.shape, 1).astype(F32)
    offs = offs_ref[...]

    def slot_of(e_lane, rank_lane):
        start = jnp.sum(jnp.where(lanef == route[:, e_lane:e_lane + 1], offs, 0.0), axis=-1, keepdims=True)
        return start + route[:, rank_lane:rank_lane + 1]

    col = lax.broadcasted_iota(I32, (TOK_CHUNK, SLOT_CHUNK), 1).astype(F32)
    s1 = slot_of(ROUTE_E1, ROUTE_RANK1) - col
    s2 = slot_of(ROUTE_E2, ROUTE_RANK2) - col
    acc_ref[...] = jnp.zeros_like(acc_ref)

    def part(j):
        pair = c * MAX_PAIRS + j
        base = (ids_ref[pair] * SLOT_CHUNK).astype(F32)
        hit = (s1 == base) | (s2 == base)
        rows = _dot(jnp.where(hit, 1.0, 0.0).astype(BF16), buf_ref[cur, j])
        return _lane_pick(route, ex_ref[pair]) * rows

    def body(t, carry):
        chunk_copy(c, 2 * t, cur).wait()
        chunk_copy(c, 2 * t + 1, cur).wait()
        acc_ref[...] += part(2 * t) + part(2 * t + 1)
        return carry

    lax.fori_loop(0, np_ref[c] // 2, body, 0)
    res = x_ref[...] + _mod_part(mod_ref[...], 5) * acc_ref[...]

    @pl.when(c * TOK_CHUNK < NP_TOK)
    def _():
        op_ref[...] = res

    @pl.when(c * TOK_CHUNK >= NP_TOK)
    def _():
        os_ref[...] = res


def _combine(ys, route, x, mods, rt, layer):
    tok = lambda c, *_: (c, 0)
    return pl.pallas_call(
        _combine_kernel,
        grid_spec=pltpu.PrefetchScalarGridSpec(
            num_scalar_prefetch=3,
            grid=(N_CHUNKS,),
            in_specs=[
                pl.BlockSpec((TOK_CHUNK, LANE), tok),
                pl.BlockSpec((1, LANE), lambda c, *_: (0, 0)),
                pl.BlockSpec((TOK_CHUNK, D_MODEL), tok),
                _mod_spec(layer, TOK_CHUNK),
                pl.BlockSpec(memory_space=pl.ANY),
            ],
            out_specs=_split_specs(TOK_CHUNK),
            scratch_shapes=[
                pltpu.VMEM((2, MAX_PAIRS, SLOT_CHUNK, D_MODEL), BF16),
                pltpu.VMEM((TOK_CHUNK, D_MODEL), F32),
                pltpu.SemaphoreType.DMA((2, MAX_PAIRS)),
            ],
        ),
        out_shape=[jax.ShapeDtypeStruct((NP_TOK, D_MODEL), F32), jax.ShapeDtypeStruct((NS_TOK, D_MODEL), F32)],
        compiler_params=_cparams("arbitrary"),
        name="moe_combine",
    )(rt["n_pairs"], rt["ids"], rt["experts"], route, rt["offs_row"], x, mods, ys)


def _rope_partner(t):
    half = AXIS_ROPE_DIM // 2
    s = t.shape[:-1]
    return t.reshape(s + (2, 2, half))[..., ::-1, :].reshape(s + (ROPE_DIM,))


def _rope_tables(n_tokens):
    rows = n_tokens // GRID_W
    row = np.repeat(np.arange(rows), GRID_W).astype(np.float32)
    col = np.tile(np.arange(GRID_W), rows).astype(np.float32)
    inv = (ROPE_BASE ** (-np.arange(0, AXIS_ROPE_DIM, 2, dtype=np.float32) / AXIS_ROPE_DIM)).astype(np.float32)
    ar, ac = row[:, None] * inv, col[:, None] * inv
    cos = np.concatenate([np.cos(ar), np.cos(ar), np.cos(ac), np.cos(ac)], axis=-1)
    sin = np.concatenate([-np.sin(ar), np.sin(ar), -np.sin(ac), np.sin(ac)], axis=-1)
    zeros = np.zeros_like(cos)
    return (jnp.asarray(np.concatenate([cos, zeros], axis=-1), F32),
            jnp.asarray(np.concatenate([sin, zeros], axis=-1), F32))


def _qk_gain_rows(g):
    z = jnp.zeros((ROPE_DIM,), F32)
    rows = jnp.stack([g[:QK_NOPE_DIM],
                      jnp.concatenate([g[QK_NOPE_DIM:], z]),
                      jnp.concatenate([_rope_partner(g[QK_NOPE_DIM:]), z])])
    return jnp.concatenate([rows, jnp.zeros((SUBLANE - 3, LANE), F32)])


def kernel(x_prompt, x_sample, c, cache_ckv, cache_krope, c_ctx, ada_w, ada_b, norm1_g, norm2_g, conv_pw1, conv_dw, conv_dw_b, conv_ln_g, conv_ln_b, conv_pw2, ffn_w1, ffn_w3, ffn_w2, mla_wdq, mla_q_norm_g, mla_wuq, mla_wdkv, mla_kv_norm_g, mla_wukv, mla_q_qk_g, mla_k_qk_g, mla_wo, moe_router, moe_w1, moe_w3, moe_w2):
    xp, xs = x_prompt.reshape(NP_TOK, D_MODEL), x_sample.reshape(NS_TOK, D_MODEL)
    cond16 = jnp.concatenate([c_ctx[None, :], c, jnp.zeros((MOD_ROWS - 1 - DEC_BATCH, D_MODEL), F32)])
    mods = _adaln(cond16, ada_w, ada_b).reshape(2, MOD_ROWS, 1, N_MOD * D_MODEL)
    vec = lambda a: a.reshape(1, -1)

    u = _glu(xp, xs, mods, vec(norm1_g[0]), conv_pw1[0].astype(BF16), 0)
    dw = jnp.concatenate([conv_dw[0], jnp.zeros((1, D_MODEL), F32)])
    x = _conv(u, xp, xs, mods, dw, vec(conv_dw_b[0]), vec(conv_ln_g[0]), vec(conv_ln_b[0]),
              conv_pw2[0].astype(BF16), 0)
    x = _ffn(x, mods, vec(norm2_g[0]), ffn_w1[0].astype(BF16), ffn_w3[0].astype(BF16),
             ffn_w2[0].astype(BF16), 0)

    wdkv = mla_wdkv[0]
    w_down = jnp.concatenate([mla_wdq[0], wdkv, _rope_partner(wdkv[:, KV_LORA_RANK:])], axis=1).astype(BF16)
    cq, ckv_b, kr2, new_ckv, new_krope = _mla_down(x, mods, vec(norm1_g[1]), w_down, vec(mla_q_norm_g[0]),
                                                   vec(mla_kv_norm_g[0]), 1)

    wuq = mla_wuq[0].reshape(Q_LORA_RANK, N_HEADS, QK_HEAD_DIM)
    wuq = jnp.concatenate([wuq, _rope_partner(wuq[..., QK_NOPE_DIM:])], axis=-1)
    wuq = wuq.transpose(1, 0, 2).astype(BF16)
    wukv = mla_wukv[0].reshape(KV_LORA_RANK, N_HEADS, HEAD_PAD).transpose(1, 0, 2).astype(BF16)
    gq, gk = _qk_gain_rows(mla_q_qk_g[0]), _qk_gain_rows(mla_k_qk_g[0])
    ckr = cache_krope[:, 0]
    cache = (cache_ckv[:, 0], jnp.concatenate([ckr, _rope_partner(ckr)], axis=-1))
    o_p = _attention(cq, ckv_b, kr2, None, wuq, wukv, gq, gk, None, tok0=0, n_tok=NP_TOK, seq=SEQ)
    o_s = _attention(cq, ckv_b, kr2, cache, wuq, wukv, gq, gk, _rope_tables(DEC_SEQ),
                     tok0=NP_TOK, n_tok=NS_TOK, seq=DEC_SEQ)

    wr = jnp.concatenate([moe_router[0], jnp.zeros((D_MODEL, LANE - N_EXPERTS), F32)], axis=1)
    wo = mla_wo[0].astype(BF16).reshape(N_HEAD_GROUPS, HEAD_GROUP * V_HEAD_DIM, D_MODEL)
    x, h, route, route_t, cstart, total = _attn_out(o_p, o_s, x, mods, vec(norm2_g[1]), wo, wr, 1)
    rt = _routing_tables(route_t, cstart, total)
    hs = _gather(h, rt)
    ys = _experts(hs, rt, moe_w1[0], moe_w3[0], moe_w2[0])
    yp, ysamp = _combine(ys, route, x, mods, rt, 1)

    return (yp.reshape(BATCH, SEQ, D_MODEL), ysamp.reshape(DEC_BATCH, DEC_SEQ, D_MODEL),
            new_ckv.reshape(BATCH, 1, SEQ, KV_LORA_RANK), new_krope.reshape(BATCH, 1, SEQ, ROPE_DIM))
```

```python
import functools

import jax
import jax.numpy as jnp
import numpy as np
from jax import lax
from jax.experimental import pallas as pl
from jax.experimental.pallas import tpu as pltpu

D_MODEL = 1024
BATCH = 32
SEQ = 256
DEC_BATCH = 8
DEC_SEQ = 1024
PAST_LEN = 512
GRID_W = 64
N_MOD = 6
CONV_WIDTH = 31
CONV_PAD = CONV_WIDTH // 2
N_HEADS = 16
QK_NOPE_DIM = 128
ROPE_DIM = 64
QK_HEAD_DIM = QK_NOPE_DIM + ROPE_DIM
V_HEAD_DIM = 128
Q_LORA_RANK = 512
KV_LORA_RANK = 256
AXIS_ROPE_DIM = ROPE_DIM // 2
ROPE_BASE = 10000.0
D_FF = 2816
N_EXPERTS = 8
TOP_K = 2
D_FF_EXPERT = 1536
EPS = 1e-6
F32 = jnp.float32
BF16 = jnp.bfloat16
I32 = jnp.int32

NP_TOK = BATCH * SEQ
NS_TOK = DEC_BATCH * DEC_SEQ
N_TOK = NP_TOK + NS_TOK
MOD_ROWS = 16
LANE = 128
SUBLANE = 8
HEAD_PAD = 2 * LANE
VMEM_LIMIT = 56 * 1024 * 1024

TOK_CHUNK = 256
N_CHUNKS = N_TOK // TOK_CHUNK
SLOT_TILE = 512
N_SLOT_TILES = (TOP_K * N_TOK + N_EXPERTS * (SLOT_TILE - 1)) // SLOT_TILE
N_SLOTS = N_SLOT_TILES * SLOT_TILE
SLOT_CHUNK = 256
MAX_PAIRS = 2 * N_EXPERTS


def _cparams(*sem):
    return pltpu.CompilerParams(dimension_semantics=sem, vmem_limit_bytes=VMEM_LIMIT)


def _mod_row(tile, tm):
    start = tile * tm
    return jnp.where(start < NP_TOK, 0, 1 + (start - NP_TOK) // DEC_SEQ)


def _mod_spec(layer, tm):
    return pl.BlockSpec((None, None, 1, N_MOD * D_MODEL),
                        lambda i, *_: (layer, _mod_row(i, tm), 0, 0))


def _split_specs(tm):
    n_p = NP_TOK // tm
    return [pl.BlockSpec((tm, D_MODEL), lambda i, *_: (jnp.minimum(i, n_p - 1), 0)),
            pl.BlockSpec((tm, D_MODEL), lambda i, *_: (jnp.maximum(i - n_p, 0), 0))]


def _pick_tokens(tm, xp_ref, xs_ref):
    return jnp.where(pl.program_id(0) * tm < NP_TOK, xp_ref[...], xs_ref[...])


def _mod_part(mod, k):
    return mod[:, k * D_MODEL:(k + 1) * D_MODEL]


def _rms(x, g):
    return x * lax.rsqrt(jnp.mean(x * x, axis=-1, keepdims=True) + EPS) * g


def _silu(x):
    return x * jax.nn.sigmoid(x)


def _split_bf16(x):
    hi = x.astype(BF16)
    lo = (x - hi.astype(F32)).astype(BF16)
    return hi, lo


def _dot(a, b):
    return jnp.dot(a, b, preferred_element_type=F32)


def _dot3(a, b):
    ah, al = _split_bf16(a)
    bh, bl = _split_bf16(b)
    return _dot(ah, bh) + (_dot(al, bh) + _dot(ah, bl))


def _lane_pick(x, idx):
    lane = lax.broadcasted_iota(I32, x.shape, 1)
    return jnp.sum(jnp.where(lane == idx, x, 0.0), axis=-1, keepdims=True)


def _adaln_kernel(cond_ref, w_ref, b_ref, o_ref):
    o_ref[...] = _dot3(_silu(cond_ref[...]), w_ref[...]) + b_ref[...]


def _adaln(cond16, ada_w, ada_b):
    depth = ada_w.shape[0]
    tn = 1536
    return pl.pallas_call(
        _adaln_kernel,
        grid=(depth, N_MOD * D_MODEL // tn),
        in_specs=[
            pl.BlockSpec((MOD_ROWS, D_MODEL), lambda l, j: (0, 0)),
            pl.BlockSpec((None, D_MODEL, tn), lambda l, j: (l, 0, j)),
            pl.BlockSpec((None, 1, tn), lambda l, j: (l, 0, j)),
        ],
        out_specs=pl.BlockSpec((None, MOD_ROWS, tn), lambda l, j: (l, 0, j)),
        out_shape=jax.ShapeDtypeStruct((depth, MOD_ROWS, N_MOD * D_MODEL), F32),
        compiler_params=_cparams("arbitrary", "arbitrary"),
        name="adaln",
    )(cond16, ada_w, ada_b.reshape(depth, 1, N_MOD * D_MODEL))


def _glu_kernel(xp_ref, xs_ref, mod_ref, g_ref, w_ref, u_ref):
    mod = mod_ref[...]
    x = _pick_tokens(u_ref.shape[0], xp_ref, xs_ref)
    h = _rms(x, g_ref[...]) * (1.0 + _mod_part(mod, 1)) + _mod_part(mod, 0)
    ag = _dot(h.astype(BF16), w_ref[...])
    u_ref[...] = ag[:, :D_MODEL] * jax.nn.sigmoid(ag[:, D_MODEL:])


def _glu(xp, xs, mods, g, pw1, layer):
    tm = 1024
    return pl.pallas_call(
        _glu_kernel,
        grid=(N_TOK // tm,),
        in_specs=_split_specs(tm) + [
            _mod_spec(layer, tm),
            pl.BlockSpec((1, D_MODEL), lambda i: (0, 0)),
            pl.BlockSpec((D_MODEL, 2 * D_MODEL), lambda i: (0, 0)),
        ],
        out_specs=pl.BlockSpec((tm, D_MODEL), lambda i: (i, 0)),
        out_shape=jax.ShapeDtypeStruct((N_TOK, D_MODEL), F32),
        compiler_params=_cparams("arbitrary"),
        name="glu",
    )(xp, xs, mods, g, pw1)


CONV_CHUNK = 256
CONV_HALO = 16
CONV_ROWS = 64
CONV_SHIFT_ROWS = CONV_CHUNK + (CONV_HALO - CONV_PAD + CONV_WIDTH - 1) // SUBLANE * SUBLANE


def _conv_kernel(uc_ref, up_ref, un_ref, dw_ref, dwb_ref, lng_ref, lnb_ref, w_ref, xp_ref, xs_ref, mod_ref,
                 o_ref, pad_ref, shift_ref, conv_ref):
    i = pl.program_id(0)
    start = i * CONV_CHUNK
    seq_len = jnp.where(start < NP_TOK, SEQ, DEC_SEQ)
    off = jnp.where(start < NP_TOK, start, start - NP_TOK) % seq_len
    prev_ok = off > 0
    next_ok = off + CONV_CHUNK < seq_len
    pad_ref[0:CONV_HALO, :] = jnp.where(prev_ok, up_ref[...], 0.0)
    pad_ref[CONV_HALO:CONV_HALO + CONV_CHUNK, :] = uc_ref[...]
    pad_ref[CONV_HALO + CONV_CHUNK:, :] = jnp.where(next_ok, un_ref[...], 0.0)

    base = CONV_HALO - CONV_PAD
    for b in range(1, SUBLANE):
        shift_ref[b - 1] = pad_ref[b:b + CONV_SHIFT_ROWS, :]
    for c in range(D_MODEL // LANE):
        cs = slice(c * LANE, (c + 1) * LANE)
        wcol = dw_ref[:, cs]
        bias = dwb_ref[:, cs]
        for r in range(CONV_CHUNK // CONV_ROWS):
            acc = jnp.broadcast_to(bias, (CONV_ROWS, LANE))
            for k in range(CONV_WIDTH):
                a, b = divmod(base + k, SUBLANE)
                lo = r * CONV_ROWS + SUBLANE * a
                src = pad_ref if b == 0 else shift_ref.at[b - 1]
                acc = acc + wcol[k:k + 1, :] * src[lo:lo + CONV_ROWS, cs]
            conv_ref[r * CONV_ROWS:(r + 1) * CONV_ROWS, cs] = acc

    t = conv_ref[...]
    mu = jnp.mean(t, axis=-1, keepdims=True)
    tc = t - mu
    y = tc * lax.rsqrt(jnp.mean(tc * tc, axis=-1, keepdims=True) + EPS) * lng_ref[...] + lnb_ref[...]
    res = _dot(_silu(y).astype(BF16), w_ref[...])
    o_ref[...] = _pick_tokens(CONV_CHUNK, xp_ref, xs_ref) + _mod_part(mod_ref[...], 2) * res


def _conv(u, xp, xs, mods, dw, dwb, lng, lnb, pw2, layer):
    n_chunks = N_TOK // CONV_CHUNK
    halo_per_chunk = CONV_CHUNK // CONV_HALO
    n_halo = N_TOK // CONV_HALO
    row = lambda i: (i, 0)
    const = lambda i: (0, 0)
    return pl.pallas_call(
        _conv_kernel,
        grid=(n_chunks,),
        in_specs=[
            pl.BlockSpec((CONV_CHUNK, D_MODEL), row),
            pl.BlockSpec((CONV_HALO, D_MODEL), lambda i: (jnp.maximum(i * halo_per_chunk - 1, 0), 0)),
            pl.BlockSpec((CONV_HALO, D_MODEL),
                         lambda i: (jnp.minimum((i + 1) * halo_per_chunk, n_halo - 1), 0)),
            pl.BlockSpec((CONV_WIDTH + 1, D_MODEL), const),
            pl.BlockSpec((1, D_MODEL), const),
            pl.BlockSpec((1, D_MODEL), const),
            pl.BlockSpec((1, D_MODEL), const),
            pl.BlockSpec((D_MODEL, D_MODEL), const),
        ] + _split_specs(CONV_CHUNK) + [
            _mod_spec(layer, CONV_CHUNK),
        ],
        out_specs=pl.BlockSpec((CONV_CHUNK, D_MODEL), row),
        out_shape=jax.ShapeDtypeStruct((N_TOK, D_MODEL), F32),
        scratch_shapes=[
            pltpu.VMEM((CONV_CHUNK + 2 * CONV_HALO, D_MODEL), F32),
            pltpu.VMEM((SUBLANE - 1, CONV_SHIFT_ROWS, D_MODEL), F32),
            pltpu.VMEM((CONV_CHUNK, D_MODEL), F32),
        ],
        compiler_params=_cparams("arbitrary"),
        name="conv",
    )(u, u, u, dw, dwb, lng, lnb, pw2, xp, xs, mods)


FFN_CHUNK = 256


def _ffn_kernel(x_ref, mod_ref, g_ref, w1_ref, w3_ref, w2_ref, o_ref):
    mod = mod_ref[...]
    x = x_ref[...]
    h = (_rms(x, g_ref[...]) * (1.0 + _mod_part(mod, 4)) + _mod_part(mod, 3)).astype(BF16)
    y = None
    for j in range(D_FF // FFN_CHUNK):
        cols = slice(j * FFN_CHUNK, (j + 1) * FFN_CHUNK)
        t = _silu(_dot(h, w1_ref[:, cols])) * _dot(h, w3_ref[:, cols])
        part = _dot(t.astype(BF16), w2_ref[cols, :])
        y = part if y is None else y + part
    o_ref[...] = x + _mod_part(mod, 5) * y


def _ffn(x, mods, g, w1, w3, w2, layer):
    tm = 512
    const = lambda i: (0, 0)
    resident = dict(pipeline_mode=pl.Buffered(1))
    return pl.pallas_call(
        _ffn_kernel,
        grid=(N_TOK // tm,),
        in_specs=[
            pl.BlockSpec((tm, D_MODEL), lambda i: (i, 0)),
            _mod_spec(layer, tm),
            pl.BlockSpec((1, D_MODEL), const),
            pl.BlockSpec((D_MODEL, D_FF), const, **resident),
            pl.BlockSpec((D_MODEL, D_FF), const, **resident),
            pl.BlockSpec((D_FF, D_MODEL), const, **resident),
        ],
        out_specs=pl.BlockSpec((tm, D_MODEL), lambda i: (i, 0)),
        out_shape=jax.ShapeDtypeStruct((N_TOK, D_MODEL), F32),
        compiler_params=_cparams("arbitrary"),
        name="ffn",
    )(x, mods, g, w1, w3, w2)


def _mla_down_kernel(x_ref, mod_ref, g_ref, w_ref, qg_ref, kvg_ref, cq_ref, ckvb_ref, kr_ref, new_ckv_ref, new_kr_ref):
    mod = mod_ref[...]
    h = _rms(x_ref[...], g_ref[...]) * (1.0 + _mod_part(mod, 1)) + _mod_part(mod, 0)
    d = _dot(h.astype(BF16), w_ref[...])
    cq_ref[...] = _rms(d[:, :Q_LORA_RANK], qg_ref[...]).astype(BF16)
    ckv = _rms(d[:, Q_LORA_RANK:Q_LORA_RANK + KV_LORA_RANK], kvg_ref[...])
    ckvb_ref[...] = ckv.astype(BF16)
    kr = d[:, Q_LORA_RANK + KV_LORA_RANK:]
    kr_ref[...] = kr

    @pl.when(pl.program_id(0) * x_ref.shape[0] < NP_TOK)
    def _():
        new_ckv_ref[...] = ckv
        new_kr_ref[...] = kr[:, :ROPE_DIM]


def _mla_down(x, mods, g, w_down, qg, kvg, layer):
    tm = 1024
    n_p = NP_TOK // tm
    n_down = Q_LORA_RANK + KV_LORA_RANK + LANE
    return pl.pallas_call(
        _mla_down_kernel,
        grid=(N_TOK // tm,),
        in_specs=[
            pl.BlockSpec((tm, D_MODEL), lambda i: (i, 0)),
            _mod_spec(layer, tm),
            pl.BlockSpec((1, D_MODEL), lambda i: (0, 0)),
            pl.BlockSpec((D_MODEL, n_down), lambda i: (0, 0)),
            pl.BlockSpec((1, Q_LORA_RANK), lambda i: (0, 0)),
            pl.BlockSpec((1, KV_LORA_RANK), lambda i: (0, 0)),
        ],
        out_specs=[
            pl.BlockSpec((tm, Q_LORA_RANK), lambda i: (i, 0)),
            pl.BlockSpec((tm, KV_LORA_RANK), lambda i: (i, 0)),
            pl.BlockSpec((tm, LANE), lambda i: (i, 0)),
            pl.BlockSpec((tm, KV_LORA_RANK), lambda i: (jnp.minimum(i, n_p - 1), 0)),
            pl.BlockSpec((tm, ROPE_DIM), lambda i: (jnp.minimum(i, n_p - 1), 0)),
        ],
        out_shape=[
            jax.ShapeDtypeStruct((N_TOK, Q_LORA_RANK), BF16),
            jax.ShapeDtypeStruct((N_TOK, KV_LORA_RANK), BF16),
            jax.ShapeDtypeStruct((N_TOK, LANE), F32),
            jax.ShapeDtypeStruct((NP_TOK, KV_LORA_RANK), F32),
            jax.ShapeDtypeStruct((NP_TOK, ROPE_DIM), F32),
        ],
        compiler_params=_cparams("arbitrary"),
        name="mla_down",
    )(x, mods, g, w_down, qg, kvg)


ATTN_BLOCK = 1024
ATTN_TQ = 256
HEAD_GROUP = 2
N_HEAD_GROUPS = N_HEADS // HEAD_GROUP
LOG2E = 1.4426950408889634
ATTN_VMEM_LIMIT = 60 * 1024 * 1024


def _attn_kernel(*refs, seq, n_cache, rope):
    it = iter(refs)
    n_src = 5 if n_cache else 3
    cur_refs = [next(it) for _ in range(n_src)]
    nxt_refs = [next(it) for _ in range(n_src)]
    wuq_ref, wukv_ref, gq_ref, gk_ref = next(it), next(it), next(it), next(it)
    if rope:
        cos_ref, sin_ref = next(it), next(it)
    o_ref, k_scr, v_scr, q_scr = next(it), next(it), next(it), next(it)

    inv_dim = 1.0 / QK_HEAD_DIM
    gq, gk = gq_ref[...], gk_ref[...]
    tables = (cos_ref[...], sin_ref[...]) if rope else None
    ones_new = jnp.ones((ATTN_BLOCK, LANE), BF16)
    ones_cache = jnp.ones((n_cache, LANE), BF16) if n_cache else None

    def latents(src_refs):
        vals = [r[...] for r in src_refs]
        if n_cache:
            vals[3] = vals[3].astype(BF16)
        return vals

    cur = latents(cur_refs)

    def normed(nope, rot2, g, tabs, out_scale):
        ssq = jnp.sum(nope * nope + 0.5 * (rot2 * rot2), axis=-1, keepdims=True)
        r = lax.rsqrt(ssq * inv_dim + EPS) * out_scale
        if tabs is None:
            rot = rot2 * g[1:2, :]
        else:
            rot = rot2 * (g[1:2, :] * tabs[0]) + pltpu.roll(rot2, ROPE_DIM, 1) * (g[2:3, :] * tabs[1])
        return jnp.concatenate([(nope * r * g[0:1, :]).astype(BF16), (rot * r).astype(BF16)], axis=1)

    def build(gi, slot, src):
        cq, ckv, kr = src[:3]
        for j in range(HEAD_GROUP):
            h = gi * HEAD_GROUP + j
            wukv = wukv_ref[h]
            kv = _dot(ckv, wukv)
            k_scr[slot, j, 0:ATTN_BLOCK, :] = normed(kv[:, :LANE], kr, gk, tables, 1.0)
            v_scr[slot, j, 0:ATTN_BLOCK, :] = jnp.concatenate([kv[:, LANE:].astype(BF16), ones_new], axis=1)
            if n_cache:
                kvc = _dot(src[3], wukv)
                k_scr[slot, j, ATTN_BLOCK:, :] = normed(kvc[:, :LANE], src[4], gk, None, 1.0)
                v_scr[slot, j, ATTN_BLOCK:, :] = jnp.concatenate([kvc[:, LANE:].astype(BF16), ones_cache], axis=1)
            q = _dot(cq, wuq_ref[h])
            q_scr[slot, j] = normed(q[:, :LANE], q[:, LANE:], gq, tables, QK_HEAD_DIM ** -0.5 * LOG2E)

    def attend(gi, slot):
        heads = []
        for j in range(HEAD_GROUP):
            outs = []
            for i in range(ATTN_BLOCK // ATTN_TQ):
                rows = slice(i * ATTN_TQ, (i + 1) * ATTN_TQ)
                keys = slice(None) if seq == ATTN_BLOCK else rows
                s = lax.dot_general(q_scr[slot, j, rows, :], k_scr[slot, j, keys, :], (((1,), (1,)), ((), ())),
                                    preferred_element_type=F32)
                p = jnp.exp2((s - jnp.max(s, axis=-1, keepdims=True)).astype(BF16))
                oe = _dot(p, v_scr[slot, j, keys, :])
                outs.append((oe[:, :LANE] / oe[:, LANE:]).astype(BF16))
            heads.append(jnp.concatenate(outs, axis=0))
        o_ref[gi] = jnp.concatenate(heads, axis=1)

    @pl.when(pl.program_id(0) == 0)
    def _():
        build(0, 0, cur)

    nxt = latents(nxt_refs)
    n_trips = N_HEAD_GROUPS // 2

    def two_groups(t, carry):
        g0 = 2 * t
        build(g0 + 1, 1, cur)
        attend(g0, 0)
        wraps = t == n_trips - 1
        build((g0 + 2) % N_HEAD_GROUPS, 0, [jnp.where(wraps, n, c) for n, c in zip(nxt, cur)])
        attend(g0 + 1, 1)
        return carry

    lax.fori_loop(0, n_trips, two_groups, 0)


def _attention(cq, ckv, kr2, cache, wuq, wukv, gq, gk, tables, *, tok0, n_tok, seq):
    assert seq in (ATTN_BLOCK, ATTN_TQ) and tok0 % ATTN_BLOCK == 0 and n_tok % ATTN_BLOCK == 0
    b0 = tok0 // ATTN_BLOCK
    n_cache = 0 if cache is None else cache[0].shape[1]
    assert n_cache == 0 or seq == ATTN_BLOCK
    rope = tables is not None
    n_blocks = n_tok // ATTN_BLOCK
    const2 = lambda b: (0, 0)
    const3 = lambda b: (0, 0, 0)
    in_specs, args = [], []
    resident = dict(pipeline_mode=pl.Buffered(1))
    for ahead in (0, 1):
        blk = lambda b, ahead=ahead: jnp.minimum(b + ahead, n_blocks - 1)
        mode = {}
        in_specs += [
            pl.BlockSpec((ATTN_BLOCK, Q_LORA_RANK), lambda b, blk=blk: (b0 + blk(b), 0), **mode),
            pl.BlockSpec((ATTN_BLOCK, KV_LORA_RANK), lambda b, blk=blk: (b0 + blk(b), 0), **mode),
            pl.BlockSpec((ATTN_BLOCK, LANE), lambda b, blk=blk: (b0 + blk(b), 0), **mode),
        ]
        args += [cq, ckv, kr2]
        if n_cache:
            in_specs += [pl.BlockSpec((None, n_cache, KV_LORA_RANK), lambda b, blk=blk: (blk(b), 0, 0), **mode),
                         pl.BlockSpec((None, n_cache, LANE), lambda b, blk=blk: (blk(b), 0, 0), **mode)]
            args += list(cache)
    in_specs += [
        pl.BlockSpec((N_HEADS, Q_LORA_RANK, HEAD_PAD), const3, **resident),
        pl.BlockSpec((N_HEADS, KV_LORA_RANK, HEAD_PAD), const3, **resident),
        pl.BlockSpec((SUBLANE, LANE), const2),
        pl.BlockSpec((SUBLANE, LANE), const2),
    ]
    args += [wuq, wukv, gq, gk]
    if rope:
        in_specs += [pl.BlockSpec((ATTN_BLOCK, LANE), const2, **resident),
                     pl.BlockSpec((ATTN_BLOCK, LANE), const2, **resident)]
        args += list(tables)
    return pl.pallas_call(
        functools.partial(_attn_kernel, seq=seq, n_cache=n_cache, rope=rope),
        grid=(n_blocks,),
        in_specs=in_specs,
        out_specs=pl.BlockSpec((N_HEAD_GROUPS, ATTN_BLOCK, HEAD_GROUP * V_HEAD_DIM), lambda b: (0, b, 0)),
        out_shape=jax.ShapeDtypeStruct((N_HEAD_GROUPS, n_tok, HEAD_GROUP * V_HEAD_DIM), BF16),
        scratch_shapes=[
            pltpu.VMEM((2, HEAD_GROUP, ATTN_BLOCK + n_cache, HEAD_PAD), BF16),
            pltpu.VMEM((2, HEAD_GROUP, ATTN_BLOCK + n_cache, HEAD_PAD), BF16),
            pltpu.VMEM((2, HEAD_GROUP, ATTN_BLOCK, HEAD_PAD), BF16),
        ],
        compiler_params=pltpu.CompilerParams(dimension_semantics=("arbitrary",), vmem_limit_bytes=ATTN_VMEM_LIMIT),
        name="attn_rope" if rope else "attn",
    )(*args)


ROUTE_TM = 4 * TOK_CHUNK
ROUTE_E1, ROUTE_E2, ROUTE_RANK1, ROUTE_RANK2 = N_EXPERTS, N_EXPERTS + 1, N_EXPERTS + 2, N_EXPERTS + 3
ROUTE_ROWS = 16


def _attn_out_kernel(op_ref, os_ref, x_ref, mod_ref, g_ref, wo_ref, wr_ref,
                     x3_ref, h_ref, route_ref, route_t_ref, cstart_ref, total_ref, carry_ref):
    i = pl.program_id(0)
    mod = mod_ref[...]
    is_prompt = i * x_ref.shape[0] < NP_TOK

    @pl.when(i == 0)
    def _():
        carry_ref[...] = jnp.zeros_like(carry_ref)

    wh, wl = _split_bf16(wr_ref[...])
    lane = lax.broadcasted_iota(I32, (TOK_CHUNK, LANE), 1)
    lanef = lane.astype(F32)
    neg = jnp.float32(-jnp.inf)
    r_id = lax.broadcasted_iota(I32, (TOK_CHUNK, TOK_CHUNK), 0)
    c_id = lax.broadcasted_iota(I32, (TOK_CHUNK, TOK_CHUNK), 1)
    tri = jnp.where(c_id < r_id, 1.0, 0.0).astype(BF16)
    seen = carry_ref[0:1, :]

    for k in range(x_ref.shape[0] // TOK_CHUNK):
        rows = slice(k * TOK_CHUNK, (k + 1) * TOK_CHUNK)
        att = _dot(jnp.where(is_prompt, op_ref[0, rows, :], os_ref[0, rows, :]), wo_ref[0])
        for gi in range(1, N_HEAD_GROUPS):
            att += _dot(jnp.where(is_prompt, op_ref[gi, rows, :], os_ref[gi, rows, :]), wo_ref[gi])
        x3 = x_ref[rows, :] + _mod_part(mod, 2) * att
        x3_ref[rows, :] = x3
        h = _rms(x3, g_ref[...]) * (1.0 + _mod_part(mod, 4)) + _mod_part(mod, 3)
        hb = h.astype(BF16)
        h_ref[rows, :] = hb
        hl = (h - hb.astype(F32)).astype(BF16)
        logits = _dot(hb, wh) + (_dot(hl, wh) + _dot(hb, wl))
        logits = jnp.where(lane < N_EXPERTS, logits, neg)
        v1 = jnp.max(logits, axis=-1, keepdims=True)
        i1 = jnp.min(jnp.where(logits == v1, lanef, float(LANE)), axis=-1, keepdims=True)
        rest = jnp.where(lanef == i1, neg, logits)
        v2 = jnp.max(rest, axis=-1, keepdims=True)
        i2 = jnp.min(jnp.where(rest == v2, lanef, float(LANE)), axis=-1, keepdims=True)
        e2 = jnp.exp(v2 - v1)
        w1 = 1.0 / (1.0 + e2)
        hot1, hot2 = lanef == i1, lanef == i2
        gates = jnp.where(hot1, w1, 0.0) + jnp.where(hot2, e2 * w1, 0.0)

        hot = jnp.where(hot1 | hot2, 1.0, 0.0)
        before = _dot(tri, hot.astype(BF16)) + seen
        rank1 = jnp.sum(jnp.where(hot1, before, 0.0), axis=-1, keepdims=True)
        rank2 = jnp.sum(jnp.where(hot2, before, 0.0), axis=-1, keepdims=True)
        route = jnp.where(lane == ROUTE_E1, i1, jnp.where(lane == ROUTE_E2, i2, jnp.where(
            lane == ROUTE_RANK1, rank1, jnp.where(lane == ROUTE_RANK2, rank2, gates))))
        route_ref[rows, :] = route
        route_t_ref[:, rows] = route.T[:ROUTE_ROWS, :]
        cstart_ref[k] = jnp.broadcast_to(seen, (SUBLANE, LANE))
        seen = before[TOK_CHUNK - 1:TOK_CHUNK, :] + hot[TOK_CHUNK - 1:TOK_CHUNK, :]

    carry_ref[...] = jnp.broadcast_to(seen, (SUBLANE, LANE))
    total_ref[...] = jnp.broadcast_to(seen, (SUBLANE, LANE))


def _attn_out(o_p, o_s, x, mods, g, wo, wr, layer):
    tm = ROUTE_TM
    per = tm // TOK_CHUNK
    n_p = NP_TOK // tm
    o_block = (N_HEAD_GROUPS, tm, HEAD_GROUP * V_HEAD_DIM)
    return pl.pallas_call(
        _attn_out_kernel,
        grid=(N_TOK // tm,),
        in_specs=[
            pl.BlockSpec(o_block, lambda i: (0, jnp.minimum(i, n_p - 1), 0)),
            pl.BlockSpec(o_block, lambda i: (0, jnp.maximum(i - n_p, 0), 0)),
            pl.BlockSpec((tm, D_MODEL), lambda i: (i, 0)),
            _mod_spec(layer, tm),
            pl.BlockSpec((1, D_MODEL), lambda i: (0, 0)),
            pl.BlockSpec((N_HEAD_GROUPS, HEAD_GROUP * V_HEAD_DIM, D_MODEL), lambda i: (0, 0, 0)),
            pl.BlockSpec((D_MODEL, LANE), lambda i: (0, 0)),
        ],
        out_specs=[
            pl.BlockSpec((tm, D_MODEL), lambda i: (i, 0)),
            pl.BlockSpec((tm, D_MODEL), lambda i: (i, 0)),
            pl.BlockSpec((tm, LANE), lambda i: (i, 0)),
            pl.BlockSpec((ROUTE_ROWS, tm), lambda i: (0, i)),
            pl.BlockSpec((per, SUBLANE, LANE), lambda i: (i, 0, 0)),
            pl.BlockSpec((SUBLANE, LANE), lambda i: (0, 0)),
        ],
        out_shape=[
            jax.ShapeDtypeStruct((N_TOK, D_MODEL), F32),
            jax.ShapeDtypeStruct((N_TOK, D_MODEL), BF16),
            jax.ShapeDtypeStruct((N_TOK, LANE), F32),
            jax.ShapeDtypeStruct((ROUTE_ROWS, N_TOK), F32),
            jax.ShapeDtypeStruct((N_CHUNKS, SUBLANE, LANE), F32),
            jax.ShapeDtypeStruct((SUBLANE, LANE), F32),
        ],
        scratch_shapes=[pltpu.VMEM((SUBLANE, LANE), F32)],
        compiler_params=_cparams("arbitrary"),
        name="attn_out",
    )(o_p, o_s, x, mods, g, wo, wr)


def _routing_tables(route_t, cstart, total):
    counts = total[0, :N_EXPERTS].astype(I32)
    padded = (counts + SLOT_TILE - 1) // SLOT_TILE * SLOT_TILE
    ends = jnp.cumsum(padded)
    offs = ends - padded
    expert_ids = jnp.arange(N_EXPERTS, dtype=I32)[:, None]

    def region_start(e_row):
        return jnp.sum(jnp.where(e_row[None, :].astype(I32) == expert_ids, offs[:, None], 0), axis=0)

    slot1 = region_start(route_t[ROUTE_E1]) + route_t[ROUTE_RANK1].astype(I32)
    slot2 = region_start(route_t[ROUTE_E2]) + route_t[ROUTE_RANK2].astype(I32)

    n_active = ends[-1] // SLOT_TILE
    tile_start = jnp.arange(N_SLOT_TILES, dtype=I32) * SLOT_TILE
    tile_expert = jnp.sum(tile_start[:, None] >= ends[None, :], axis=1).astype(I32)
    last_expert = jnp.sum((n_active - 1) * SLOT_TILE >= ends).astype(I32)
    tile_active = tile_start < ends[-1]
    tile_expert = jnp.where(tile_active, tile_expert, last_expert)
    prev_expert = jnp.concatenate([jnp.full((1,), -1, I32), tile_expert[:-1]])
    tile_first = (tile_active & (tile_expert != prev_expert)).astype(I32)
    weight_slot = ((jnp.cumsum(tile_first) - 1) % 2).astype(I32)
    later = (expert_ids.T > expert_ids) & (counts > 0)[None, :]
    next_of = jnp.min(jnp.where(later, expert_ids.T, N_EXPERTS), axis=1)
    next_of = jnp.where(next_of == N_EXPERTS, -1, next_of).astype(I32)
    next_expert = jnp.sum(jnp.where(tile_expert[:, None] == expert_ids.T, next_of[None, :], 0), axis=1).astype(I32)

    cc = jnp.concatenate([cstart[:, 0, :N_EXPERTS], total[0:1, :N_EXPERTS]]).astype(I32)

    g_start = jnp.arange(N_SLOTS // GATHER_TILE, dtype=I32) * GATHER_TILE
    g_expert = jnp.minimum(jnp.sum(g_start[:, None] >= ends[None, :], axis=1), N_EXPERTS - 1).astype(I32)
    g_hot = (g_expert[None, :] == expert_ids).astype(I32)
    rank0 = g_start - jnp.sum(g_hot * offs[:, None], axis=0)
    cc_tile = jnp.sum(cc[:, :, None] * g_hot[None, :, :], axis=1)
    c_lo = jnp.sum(cc_tile[1:] <= rank0[None, :], axis=0).astype(I32)
    rank_end = jnp.minimum(rank0 + GATHER_TILE, jnp.sum(g_hot * counts[:, None], axis=0))
    c_hi = jnp.sum(cc_tile[:-1] < rank_end[None, :], axis=0).astype(I32) - 1
    idle = (g_start >= ends[-1]) | (c_hi < c_lo)
    c_lo = jnp.where(idle, 1, c_lo)
    c_hi = jnp.where(idle, 0, c_hi)

    lo = offs[None, :] + cc[:-1]
    hi = offs[None, :] + cc[1:]
    first, last = lo // SLOT_CHUNK, (hi - 1) // SLOT_CHUNK
    ids = jnp.concatenate([first, last], axis=1)
    valid = jnp.concatenate([hi > lo, (hi > lo) & (last != first)], axis=1)
    experts = jnp.tile(jnp.arange(N_EXPERTS, dtype=I32), (N_CHUNKS, 2))
    dest = jnp.cumsum(valid, axis=1) - 1
    place = (valid[:, :, None] & (dest[:, :, None] == jnp.arange(MAX_PAIRS, dtype=I32)[None, None, :])).astype(I32)
    ids = jnp.sum(ids[:, :, None] * place, axis=1).astype(I32)
    experts = jnp.sum(experts[:, :, None] * place, axis=1).astype(I32)
    n_pairs = jnp.sum(valid, axis=1).astype(I32)
    unused = jnp.arange(MAX_PAIRS, dtype=I32)[None, :] >= n_pairs[:, None]
    ids = jnp.where(unused, ids[:, 0:1], ids)
    experts = jnp.where(unused, -1, experts)
    n_pairs = n_pairs + n_pairs % 2

    pad = jnp.zeros((N_CHUNKS, SUBLANE - TOP_K, TOK_CHUNK), I32)
    slots_lane = jnp.concatenate([slot1.reshape(N_CHUNKS, 1, TOK_CHUNK), slot2.reshape(N_CHUNKS, 1, TOK_CHUNK), pad],
                                 axis=1)
    offs_row = jnp.concatenate([offs.astype(F32), jnp.zeros((LANE - N_EXPERTS,), F32)]).reshape(1, LANE)
    return dict(tile_expert=tile_expert, n_active=n_active.reshape(1).astype(I32), tile_first=tile_first,
                weight_slot=weight_slot, next_expert=next_expert, c_lo=c_lo, c_hi=c_hi,
                ids=ids.reshape(-1), experts=experts.reshape(-1), n_pairs=n_pairs,
                slots_lane=slots_lane, offs_row=offs_row)


GATHER_TILE = 256
GATHER_UNROLL = 3


def _gather_kernel(clo_ref, chi_ref, slots_ref, h_ref, o_ref, acc_ref):
    g = pl.program_id(0)
    slot_id = g * GATHER_TILE + lax.broadcasted_iota(I32, (GATHER_TILE, TOK_CHUNK), 0)
    acc_ref[...] = jnp.zeros_like(acc_ref)

    c_lo, c_hi = clo_ref[g], chi_ref[g]

    def one_hot(c, value):
        sl = slots_ref[c]
        hit = (sl[0:1, :] == slot_id) | (sl[1:2, :] == slot_id)
        return jnp.where(hit, value, 0.0).astype(BF16)

    def rows(c):
        return h_ref[pl.ds(pl.multiple_of(c * TOK_CHUNK, TOK_CHUNK), TOK_CHUNK), :]

    def body(t, carry):
        c0 = c_lo + GATHER_UNROLL * t
        total = _dot(one_hot(c0, 1.0), rows(c0))
        for k in range(1, GATHER_UNROLL):
            live = jnp.where(c0 + k <= c_hi, 1.0, 0.0)
            ck = jnp.minimum(c0 + k, c_hi)
            total += _dot(one_hot(ck, live), rows(ck))
        acc_ref[...] += total
        return carry

    lax.fori_loop(0, (c_hi - c_lo + GATHER_UNROLL) // GATHER_UNROLL, body, 0)
    o_ref[...] = acc_ref[...].astype(BF16)


def _gather(h, rt):
    return pl.pallas_call(
        _gather_kernel,
        grid_spec=pltpu.PrefetchScalarGridSpec(
            num_scalar_prefetch=2,
            grid=(N_SLOTS // GATHER_TILE,),
            in_specs=[
                pl.BlockSpec((N_CHUNKS, SUBLANE, TOK_CHUNK), lambda g, *_: (0, 0, 0)),
                pl.BlockSpec((N_TOK, D_MODEL), lambda g, *_: (0, 0), pipeline_mode=pl.Buffered(1)),
            ],
            out_specs=pl.BlockSpec((GATHER_TILE, D_MODEL), lambda g, *_: (g, 0)),
            scratch_shapes=[pltpu.VMEM((GATHER_TILE, D_MODEL), F32)],
        ),
        out_shape=jax.ShapeDtypeStruct((N_SLOTS, D_MODEL), BF16),
        compiler_params=_cparams("arbitrary"),
        name="moe_gather",
    )(rt["c_lo"], rt["c_hi"], rt["slots_lane"], h)


def _experts_kernel(te_ref, na_ref, first_ref, nxt_ref, ws_ref, h_ref, w1_hbm, w3_hbm, w2_hbm, o_ref,
                    w1_buf, w3_buf, w2_buf, sem):
    g = pl.program_id(0)
    slot = ws_ref[g]

    def weight_copies(e, s):
        return [pltpu.make_async_copy(w1_hbm.at[e], w1_buf.at[s], sem.at[s, 0]),
                pltpu.make_async_copy(w3_hbm.at[e], w3_buf.at[s], sem.at[s, 1]),
                pltpu.make_async_copy(w2_hbm.at[e], w2_buf.at[s], sem.at[s, 2])]

    @pl.when(g == 0)
    def _():
        for cp in weight_copies(te_ref[0], 0):
            cp.start()

    @pl.when(first_ref[g] == 1)
    def _():
        for cp in weight_copies(te_ref[g], slot):
            cp.wait()

        @pl.when(nxt_ref[g] >= 0)
        def _():
            for cp in weight_copies(nxt_ref[g], 1 - slot):
                cp.start()

    @pl.when(g < na_ref[0])
    def _():
        h = h_ref[...].astype(w1_buf.dtype)
        y = None
        for j in range(D_FF_EXPERT // FFN_CHUNK):
            cols = slice(j * FFN_CHUNK, (j + 1) * FFN_CHUNK)
            t = _silu(_dot(h, w1_buf[slot, :, cols])) * _dot(h, w3_buf[slot, :, cols])
            part = _dot(t.astype(w2_buf.dtype), w2_buf[slot, cols, :])
            y = part if y is None else y + part
        o_ref[...] = y.astype(BF16)

    @pl.when(g >= na_ref[0])
    def _():
        o_ref[...] = jnp.zeros_like(o_ref)


def _experts(hs, rt, w1, w3, w2):
    tile = lambda g, te, na, *_: (jnp.minimum(g, na[0] - 1), 0)
    return pl.pallas_call(
        _experts_kernel,
        grid_spec=pltpu.PrefetchScalarGridSpec(
            num_scalar_prefetch=5,
            grid=(N_SLOT_TILES,),
            in_specs=[
                pl.BlockSpec((SLOT_TILE, D_MODEL), tile),
                pl.BlockSpec(memory_space=pl.ANY),
                pl.BlockSpec(memory_space=pl.ANY),
                pl.BlockSpec(memory_space=pl.ANY),
            ],
            out_specs=pl.BlockSpec((SLOT_TILE, D_MODEL), lambda g, *_: (g, 0)),
            scratch_shapes=[
                pltpu.VMEM((2, D_MODEL, D_FF_EXPERT), w1.dtype),
                pltpu.VMEM((2, D_MODEL, D_FF_EXPERT), w3.dtype),
                pltpu.VMEM((2, D_FF_EXPERT, D_MODEL), w2.dtype),
                pltpu.SemaphoreType.DMA((2, 3)),
            ],
        ),
        out_shape=jax.ShapeDtypeStruct((N_SLOTS, D_MODEL), BF16),
        compiler_params=_cparams("arbitrary"),
        name="moe_experts",
    )(rt["tile_expert"], rt["n_active"], rt["tile_first"], rt["next_expert"], rt["weight_slot"], hs, w1, w3, w2)


def _combine_kernel(np_ref, ids_ref, ex_ref, route_ref, offs_ref, x_ref, mod_ref, y_hbm,
                    op_ref, os_ref, buf_ref, acc_ref, sem):
    c = pl.program_id(0)
    cur = c % 2

    def chunk_copy(step, j, half):
        chunk = ids_ref[step * MAX_PAIRS + j]
        src = y_hbm.at[pl.ds(pl.multiple_of(chunk * SLOT_CHUNK, SLOT_CHUNK), SLOT_CHUNK)]
        return pltpu.make_async_copy(src, buf_ref.at[half, j], sem.at[half, j])

    def start_all(step, half):
        def go(j, carry):
            chunk_copy(step, j, half).start()
            return carry

        lax.fori_loop(0, np_ref[step], go, 0)

    @pl.when(c == 0)
    def _():
        start_all(0, 0)

    @pl.when(c + 1 < pl.num_programs(0))
    def _():
        start_all(c + 1, 1 - cur)

    route = route_ref[...]
    lanef = lax.broadcasted_iota(I32, route.shape, 1).astype(F32)
    offs = offs_ref[...]

    def slot_of(e_lane, rank_lane):
        start = jnp.sum(jnp.where(lanef == route[:, e_lane:e_lane + 1], offs, 0.0), axis=-1, keepdims=True)
        return start + route[:, rank_lane:rank_lane + 1]

    col = lax.broadcasted_iota(I32, (TOK_CHUNK, SLOT_CHUNK), 1).astype(F32)
    s1 = slot_of(ROUTE_E1, ROUTE_RANK1) - col
    s2 = slot_of(ROUTE_E2, ROUTE_RANK2) - col
    acc_ref[...] = jnp.zeros_like(acc_ref)

    def part(j):
        pair = c * MAX_PAIRS + j
        base = (ids_ref[pair] * SLOT_CHUNK).astype(F32)
        hit = (s1 == base) | (s2 == base)
        rows = _dot(jnp.where(hit, 1.0, 0.0).astype(BF16), buf_ref[cur, j])
        return _lane_pick(route, ex_ref[pair]) * rows

    def body(t, carry):
        chunk_copy(c, 2 * t, cur).wait()
        chunk_copy(c, 2 * t + 1, cur).wait()
        acc_ref[...] += part(2 * t) + part(2 * t + 1)
        return carry

    lax.fori_loop(0, np_ref[c] // 2, body, 0)
    res = x_ref[...] + _mod_part(mod_ref[...], 5) * acc_ref[...]

    @pl.when(c * TOK_CHUNK < NP_TOK)
    def _():
        op_ref[...] = res

    @pl.when(c * TOK_CHUNK >= NP_TOK)
    def _():
        os_ref[...] = res


def _combine(ys, route, x, mods, rt, layer):
    tok = lambda c, *_: (c, 0)
    return pl.pallas_call(
        _combine_kernel,
        grid_spec=pltpu.PrefetchScalarGridSpec(
            num_scalar_prefetch=3,
            grid=(N_CHUNKS,),
            in_specs=[
                pl.BlockSpec((TOK_CHUNK, LANE), tok),
                pl.BlockSpec((1, LANE), lambda c, *_: (0, 0)),
                pl.BlockSpec((TOK_CHUNK, D_MODEL), tok),
                _mod_spec(layer, TOK_CHUNK),
                pl.BlockSpec(memory_space=pl.ANY),
            ],
            out_specs=_split_specs(TOK_CHUNK),
            scratch_shapes=[
                pltpu.VMEM((2, MAX_PAIRS, SLOT_CHUNK, D_MODEL), BF16),
                pltpu.VMEM((TOK_CHUNK, D_MODEL), F32),
                pltpu.SemaphoreType.DMA((2, MAX_PAIRS)),
            ],
        ),
        out_shape=[jax.ShapeDtypeStruct((NP_TOK, D_MODEL), F32), jax.ShapeDtypeStruct((NS_TOK, D_MODEL), F32)],
        compiler_params=_cparams("arbitrary"),
        name="moe_combine",
    )(rt["n_pairs"], rt["ids"], rt["experts"], route, rt["offs_row"], x, mods, ys)


def _rope_partner(t):
    half = AXIS_ROPE_DIM // 2
    s = t.shape[:-1]
    return t.reshape(s + (2, 2, half))[..., ::-1, :].reshape(s + (ROPE_DIM,))


def _rope_tables(n_tokens):
    rows = n_tokens // GRID_W
    row = np.repeat(np.arange(rows), GRID_W).astype(np.float32)
    col = np.tile(np.arange(GRID_W), rows).astype(np.float32)
    inv = (ROPE_BASE ** (-np.arange(0, AXIS_ROPE_DIM, 2, dtype=np.float32) / AXIS_ROPE_DIM)).astype(np.float32)
    ar, ac = row[:, None] * inv, col[:, None] * inv
    cos = np.concatenate([np.cos(ar), np.cos(ar), np.cos(ac), np.cos(ac)], axis=-1)
    sin = np.concatenate([-np.sin(ar), np.sin(ar), -np.sin(ac), np.sin(ac)], axis=-1)
    zeros = np.zeros_like(cos)
    return (jnp.asarray(np.concatenate([cos, zeros], axis=-1), F32),
            jnp.asarray(np.concatenate([sin, zeros], axis=-1), F32))


def _qk_gain_rows(g):
    z = jnp.zeros((ROPE_DIM,), F32)
    rows = jnp.stack([g[:QK_NOPE_DIM],
                      jnp.concatenate([g[QK_NOPE_DIM:], z]),
                      jnp.concatenate([_rope_partner(g[QK_NOPE_DIM:]), z])])
    return jnp.concatenate([rows, jnp.zeros((SUBLANE - 3, LANE), F32)])


def kernel(x_prompt, x_sample, c, cache_ckv, cache_krope, c_ctx, ada_w, ada_b, norm1_g, norm2_g, conv_pw1, conv_dw, conv_dw_b, conv_ln_g, conv_ln_b, conv_pw2, ffn_w1, ffn_w3, ffn_w2, mla_wdq, mla_q_norm_g, mla_wuq, mla_wdkv, mla_kv_norm_g, mla_wukv, mla_q_qk_g, mla_k_qk_g, mla_wo, moe_router, moe_w1, moe_w3, moe_w2):
    xp, xs = x_prompt.reshape(NP_TOK, D_MODEL), x_sample.reshape(NS_TOK, D_MODEL)
    cond16 = jnp.concatenate([c_ctx[None, :], c, jnp.zeros((MOD_ROWS - 1 - DEC_BATCH, D_MODEL), F32)])
    mods = _adaln(cond16, ada_w, ada_b).reshape(2, MOD_ROWS, 1, N_MOD * D_MODEL)
    vec = lambda a: a.reshape(1, -1)

    u = _glu(xp, xs, mods, vec(norm1_g[0]), conv_pw1[0].astype(BF16), 0)
    dw = jnp.concatenate([conv_dw[0], jnp.zeros((1, D_MODEL), F32)])
    x = _conv(u, xp, xs, mods, dw, vec(conv_dw_b[0]), vec(conv_ln_g[0]), vec(conv_ln_b[0]),
              conv_pw2[0].astype(BF16), 0)
    x = _ffn(x, mods, vec(norm2_g[0]), ffn_w1[0].astype(BF16), ffn_w3[0].astype(BF16),
             ffn_w2[0].astype(BF16), 0)

    wdkv = mla_wdkv[0]
    w_down = jnp.concatenate([mla_wdq[0], wdkv, _rope_partner(wdkv[:, KV_LORA_RANK:])], axis=1).astype(BF16)
    cq, ckv_b, kr2, new_ckv, new_krope = _mla_down(x, mods, vec(norm1_g[1]), w_down, vec(mla_q_norm_g[0]),
                                                   vec(mla_kv_norm_g[0]), 1)

    wuq = mla_wuq[0].reshape(Q_LORA_RANK, N_HEADS, QK_HEAD_DIM)
    wuq = jnp.concatenate([wuq, _rope_partner(wuq[..., QK_NOPE_DIM:])], axis=-1)
    wuq = wuq.transpose(1, 0, 2).astype(BF16)
    wukv = mla_wukv[0].reshape(KV_LORA_RANK, N_HEADS, HEAD_PAD).transpose(1, 0, 2).astype(BF16)
    gq, gk = _qk_gain_rows(mla_q_qk_g[0]), _qk_gain_rows(mla_k_qk_g[0])
    ckr = cache_krope[:, 0]
    cache = (cache_ckv[:, 0], jnp.concatenate([ckr, _rope_partner(ckr)], axis=-1))
    o_p = _attention(cq, ckv_b, kr2, None, wuq, wukv, gq, gk, None, tok0=0, n_tok=NP_TOK, seq=SEQ)
    o_s = _attention(cq, ckv_b, kr2, cache, wuq, wukv, gq, gk, _rope_tables(DEC_SEQ),
                     tok0=NP_TOK, n_tok=NS_TOK, seq=DEC_SEQ)

    wr = jnp.concatenate([moe_router[0], jnp.zeros((D_MODEL, LANE - N_EXPERTS), F32)], axis=1)
    wo = mla_wo[0].astype(BF16).reshape(N_HEAD_GROUPS, HEAD_GROUP * V_HEAD_DIM, D_MODEL)
    x, h, route, route_t, cstart, total = _attn_out(o_p, o_s, x, mods, vec(norm2_g[1]), wo, wr, 1)
    rt = _routing_tables(route_t, cstart, total)
    hs = _gather(h, rt)
    ys = _experts(hs, rt, moe_w1[0], moe_w3[0], moe_w2[0])
    yp, ysamp = _combine(ys, route, x, mods, rt, 1)

    return (yp.reshape(BATCH, SEQ, D_MODEL), ysamp.reshape(DEC_BATCH, DEC_SEQ, D_MODEL),
            new_ckv.reshape(BATCH, 1, SEQ, KV_LORA_RANK), new_krope.reshape(BATCH, 1, SEQ, ROPE_DIM))
```

```python
import functools

import jax
import jax.numpy as jnp
import numpy as np
from jax import lax
from jax.experimental import pallas as pl
from jax.experimental.pallas import tpu as pltpu

D_MODEL = 1024
BATCH = 32
SEQ = 256
DEC_BATCH = 8
DEC_SEQ = 1024
PAST_LEN = 512
GRID_W = 64
N_MOD = 6
CONV_WIDTH = 31
CONV_PAD = CONV_WIDTH // 2
N_HEADS = 16
QK_NOPE_DIM = 128
ROPE_DIM = 64
QK_HEAD_DIM = QK_NOPE_DIM + ROPE_DIM
V_HEAD_DIM = 128
Q_LORA_RANK = 512
KV_LORA_RANK = 256
AXIS_ROPE_DIM = ROPE_DIM // 2
ROPE_BASE = 10000.0
D_FF = 2816
N_EXPERTS = 8
TOP_K = 2
D_FF_EXPERT = 1536
EPS = 1e-6
F32 = jnp.float32
BF16 = jnp.bfloat16
I32 = jnp.int32

NP_TOK = BATCH * SEQ
NS_TOK = DEC_BATCH * DEC_SEQ
N_TOK = NP_TOK + NS_TOK
MOD_ROWS = 16
LANE = 128
SUBLANE = 8
HEAD_PAD = 2 * LANE
VMEM_LIMIT = 56 * 1024 * 1024

TOK_CHUNK = 256
N_CHUNKS = N_TOK // TOK_CHUNK
SLOT_TILE = 512
N_SLOT_TILES = (TOP_K * N_TOK + N_EXPERTS * (SLOT_TILE - 1)) // SLOT_TILE
N_SLOTS = N_SLOT_TILES * SLOT_TILE
SLOT_CHUNK = 256
MAX_PAIRS = 2 * N_EXPERTS


def _cparams(*sem):
    return pltpu.CompilerParams(dimension_semantics=sem, vmem_limit_bytes=VMEM_LIMIT)


def _mod_row(tile, tm):
    start = tile * tm
    return jnp.where(start < NP_TOK, 0, 1 + (start - NP_TOK) // DEC_SEQ)


def _mod_spec(layer, tm):
    return pl.BlockSpec((None, None, 1, N_MOD * D_MODEL),
                        lambda i, *_: (layer, _mod_row(i, tm), 0, 0))


def _split_specs(tm):
    n_p = NP_TOK // tm
    return [pl.BlockSpec((tm, D_MODEL), lambda i, *_: (jnp.minimum(i, n_p - 1), 0)),
            pl.BlockSpec((tm, D_MODEL), lambda i, *_: (jnp.maximum(i - n_p, 0), 0))]


def _pick_tokens(tm, xp_ref, xs_ref):
    return jnp.where(pl.program_id(0) * tm < NP_TOK, xp_ref[...], xs_ref[...])


def _mod_part(mod, k):
    return mod[:, k * D_MODEL:(k + 1) * D_MODEL]


def _rms(x, g):
    return x * lax.rsqrt(jnp.mean(x * x, axis=-1, keepdims=True) + EPS) * g


def _silu(x):
    return x * jax.nn.sigmoid(x)


def _split_bf16(x):
    hi = x.astype(BF16)
    lo = (x - hi.astype(F32)).astype(BF16)
    return hi, lo


def _dot(a, b):
    return jnp.dot(a, b, preferred_element_type=F32)


def _dot3(a, b):
    ah, al = _split_bf16(a)
    bh, bl = _split_bf16(b)
    return _dot(ah, bh) + (_dot(al, bh) + _dot(ah, bl))


def _lane_pick(x, idx):
    lane = lax.broadcasted_iota(I32, x.shape, 1)
    return jnp.sum(jnp.where(lane == idx, x, 0.0), axis=-1, keepdims=True)


def _adaln_kernel(cond_ref, w_ref, b_ref, o_ref):
    o_ref[...] = _dot3(_silu(cond_ref[...]), w_ref[...]) + b_ref[...]


def _adaln(cond16, ada_w, ada_b):
    depth = ada_w.shape[0]
    tn = 1536
    return pl.pallas_call(
        _adaln_kernel,
        grid=(depth, N_MOD * D_MODEL // tn),
        in_specs=[
            pl.BlockSpec((MOD_ROWS, D_MODEL), lambda l, j: (0, 0)),
            pl.BlockSpec((None, D_MODEL, tn), lambda l, j: (l, 0, j)),
            pl.BlockSpec((None, 1, tn), lambda l, j: (l, 0, j)),
        ],
        out_specs=pl.BlockSpec((None, MOD_ROWS, tn), lambda l, j: (l, 0, j)),
        out_shape=jax.ShapeDtypeStruct((depth, MOD_ROWS, N_MOD * D_MODEL), F32),
        compiler_params=_cparams("arbitrary", "arbitrary"),
        name="adaln",
    )(cond16, ada_w, ada_b.reshape(depth, 1, N_MOD * D_MODEL))


def _glu_kernel(xp_ref, xs_ref, mod_ref, g_ref, w_ref, u_ref):
    mod = mod_ref[...]
    x = _pick_tokens(u_ref.shape[0], xp_ref, xs_ref)
    h = _rms(x, g_ref[...]) * (1.0 + _mod_part(mod, 1)) + _mod_part(mod, 0)
    hb = h.astype(BF16)
    for j in range(D_MODEL // FFN_CHUNK):
        cols = slice(j * FFN_CHUNK, (j + 1) * FFN_CHUNK)
        gate_cols = slice(D_MODEL + j * FFN_CHUNK, D_MODEL + (j + 1) * FFN_CHUNK)
        u_ref[:, cols] = _dot(hb, w_ref[:, cols]) * jax.nn.sigmoid(_dot(hb, w_ref[:, gate_cols]))


def _glu(xp, xs, mods, g, pw1, layer):
    tm = 1024
    return pl.pallas_call(
        _glu_kernel,
        grid=(N_TOK // tm,),
        in_specs=_split_specs(tm) + [
            _mod_spec(layer, tm),
            pl.BlockSpec((1, D_MODEL), lambda i: (0, 0)),
            pl.BlockSpec((D_MODEL, 2 * D_MODEL), lambda i: (0, 0)),
        ],
        out_specs=pl.BlockSpec((tm, D_MODEL), lambda i: (i, 0)),
        out_shape=jax.ShapeDtypeStruct((N_TOK, D_MODEL), F32),
        compiler_params=_cparams("arbitrary"),
        name="glu",
    )(xp, xs, mods, g, pw1)


CONV_CHUNK = 256
CONV_HALO = 16
CONV_ROWS = 64
CONV_SHIFT_ROWS = CONV_CHUNK + (CONV_HALO - CONV_PAD + CONV_WIDTH - 1) // SUBLANE * SUBLANE


def _conv_kernel(uc_ref, up_ref, un_ref, dw_ref, dwb_ref, lng_ref, lnb_ref, w_ref, xp_ref, xs_ref, mod_ref,
                 o_ref, pad_ref, shift_ref, conv_ref):
    i = pl.program_id(0)
    start = i * CONV_CHUNK
    seq_len = jnp.where(start < NP_TOK, SEQ, DEC_SEQ)
    off = jnp.where(start < NP_TOK, start, start - NP_TOK) % seq_len
    prev_ok = off > 0
    next_ok = off + CONV_CHUNK < seq_len
    pad_ref[0:CONV_HALO, :] = jnp.where(prev_ok, up_ref[...], 0.0)
    pad_ref[CONV_HALO:CONV_HALO + CONV_CHUNK, :] = uc_ref[...]
    pad_ref[CONV_HALO + CONV_CHUNK:, :] = jnp.where(next_ok, un_ref[...], 0.0)

    base = CONV_HALO - CONV_PAD
    for b in range(1, SUBLANE):
        shift_ref[b - 1] = pad_ref[b:b + CONV_SHIFT_ROWS, :]
    for c in range(D_MODEL // LANE):
        cs = slice(c * LANE, (c + 1) * LANE)
        wcol = dw_ref[:, cs]
        bias = dwb_ref[:, cs]
        for r in range(CONV_CHUNK // CONV_ROWS):
            acc = jnp.broadcast_to(bias, (CONV_ROWS, LANE))
            for k in range(CONV_WIDTH):
                a, b = divmod(base + k, SUBLANE)
                lo = r * CONV_ROWS + SUBLANE * a
                src = pad_ref if b == 0 else shift_ref.at[b - 1]
                acc = acc + wcol[k:k + 1, :] * src[lo:lo + CONV_ROWS, cs]
            conv_ref[r * CONV_ROWS:(r + 1) * CONV_ROWS, cs] = acc

    t = conv_ref[...]
    mu = jnp.mean(t, axis=-1, keepdims=True)
    tc = t - mu
    y = tc * lax.rsqrt(jnp.mean(tc * tc, axis=-1, keepdims=True) + EPS) * lng_ref[...] + lnb_ref[...]
    res = _dot(_silu(y).astype(BF16), w_ref[...])
    o_ref[...] = _pick_tokens(CONV_CHUNK, xp_ref, xs_ref) + _mod_part(mod_ref[...], 2) * res


def _conv(u, xp, xs, mods, dw, dwb, lng, lnb, pw2, layer):
    n_chunks = N_TOK // CONV_CHUNK
    halo_per_chunk = CONV_CHUNK // CONV_HALO
    n_halo = N_TOK // CONV_HALO
    row = lambda i: (i, 0)
    const = lambda i: (0, 0)
    return pl.pallas_call(
        _conv_kernel,
        grid=(n_chunks,),
        in_specs=[
            pl.BlockSpec((CONV_CHUNK, D_MODEL), row),
            pl.BlockSpec((CONV_HALO, D_MODEL), lambda i: (jnp.maximum(i * halo_per_chunk - 1, 0), 0)),
            pl.BlockSpec((CONV_HALO, D_MODEL),
                         lambda i: (jnp.minimum((i + 1) * halo_per_chunk, n_halo - 1), 0)),
            pl.BlockSpec((CONV_WIDTH + 1, D_MODEL), const),
            pl.BlockSpec((1, D_MODEL), const),
            pl.BlockSpec((1, D_MODEL), const),
            pl.BlockSpec((1, D_MODEL), const),
            pl.BlockSpec((D_MODEL, D_MODEL), const),
        ] + _split_specs(CONV_CHUNK) + [
            _mod_spec(layer, CONV_CHUNK),
        ],
        out_specs=pl.BlockSpec((CONV_CHUNK, D_MODEL), row),
        out_shape=jax.ShapeDtypeStruct((N_TOK, D_MODEL), F32),
        scratch_shapes=[
            pltpu.VMEM((CONV_CHUNK + 2 * CONV_HALO, D_MODEL), F32),
            pltpu.VMEM((SUBLANE - 1, CONV_SHIFT_ROWS, D_MODEL), F32),
            pltpu.VMEM((CONV_CHUNK, D_MODEL), F32),
        ],
        compiler_params=_cparams("arbitrary"),
        name="conv",
    )(u, u, u, dw, dwb, lng, lnb, pw2, xp, xs, mods)


FFN_CHUNK = 256


def _ffn_kernel(x_ref, mod_ref, g_ref, w1_ref, w3_ref, w2_ref, o_ref):
    mod = mod_ref[...]
    x = x_ref[...]
    h = (_rms(x, g_ref[...]) * (1.0 + _mod_part(mod, 4)) + _mod_part(mod, 3)).astype(BF16)
    y = None
    for j in range(D_FF // FFN_CHUNK):
        cols = slice(j * FFN_CHUNK, (j + 1) * FFN_CHUNK)
        t = _silu(_dot(h, w1_ref[:, cols])) * _dot(h, w3_ref[:, cols])
        part = _dot(t.astype(BF16), w2_ref[cols, :])
        y = part if y is None else y + part
    o_ref[...] = x + _mod_part(mod, 5) * y


def _ffn(x, mods, g, w1, w3, w2, layer):
    tm = 512
    const = lambda i: (0, 0)
    resident = dict(pipeline_mode=pl.Buffered(1))
    return pl.pallas_call(
        _ffn_kernel,
        grid=(N_TOK // tm,),
        in_specs=[
            pl.BlockSpec((tm, D_MODEL), lambda i: (i, 0)),
            _mod_spec(layer, tm),
            pl.BlockSpec((1, D_MODEL), const),
            pl.BlockSpec((D_MODEL, D_FF), const, **resident),
            pl.BlockSpec((D_MODEL, D_FF), const, **resident),
            pl.BlockSpec((D_FF, D_MODEL), const, **resident),
        ],
        out_specs=pl.BlockSpec((tm, D_MODEL), lambda i: (i, 0)),
        out_shape=jax.ShapeDtypeStruct((N_TOK, D_MODEL), F32),
        compiler_params=_cparams("arbitrary"),
        name="ffn",
    )(x, mods, g, w1, w3, w2)


def _mla_down_kernel(x_ref, mod_ref, g_ref, w_ref, qg_ref, kvg_ref, cq_ref, ckvb_ref, kr_ref, new_ckv_ref, new_kr_ref):
    mod = mod_ref[...]
    h = _rms(x_ref[...], g_ref[...]) * (1.0 + _mod_part(mod, 1)) + _mod_part(mod, 0)
    d = _dot(h.astype(BF16), w_ref[...])
    cq_ref[...] = _rms(d[:, :Q_LORA_RANK], qg_ref[...]).astype(BF16)
    ckv = _rms(d[:, Q_LORA_RANK:Q_LORA_RANK + KV_LORA_RANK], kvg_ref[...])
    ckvb_ref[...] = ckv.astype(BF16)
    kr = d[:, Q_LORA_RANK + KV_LORA_RANK:]
    kr_ref[...] = kr

    @pl.when(pl.program_id(0) * x_ref.shape[0] < NP_TOK)
    def _():
        new_ckv_ref[...] = ckv
        new_kr_ref[...] = kr[:, :ROPE_DIM]


def _mla_down(x, mods, g, w_down, qg, kvg, layer):
    tm = 1024
    n_p = NP_TOK // tm
    n_down = Q_LORA_RANK + KV_LORA_RANK + LANE
    return pl.pallas_call(
        _mla_down_kernel,
        grid=(N_TOK // tm,),
        in_specs=[
            pl.BlockSpec((tm, D_MODEL), lambda i: (i, 0)),
            _mod_spec(layer, tm),
            pl.BlockSpec((1, D_MODEL), lambda i: (0, 0)),
            pl.BlockSpec((D_MODEL, n_down), lambda i: (0, 0)),
            pl.BlockSpec((1, Q_LORA_RANK), lambda i: (0, 0)),
            pl.BlockSpec((1, KV_LORA_RANK), lambda i: (0, 0)),
        ],
        out_specs=[
            pl.BlockSpec((tm, Q_LORA_RANK), lambda i: (i, 0)),
            pl.BlockSpec((tm, KV_LORA_RANK), lambda i: (i, 0)),
            pl.BlockSpec((tm, LANE), lambda i: (i, 0)),
            pl.BlockSpec((tm, KV_LORA_RANK), lambda i: (jnp.minimum(i, n_p - 1), 0)),
            pl.BlockSpec((tm, ROPE_DIM), lambda i: (jnp.minimum(i, n_p - 1), 0)),
        ],
        out_shape=[
            jax.ShapeDtypeStruct((N_TOK, Q_LORA_RANK), BF16),
            jax.ShapeDtypeStruct((N_TOK, KV_LORA_RANK), BF16),
            jax.ShapeDtypeStruct((N_TOK, LANE), F32),
            jax.ShapeDtypeStruct((NP_TOK, KV_LORA_RANK), F32),
            jax.ShapeDtypeStruct((NP_TOK, ROPE_DIM), F32),
        ],
        compiler_params=_cparams("arbitrary"),
        name="mla_down",
    )(x, mods, g, w_down, qg, kvg)


ATTN_BLOCK = 1024
ATTN_TQ = 256
HEAD_GROUP = 2
N_HEAD_GROUPS = N_HEADS // HEAD_GROUP
LOG2E = 1.4426950408889634
ATTN_VMEM_LIMIT = 60 * 1024 * 1024


def _attn_kernel(*refs, seq, n_cache, rope):
    it = iter(refs)
    n_src = 5 if n_cache else 3
    cur_refs = [next(it) for _ in range(n_src)]
    nxt_refs = [next(it) for _ in range(n_src)]
    wuq_ref, wukv_ref, gq_ref, gk_ref = next(it), next(it), next(it), next(it)
    if rope:
        cos_ref, sin_ref = next(it), next(it)
    o_ref, k_scr, v_scr, q_scr = next(it), next(it), next(it), next(it)

    inv_dim = 1.0 / QK_HEAD_DIM
    gq, gk = gq_ref[...], gk_ref[...]
    tables = (cos_ref[...], sin_ref[...]) if rope else None
    ones_new = jnp.ones((ATTN_BLOCK, LANE), BF16)
    ones_cache = jnp.ones((n_cache, LANE), BF16) if n_cache else None

    def latents(src_refs):
        vals = [r[...] for r in src_refs]
        if n_cache:
            vals[3] = vals[3].astype(BF16)
        return vals

    cur = latents(cur_refs)

    def normed(nope, rot2, g, tabs, out_scale):
        ssq = jnp.sum(nope * nope + 0.5 * (rot2 * rot2), axis=-1, keepdims=True)
        r = lax.rsqrt(ssq * inv_dim + EPS) * out_scale
        if tabs is None:
            rot = rot2 * g[1:2, :]
        else:
            rot = rot2 * (g[1:2, :] * tabs[0]) + pltpu.roll(rot2, ROPE_DIM, 1) * (g[2:3, :] * tabs[1])
        return jnp.concatenate([(nope * r * g[0:1, :]).astype(BF16), (rot * r).astype(BF16)], axis=1)

    def build(gi, slot, src):
        cq, ckv, kr = src[:3]
        for j in range(HEAD_GROUP):
            h = gi * HEAD_GROUP + j
            wukv = wukv_ref[h]
            kv = _dot(ckv, wukv)
            k_scr[slot, j, 0:ATTN_BLOCK, :] = normed(kv[:, :LANE], kr, gk, tables, 1.0)
            v_scr[slot, j, 0:ATTN_BLOCK, :] = jnp.concatenate([kv[:, LANE:].astype(BF16), ones_new], axis=1)
            if n_cache:
                kvc = _dot(src[3], wukv)
                k_scr[slot, j, ATTN_BLOCK:, :] = normed(kvc[:, :LANE], src[4], gk, None, 1.0)
                v_scr[slot, j, ATTN_BLOCK:, :] = jnp.concatenate([kvc[:, LANE:].astype(BF16), ones_cache], axis=1)
            q = _dot(cq, wuq_ref[h])
            q_scr[slot, j] = normed(q[:, :LANE], q[:, LANE:], gq, tables, QK_HEAD_DIM ** -0.5 * LOG2E)

    def attend(gi, slot):
        heads = []
        for j in range(HEAD_GROUP):
            outs = []
            for i in range(ATTN_BLOCK // ATTN_TQ):
                rows = slice(i * ATTN_TQ, (i + 1) * ATTN_TQ)
                keys = slice(None) if seq == ATTN_BLOCK else rows
                s = lax.dot_general(q_scr[slot, j, rows, :], k_scr[slot, j, keys, :], (((1,), (1,)), ((), ())),
                                    preferred_element_type=F32)
                p = jnp.exp2((s - jnp.max(s, axis=-1, keepdims=True)).astype(BF16))
                oe = _dot(p, v_scr[slot, j, keys, :])
                outs.append((oe[:, :LANE] / oe[:, LANE:]).astype(BF16))
            heads.append(jnp.concatenate(outs, axis=0))
        o_ref[gi] = jnp.concatenate(heads, axis=1)

    @pl.when(pl.program_id(0) == 0)
    def _():
        build(0, 0, cur)

    nxt = latents(nxt_refs)
    n_trips = N_HEAD_GROUPS // 2

    def two_groups(t, carry):
        g0 = 2 * t
        build(g0 + 1, 1, cur)
        attend(g0, 0)
        wraps = t == n_trips - 1
        build((g0 + 2) % N_HEAD_GROUPS, 0, [jnp.where(wraps, n, c) for n, c in zip(nxt, cur)])
        attend(g0 + 1, 1)
        return carry

    lax.fori_loop(0, n_trips, two_groups, 0)


def _attention(cq, ckv, kr2, cache, wuq, wukv, gq, gk, tables, *, tok0, n_tok, seq):
    assert seq in (ATTN_BLOCK, ATTN_TQ) and tok0 % ATTN_BLOCK == 0 and n_tok % ATTN_BLOCK == 0
    b0 = tok0 // ATTN_BLOCK
    n_cache = 0 if cache is None else cache[0].shape[1]
    assert n_cache == 0 or seq == ATTN_BLOCK
    rope = tables is not None
    n_blocks = n_tok // ATTN_BLOCK
    const2 = lambda b: (0, 0)
    const3 = lambda b: (0, 0, 0)
    in_specs, args = [], []
    resident = dict(pipeline_mode=pl.Buffered(1))
    for ahead in (0, 1):
        blk = lambda b, ahead=ahead: jnp.minimum(b + ahead, n_blocks - 1)
        mode = {}
        in_specs += [
            pl.BlockSpec((ATTN_BLOCK, Q_LORA_RANK), lambda b, blk=blk: (b0 + blk(b), 0), **mode),
            pl.BlockSpec((ATTN_BLOCK, KV_LORA_RANK), lambda b, blk=blk: (b0 + blk(b), 0), **mode),
            pl.BlockSpec((ATTN_BLOCK, LANE), lambda b, blk=blk: (b0 + blk(b), 0), **mode),
        ]
        args += [cq, ckv, kr2]
        if n_cache:
            in_specs += [pl.BlockSpec((None, n_cache, KV_LORA_RANK), lambda b, blk=blk: (blk(b), 0, 0), **mode),
                         pl.BlockSpec((None, n_cache, LANE), lambda b, blk=blk: (blk(b), 0, 0), **mode)]
            args += list(cache)
    in_specs += [
        pl.BlockSpec((N_HEADS, Q_LORA_RANK, HEAD_PAD), const3, **resident),
        pl.BlockSpec((N_HEADS, KV_LORA_RANK, HEAD_PAD), const3, **resident),
        pl.BlockSpec((SUBLANE, LANE), const2),
        pl.BlockSpec((SUBLANE, LANE), const2),
    ]
    args += [wuq, wukv, gq, gk]
    if rope:
        in_specs += [pl.BlockSpec((ATTN_BLOCK, LANE), const2, **resident),
                     pl.BlockSpec((ATTN_BLOCK, LANE), const2, **resident)]
        args += list(tables)
    return pl.pallas_call(
        functools.partial(_attn_kernel, seq=seq, n_cache=n_cache, rope=rope),
        grid=(n_blocks,),
        in_specs=in_specs,
        out_specs=pl.BlockSpec((N_HEAD_GROUPS, ATTN_BLOCK, HEAD_GROUP * V_HEAD_DIM), lambda b: (0, b, 0)),
        out_shape=jax.ShapeDtypeStruct((N_HEAD_GROUPS, n_tok, HEAD_GROUP * V_HEAD_DIM), BF16),
        scratch_shapes=[
            pltpu.VMEM((2, HEAD_GROUP, ATTN_BLOCK + n_cache, HEAD_PAD), BF16),
            pltpu.VMEM((2, HEAD_GROUP, ATTN_BLOCK + n_cache, HEAD_PAD), BF16),
            pltpu.VMEM((2, HEAD_GROUP, ATTN_BLOCK, HEAD_PAD), BF16),
        ],
        compiler_params=pltpu.CompilerParams(dimension_semantics=("arbitrary",), vmem_limit_bytes=ATTN_VMEM_LIMIT),
        name="attn_rope" if rope else "attn",
    )(*args)


ROUTE_TM = 4 * TOK_CHUNK
ROUTE_E1, ROUTE_E2, ROUTE_RANK1, ROUTE_RANK2 = N_EXPERTS, N_EXPERTS + 1, N_EXPERTS + 2, N_EXPERTS + 3
ROUTE_ROWS = 16


def _attn_out_kernel(op_ref, os_ref, x_ref, mod_ref, g_ref, wo_ref, wr_ref,
                     x3_ref, h_ref, route_ref, route_t_ref, cstart_ref, total_ref, carry_ref):
    i = pl.program_id(0)
    mod = mod_ref[...]
    is_prompt = i * x_ref.shape[0] < NP_TOK

    @pl.when(i == 0)
    def _():
        carry_ref[...] = jnp.zeros_like(carry_ref)

    wh, wl = _split_bf16(wr_ref[...])
    lane = lax.broadcasted_iota(I32, (TOK_CHUNK, LANE), 1)
    lanef = lane.astype(F32)
    neg = jnp.float32(-jnp.inf)
    r_id = lax.broadcasted_iota(I32, (TOK_CHUNK, TOK_CHUNK), 0)
    c_id = lax.broadcasted_iota(I32, (TOK_CHUNK, TOK_CHUNK), 1)
    tri = jnp.where(c_id < r_id, 1.0, 0.0).astype(BF16)
    seen = carry_ref[0:1, :]

    for k in range(x_ref.shape[0] // TOK_CHUNK):
        rows = slice(k * TOK_CHUNK, (k + 1) * TOK_CHUNK)
        att = _dot(jnp.where(is_prompt, op_ref[0, rows, :], os_ref[0, rows, :]), wo_ref[0])
        for gi in range(1, N_HEAD_GROUPS):
            att += _dot(jnp.where(is_prompt, op_ref[gi, rows, :], os_ref[gi, rows, :]), wo_ref[gi])
        x3 = x_ref[rows, :] + _mod_part(mod, 2) * att
        x3_ref[rows, :] = x3
        h = _rms(x3, g_ref[...]) * (1.0 + _mod_part(mod, 4)) + _mod_part(mod, 3)
        hb = h.astype(BF16)
        h_ref[rows, :] = hb
        hl = (h - hb.astype(F32)).astype(BF16)
        logits = _dot(hb, wh) + (_dot(hl, wh) + _dot(hb, wl))
        logits = jnp.where(lane < N_EXPERTS, logits, neg)
        v1 = jnp.max(logits, axis=-1, keepdims=True)
        i1 = jnp.min(jnp.where(logits == v1, lanef, float(LANE)), axis=-1, keepdims=True)
        rest = jnp.where(lanef == i1, neg, logits)
        v2 = jnp.max(rest, axis=-1, keepdims=True)
        i2 = jnp.min(jnp.where(rest == v2, lanef, float(LANE)), axis=-1, keepdims=True)
        e2 = jnp.exp(v2 - v1)
        w1 = 1.0 / (1.0 + e2)
        hot1, hot2 = lanef == i1, lanef == i2
        gates = jnp.where(hot1, w1, 0.0) + jnp.where(hot2, e2 * w1, 0.0)

        hot = jnp.where(hot1 | hot2, 1.0, 0.0)
        before = _dot(tri, hot.astype(BF16)) + seen
        rank1 = jnp.sum(jnp.where(hot1, before, 0.0), axis=-1, keepdims=True)
        rank2 = jnp.sum(jnp.where(hot2, before, 0.0), axis=-1, keepdims=True)
        route = jnp.where(lane == ROUTE_E1, i1, jnp.where(lane == ROUTE_E2, i2, jnp.where(
            lane == ROUTE_RANK1, rank1, jnp.where(lane == ROUTE_RANK2, rank2, gates))))
        route_ref[rows, :] = route
        route_t_ref[:, rows] = route.T[:ROUTE_ROWS, :]
        cstart_ref[k] = jnp.broadcast_to(seen, (SUBLANE, LANE))
        seen = before[TOK_CHUNK - 1:TOK_CHUNK, :] + hot[TOK_CHUNK - 1:TOK_CHUNK, :]

    carry_ref[...] = jnp.broadcast_to(seen, (SUBLANE, LANE))
    total_ref[...] = jnp.broadcast_to(seen, (SUBLANE, LANE))


def _attn_out(o_p, o_s, x, mods, g, wo, wr, layer):
    tm = ROUTE_TM
    per = tm // TOK_CHUNK
    n_p = NP_TOK // tm
    o_block = (N_HEAD_GROUPS, tm, HEAD_GROUP * V_HEAD_DIM)
    return pl.pallas_call(
        _attn_out_kernel,
        grid=(N_TOK // tm,),
        in_specs=[
            pl.BlockSpec(o_block, lambda i: (0, jnp.minimum(i, n_p - 1), 0)),
            pl.BlockSpec(o_block, lambda i: (0, jnp.maximum(i - n_p, 0), 0)),
            pl.BlockSpec((tm, D_MODEL), lambda i: (i, 0)),
            _mod_spec(layer, tm),
            pl.BlockSpec((1, D_MODEL), lambda i: (0, 0)),
            pl.BlockSpec((N_HEAD_GROUPS, HEAD_GROUP * V_HEAD_DIM, D_MODEL), lambda i: (0, 0, 0)),
            pl.BlockSpec((D_MODEL, LANE), lambda i: (0, 0)),
        ],
        out_specs=[
            pl.BlockSpec((tm, D_MODEL), lambda i: (i, 0)),
            pl.BlockSpec((tm, D_MODEL), lambda i: (i, 0)),
            pl.BlockSpec((tm, LANE), lambda i: (i, 0)),
            pl.BlockSpec((ROUTE_ROWS, tm), lambda i: (0, i)),
            pl.BlockSpec((per, SUBLANE, LANE), lambda i: (i, 0, 0)),
            pl.BlockSpec((SUBLANE, LANE), lambda i: (0, 0)),
        ],
        out_shape=[
            jax.ShapeDtypeStruct((N_TOK, D_MODEL), F32),
            jax.ShapeDtypeStruct((N_TOK, D_MODEL), BF16),
            jax.ShapeDtypeStruct((N_TOK, LANE), F32),
            jax.ShapeDtypeStruct((ROUTE_ROWS, N_TOK), F32),
            jax.ShapeDtypeStruct((N_CHUNKS, SUBLANE, LANE), F32),
            jax.ShapeDtypeStruct((SUBLANE, LANE), F32),
        ],
        scratch_shapes=[pltpu.VMEM((SUBLANE, LANE), F32)],
        compiler_params=_cparams("arbitrary"),
        name="attn_out",
    )(o_p, o_s, x, mods, g, wo, wr)


def _routing_tables(route_t, cstart, total):
    counts = total[0, :N_EXPERTS].astype(I32)
    padded = (counts + SLOT_TILE - 1) // SLOT_TILE * SLOT_TILE
    ends = jnp.cumsum(padded)
    offs = ends - padded
    expert_ids = jnp.arange(N_EXPERTS, dtype=I32)[:, None]

    def region_start(e_row):
        return jnp.sum(jnp.where(e_row[None, :].astype(I32) == expert_ids, offs[:, None], 0), axis=0)

    slot1 = region_start(route_t[ROUTE_E1]) + route_t[ROUTE_RANK1].astype(I32)
    slot2 = region_start(route_t[ROUTE_E2]) + route_t[ROUTE_RANK2].astype(I32)

    n_active = ends[-1] // SLOT_TILE
    tile_start = jnp.arange(N_SLOT_TILES, dtype=I32) * SLOT_TILE
    tile_expert = jnp.sum(tile_start[:, None] >= ends[None, :], axis=1).astype(I32)
    last_expert = jnp.sum((n_active - 1) * SLOT_TILE >= ends).astype(I32)
    tile_active = tile_start < ends[-1]
    tile_expert = jnp.where(tile_active, tile_expert, last_expert)
    prev_expert = jnp.concatenate([jnp.full((1,), -1, I32), tile_expert[:-1]])
    tile_first = (tile_active & (tile_expert != prev_expert)).astype(I32)
    weight_slot = ((jnp.cumsum(tile_first) - 1) % 2).astype(I32)
    later = (expert_ids.T > expert_ids) & (counts > 0)[None, :]
    next_of = jnp.min(jnp.where(later, expert_ids.T, N_EXPERTS), axis=1)
    next_of = jnp.where(next_of == N_EXPERTS, -1, next_of).astype(I32)
    next_expert = jnp.sum(jnp.where(tile_expert[:, None] == expert_ids.T, next_of[None, :], 0), axis=1).astype(I32)

    cc = jnp.concatenate([cstart[:, 0, :N_EXPERTS], total[0:1, :N_EXPERTS]]).astype(I32)

    g_start = jnp.arange(N_SLOTS // GATHER_TILE, dtype=I32) * GATHER_TILE
    g_expert = jnp.minimum(jnp.sum(g_start[:, None] >= ends[None, :], axis=1), N_EXPERTS - 1).astype(I32)
    g_hot = (g_expert[None, :] == expert_ids).astype(I32)
    rank0 = g_start - jnp.sum(g_hot * offs[:, None], axis=0)
    cc_tile = jnp.sum(cc[:, :, None] * g_hot[None, :, :], axis=1)
    c_lo = jnp.sum(cc_tile[1:] <= rank0[None, :], axis=0).astype(I32)
    rank_end = jnp.minimum(rank0 + GATHER_TILE, jnp.sum(g_hot * counts[:, None], axis=0))
    c_hi = jnp.sum(cc_tile[:-1] < rank_end[None, :], axis=0).astype(I32) - 1
    idle = (g_start >= ends[-1]) | (c_hi < c_lo)
    c_lo = jnp.where(idle, 1, c_lo)
    c_hi = jnp.where(idle, 0, c_hi)

    lo = offs[None, :] + cc[:-1]
    hi = offs[None, :] + cc[1:]
    first, last = lo // SLOT_CHUNK, (hi - 1) // SLOT_CHUNK
    ids = jnp.concatenate([first, last], axis=1)
    valid = jnp.concatenate([hi > lo, (hi > lo) & (last != first)], axis=1)
    experts = jnp.tile(jnp.arange(N_EXPERTS, dtype=I32), (N_CHUNKS, 2))
    dest = jnp.cumsum(valid, axis=1) - 1
    place = (valid[:, :, None] & (dest[:, :, None] == jnp.arange(MAX_PAIRS, dtype=I32)[None, None, :])).astype(I32)
    ids = jnp.sum(ids[:, :, None] * place, axis=1).astype(I32)
    experts = jnp.sum(experts[:, :, None] * place, axis=1).astype(I32)
    n_pairs = jnp.sum(valid, axis=1).astype(I32)
    unused = jnp.arange(MAX_PAIRS, dtype=I32)[None, :] >= n_pairs[:, None]
    ids = jnp.where(unused, ids[:, 0:1], ids)
    experts = jnp.where(unused, -1, experts)
    n_pairs = n_pairs + n_pairs % 2

    pad = jnp.zeros((N_CHUNKS, SUBLANE - TOP_K, TOK_CHUNK), I32)
    slots_lane = jnp.concatenate([slot1.reshape(N_CHUNKS, 1, TOK_CHUNK), slot2.reshape(N_CHUNKS, 1, TOK_CHUNK), pad],
                                 axis=1)
    offs_row = jnp.concatenate([offs.astype(F32), jnp.zeros((LANE - N_EXPERTS,), F32)]).reshape(1, LANE)
    return dict(tile_expert=tile_expert, n_active=n_active.reshape(1).astype(I32), tile_first=tile_first,
                weight_slot=weight_slot, next_expert=next_expert, c_lo=c_lo, c_hi=c_hi,
                ids=ids.reshape(-1), experts=experts.reshape(-1), n_pairs=n_pairs,
                slots_lane=slots_lane, offs_row=offs_row)


GATHER_TILE = 256
GATHER_UNROLL = 3


def _gather_kernel(clo_ref, chi_ref, slots_ref, h_ref, o_ref, acc_ref):
    g = pl.program_id(0)
    slot_id = g * GATHER_TILE + lax.broadcasted_iota(I32, (GATHER_TILE, TOK_CHUNK), 0)
    acc_ref[...] = jnp.zeros_like(acc_ref)

    c_lo, c_hi = clo_ref[g], chi_ref[g]

    def one_hot(c, value):
        sl = slots_ref[c]
        hit = (sl[0:1, :] == slot_id) | (sl[1:2, :] == slot_id)
        return jnp.where(hit, value, 0.0).astype(BF16)

    def rows(c):
        return h_ref[pl.ds(pl.multiple_of(c * TOK_CHUNK, TOK_CHUNK), TOK_CHUNK), :]

    def body(t, carry):
        c0 = c_lo + GATHER_UNROLL * t
        total = _dot(one_hot(c0, 1.0), rows(c0))
        for k in range(1, GATHER_UNROLL):
            live = jnp.where(c0 + k <= c_hi, 1.0, 0.0)
            ck = jnp.minimum(c0 + k, c_hi)
            total += _dot(one_hot(ck, live), rows(ck))
        acc_ref[...] += total
        return carry

    lax.fori_loop(0, (c_hi - c_lo + GATHER_UNROLL) // GATHER_UNROLL, body, 0)
    o_ref[...] = acc_ref[...].astype(BF16)


def _gather(h, rt):
    return pl.pallas_call(
        _gather_kernel,
        grid_spec=pltpu.PrefetchScalarGridSpec(
            num_scalar_prefetch=2,
            grid=(N_SLOTS // GATHER_TILE,),
            in_specs=[
                pl.BlockSpec((N_CHUNKS, SUBLANE, TOK_CHUNK), lambda g, *_: (0, 0, 0)),
                pl.BlockSpec((N_TOK, D_MODEL), lambda g, *_: (0, 0), pipeline_mode=pl.Buffered(1)),
            ],
            out_specs=pl.BlockSpec((GATHER_TILE, D_MODEL), lambda g, *_: (g, 0)),
            scratch_shapes=[pltpu.VMEM((GATHER_TILE, D_MODEL), F32)],
        ),
        out_shape=jax.ShapeDtypeStruct((N_SLOTS, D_MODEL), BF16),
        compiler_params=_cparams("arbitrary"),
        name="moe_gather",
    )(rt["c_lo"], rt["c_hi"], rt["slots_lane"], h)


def _experts_kernel(te_ref, na_ref, first_ref, nxt_ref, ws_ref, h_ref, w1_hbm, w3_hbm, w2_hbm, o_ref,
                    w1_buf, w3_buf, w2_buf, sem):
    g = pl.program_id(0)
    slot = ws_ref[g]

    def weight_copies(e, s):
        return [pltpu.make_async_copy(w1_hbm.at[e], w1_buf.at[s], sem.at[s, 0]),
                pltpu.make_async_copy(w3_hbm.at[e], w3_buf.at[s], sem.at[s, 1]),
                pltpu.make_async_copy(w2_hbm.at[e], w2_buf.at[s], sem.at[s, 2])]

    @pl.when(g == 0)
    def _():
        for cp in weight_copies(te_ref[0], 0):
            cp.start()

    @pl.when(first_ref[g] == 1)
    def _():
        for cp in weight_copies(te_ref[g], slot):
            cp.wait()

        @pl.when(nxt_ref[g] >= 0)
        def _():
            for cp in weight_copies(nxt_ref[g], 1 - slot):
                cp.start()

    @pl.when(g < na_ref[0])
    def _():
        h = h_ref[...].astype(w1_buf.dtype)
        y = None
        for j in range(D_FF_EXPERT // FFN_CHUNK):
            cols = slice(j * FFN_CHUNK, (j + 1) * FFN_CHUNK)
            t = _silu(_dot(h, w1_buf[slot, :, cols])) * _dot(h, w3_buf[slot, :, cols])
            part = _dot(t.astype(w2_buf.dtype), w2_buf[slot, cols, :])
            y = part if y is None else y + part
        o_ref[...] = y.astype(BF16)

    @pl.when(g >= na_ref[0])
    def _():
        o_ref[...] = jnp.zeros_like(o_ref)


def _experts(hs, rt, w1, w3, w2):
    tile = lambda g, te, na, *_: (jnp.minimum(g, na[0] - 1), 0)
    return pl.pallas_call(
        _experts_kernel,
        grid_spec=pltpu.PrefetchScalarGridSpec(
            num_scalar_prefetch=5,
            grid=(N_SLOT_TILES,),
            in_specs=[
                pl.BlockSpec((SLOT_TILE, D_MODEL), tile),
                pl.BlockSpec(memory_space=pl.ANY),
                pl.BlockSpec(memory_space=pl.ANY),
                pl.BlockSpec(memory_space=pl.ANY),
            ],
            out_specs=pl.BlockSpec((SLOT_TILE, D_MODEL), lambda g, *_: (g, 0)),
            scratch_shapes=[
                pltpu.VMEM((2, D_MODEL, D_FF_EXPERT), w1.dtype),
                pltpu.VMEM((2, D_MODEL, D_FF_EXPERT), w3.dtype),
                pltpu.VMEM((2, D_FF_EXPERT, D_MODEL), w2.dtype),
                pltpu.SemaphoreType.DMA((2, 3)),
            ],
        ),
        out_shape=jax.ShapeDtypeStruct((N_SLOTS, D_MODEL), BF16),
        compiler_params=_cparams("arbitrary"),
        name="moe_experts",
    )(rt["tile_expert"], rt["n_active"], rt["tile_first"], rt["next_expert"], rt["weight_slot"], hs, w1, w3, w2)


def _combine_kernel(np_ref, ids_ref, ex_ref, route_ref, offs_ref, x_ref, mod_ref, y_hbm,
                    op_ref, os_ref, buf_ref, acc_ref, sem):
    c = pl.program_id(0)
    cur = c % 2

    def chunk_copy(step, j, half):
        chunk = ids_ref[step * MAX_PAIRS + j]
        src = y_hbm.at[pl.ds(pl.multiple_of(chunk * SLOT_CHUNK, SLOT_CHUNK), SLOT_CHUNK)]
        return pltpu.make_async_copy(src, buf_ref.at[half, j], sem.at[half, j])

    def start_all(step, half):
        def go(j, carry):
            chunk_copy(step, j, half).start()
            return carry

        lax.fori_loop(0, np_ref[step], go, 0)

    @pl.when(c == 0)
    def _():
        start_all(0, 0)

    @pl.when(c + 1 < pl.num_programs(0))
    def _():
        start_all(c + 1, 1 - cur)

    route = route_ref[...]
    lanef = lax.broadcasted_iota(I32, route.shape, 1).astype(F32)
    offs = offs_ref[...]

    def slot_of(e_lane, rank_lane):
        start = jnp.sum(jnp.where(lanef == route[:, e_lane:e_lane + 1], offs, 0.0), axis=-1, keepdims=True)
        return start + route[:, rank_lane:rank_lane + 1]

    col = lax.broadcasted_iota(I32, (TOK_CHUNK, SLOT_CHUNK), 1).astype(F32)
    s1 = slot_of(ROUTE_E1, ROUTE_RANK1) - col
    s2 = slot_of(ROUTE_E2, ROUTE_RANK2) - col
    acc_ref[...] = jnp.zeros_like(acc_ref)

    def part(j):
        pair = c * MAX_PAIRS + j
        base = (ids_ref[pair] * SLOT_CHUNK).astype(F32)
        hit = (s1 == base) | (s2 == base)
        rows = _dot(jnp.where(hit, 1.0, 0.0).astype(BF16), buf_ref[cur, j])
        return _lane_pick(route, ex_ref[pair]) * rows

    def body(t, carry):
        chunk_copy(c, 2 * t, cur).wait()
        chunk_copy(c, 2 * t + 1, cur).wait()
        acc_ref[...] += part(2 * t) + part(2 * t + 1)
        return carry

    lax.fori_loop(0, np_ref[c] // 2, body, 0)
    res = x_ref[...] + _mod_part(mod_ref[...], 5) * acc_ref[...]

    @pl.when(c * TOK_CHUNK < NP_TOK)
    def _():
        op_ref[...] = res

    @pl.when(c * TOK_CHUNK >= NP_TOK)
    def _():
        os_ref[...] = res


def _combine(ys, route, x, mods, rt, layer):
    tok = lambda c, *_: (c, 0)
    return pl.pallas_call(
        _combine_kernel,
        grid_spec=pltpu.PrefetchScalarGridSpec(
            num_scalar_prefetch=3,
            grid=(N_CHUNKS,),
            in_specs=[
                pl.BlockSpec((TOK_CHUNK, LANE), tok),
                pl.BlockSpec((1, LANE), lambda c, *_: (0, 0)),
                pl.BlockSpec((TOK_CHUNK, D_MODEL), tok),
                _mod_spec(layer, TOK_CHUNK),
                pl.BlockSpec(memory_space=pl.ANY),
            ],
            out_specs=_split_specs(TOK_CHUNK),
            scratch_shapes=[
                pltpu.VMEM((2, MAX_PAIRS, SLOT_CHUNK, D_MODEL), BF16),
                pltpu.VMEM((TOK_CHUNK, D_MODEL), F32),
                pltpu.SemaphoreType.DMA((2, MAX_PAIRS)),
            ],
        ),
        out_shape=[jax.ShapeDtypeStruct((NP_TOK, D_MODEL), F32), jax.ShapeDtypeStruct((NS_TOK, D_MODEL), F32)],
        compiler_params=_cparams("arbitrary"),
        name="moe_combine",
    )(rt["n_pairs"], rt["ids"], rt["experts"], route, rt["offs_row"], x, mods, ys)


def _rope_partner(t):
    half = AXIS_ROPE_DIM // 2
    s = t.shape[:-1]
    return t.reshape(s + (2, 2, half))[..., ::-1, :].reshape(s + (ROPE_DIM,))


def _rope_tables(n_tokens):
    rows = n_tokens // GRID_W
    row = np.repeat(np.arange(rows), GRID_W).astype(np.float32)
    col = np.tile(np.arange(GRID_W), rows).astype(np.float32)
    inv = (ROPE_BASE ** (-np.arange(0, AXIS_ROPE_DIM, 2, dtype=np.float32) / AXIS_ROPE_DIM)).astype(np.float32)
    ar, ac = row[:, None] * inv, col[:, None] * inv
    cos = np.concatenate([np.cos(ar), np.cos(ar), np.cos(ac), np.cos(ac)], axis=-1)
    sin = np.concatenate([-np.sin(ar), np.sin(ar), -np.sin(ac), np.sin(ac)], axis=-1)
    zeros = np.zeros_like(cos)
    return (jnp.asarray(np.concatenate([cos, zeros], axis=-1), F32),
            jnp.asarray(np.concatenate([sin, zeros], axis=-1), F32))


def _qk_gain_rows(g):
    z = jnp.zeros((ROPE_DIM,), F32)
    rows = jnp.stack([g[:QK_NOPE_DIM],
                      jnp.concatenate([g[QK_NOPE_DIM:], z]),
                      jnp.concatenate([_rope_partner(g[QK_NOPE_DIM:]), z])])
    return jnp.concatenate([rows, jnp.zeros((SUBLANE - 3, LANE), F32)])


def kernel(x_prompt, x_sample, c, cache_ckv, cache_krope, c_ctx, ada_w, ada_b, norm1_g, norm2_g, conv_pw1, conv_dw, conv_dw_b, conv_ln_g, conv_ln_b, conv_pw2, ffn_w1, ffn_w3, ffn_w2, mla_wdq, mla_q_norm_g, mla_wuq, mla_wdkv, mla_kv_norm_g, mla_wukv, mla_q_qk_g, mla_k_qk_g, mla_wo, moe_router, moe_w1, moe_w3, moe_w2):
    xp, xs = x_prompt.reshape(NP_TOK, D_MODEL), x_sample.reshape(NS_TOK, D_MODEL)
    cond16 = jnp.concatenate([c_ctx[None, :], c, jnp.zeros((MOD_ROWS - 1 - DEC_BATCH, D_MODEL), F32)])
    mods = _adaln(cond16, ada_w, ada_b).reshape(2, MOD_ROWS, 1, N_MOD * D_MODEL)
    vec = lambda a: a.reshape(1, -1)

    u = _glu(xp, xs, mods, vec(norm1_g[0]), conv_pw1[0].astype(BF16), 0)
    dw = jnp.concatenate([conv_dw[0], jnp.zeros((1, D_MODEL), F32)])
    x = _conv(u, xp, xs, mods, dw, vec(conv_dw_b[0]), vec(conv_ln_g[0]), vec(conv_ln_b[0]),
              conv_pw2[0].astype(BF16), 0)
    x = _ffn(x, mods, vec(norm2_g[0]), ffn_w1[0].astype(BF16), ffn_w3[0].astype(BF16),
             ffn_w2[0].astype(BF16), 0)

    wdkv = mla_wdkv[0]
    w_down = jnp.concatenate([mla_wdq[0], wdkv, _rope_partner(wdkv[:, KV_LORA_RANK:])], axis=1).astype(BF16)
    cq, ckv_b, kr2, new_ckv, new_krope = _mla_down(x, mods, vec(norm1_g[1]), w_down, vec(mla_q_norm_g[0]),
                                                   vec(mla_kv_norm_g[0]), 1)

    wuq = mla_wuq[0].reshape(Q_LORA_RANK, N_HEADS, QK_HEAD_DIM)
    wuq = jnp.concatenate([wuq, _rope_partner(wuq[..., QK_NOPE_DIM:])], axis=-1)
    wuq = wuq.transpose(1, 0, 2).astype(BF16)
    wukv = mla_wukv[0].reshape(KV_LORA_RANK, N_HEADS, HEAD_PAD).transpose(1, 0, 2).astype(BF16)
    gq, gk = _qk_gain_rows(mla_q_qk_g[0]), _qk_gain_rows(mla_k_qk_g[0])
    ckr = cache_krope[:, 0]
    cache = (cache_ckv[:, 0], jnp.concatenate([ckr, _rope_partner(ckr)], axis=-1))
    o_p = _attention(cq, ckv_b, kr2, None, wuq, wukv, gq, gk, None, tok0=0, n_tok=NP_TOK, seq=SEQ)
    o_s = _attention(cq, ckv_b, kr2, cache, wuq, wukv, gq, gk, _rope_tables(DEC_SEQ),
                     tok0=NP_TOK, n_tok=NS_TOK, seq=DEC_SEQ)

    wr = jnp.concatenate([moe_router[0], jnp.zeros((D_MODEL, LANE - N_EXPERTS), F32)], axis=1)
    wo = mla_wo[0].astype(BF16).reshape(N_HEAD_GROUPS, HEAD_GROUP * V_HEAD_DIM, D_MODEL)
    x, h, route, route_t, cstart, total = _attn_out(o_p, o_s, x, mods, vec(norm2_g[1]), wo, wr, 1)
    rt = _routing_tables(route_t, cstart, total)
    hs = _gather(h, rt)
    ys = _experts(hs, rt, moe_w1[0], moe_w3[0], moe_w2[0])
    yp, ysamp = _combine(ys, route, x, mods, rt, 1)

    return (yp.reshape(BATCH, SEQ, D_MODEL), ysamp.reshape(DEC_BATCH, DEC_SEQ, D_MODEL),
            new_ckv.reshape(BATCH, 1, SEQ, KV_LORA_RANK), new_krope.reshape(BATCH, 1, SEQ, ROPE_DIM))
```
